```python
import math
import jax, jax.numpy as jnp
from jax import lax
import numpy as np

D_MODEL = 4096
BATCH = 16
SEQ = 2048
DEPTH = 1

D_FF = 11008
ATTN_HEAD_DIM = 128
N_ATTN_HEADS = D_MODEL // (2 * ATTN_HEAD_DIM)
D_ATTN = N_ATTN_HEADS * ATTN_HEAD_DIM
DILATED_CONFIGS = ((128, 1), (512, 4), (2048, 16))
ATTN_BLOCK = 128
DN_HEAD_DIM = 128
N_DN_HEADS = D_MODEL // (2 * DN_HEAD_DIM)
D_DN = N_DN_HEADS * DN_HEAD_DIM
CONV_WIDTH = 4
CHUNK = 64
D_MIX = D_ATTN + D_DN
IN_SPLITS = (D_ATTN, D_ATTN, D_ATTN, 3 * D_DN, D_DN, N_DN_HEADS, N_DN_HEADS)
D_IN_PROJ = sum(IN_SPLITS)
EPS = 1e-6

kernel_name = "hymba_dilated_swa_gated_deltanet_macaron"


def _rmsnorm(x, w):
    xf = x.astype(jnp.float32)
    y = xf * lax.rsqrt(jnp.mean(xf * xf, axis=-1, keepdims=True) + EPS)
    return (y * w.astype(jnp.float32)).astype(x.dtype)


def _swiglu(h, w_gate, w_up, w_down):
    return (jax.nn.silu(h @ w_gate) * (h @ w_up)) @ w_down


def _band_attention(q, k, v, steps):
    G, L, H, Dh = q.shape
    nb = -(-L // ATTN_BLOCK)
    lp = nb * ATTN_BLOCK
    qb = jnp.pad(q, ((0, 0), (0, lp - L), (0, 0), (0, 0))).reshape(G, nb, ATTN_BLOCK, H, Dh)

    def band(t):
        t = jnp.pad(t, ((0, 0), (ATTN_BLOCK, lp - L), (0, 0), (0, 0)))
        t = t.reshape(G, nb + 1, ATTN_BLOCK, H, Dh)
        return jnp.concatenate([t[:, :-1], t[:, 1:]], axis=2)

    kw, vw = band(k), band(v)
    s = jnp.einsum('gnqhd,gnkhd->gnhqk', qb, kw, preferred_element_type=jnp.float32) * (Dh ** -0.5)
    i = jnp.arange(ATTN_BLOCK)[:, None]
    j = jnp.arange(2 * ATTN_BLOCK)[None, :]
    dist = i + ATTN_BLOCK - j
    kpos = jnp.arange(nb)[:, None, None] * ATTN_BLOCK - ATTN_BLOCK + j
    valid = (dist >= 0) & (dist <= steps) & (kpos >= 0)
    s = jnp.where(valid[:, None], s, -jnp.inf)
    m = jnp.max(s, axis=-1, keepdims=True)
    p = jnp.exp(s - m)
    den = jnp.sum(p, axis=-1)
    num = jnp.einsum('gnhqk,gnkhd->gnqhd', p, vw.astype(jnp.float32))
    m = m[..., 0].transpose(0, 1, 3, 2).reshape(G, lp, H)[:, :L]
    den = den.transpose(0, 1, 3, 2).reshape(G, lp, H)[:, :L]
    num = num.reshape(G, lp, H, Dh)[:, :L]
    return m, num, den


def _dilated_attention(q, k, v):
    B, S, H, Dh = q.shape
    ms, nums, dens = [], [], []
    for window, d in DILATED_CONFIGS:
        L = S // d

        def to_res(t):
            return t.reshape(B, L, d, H, Dh).transpose(0, 2, 1, 3, 4).reshape(B * d, L, H, Dh)

        def from_res(t):
            rest = t.shape[2:]
            return jnp.swapaxes(t.reshape(B, d, L, *rest), 1, 2).reshape(B, S, *rest)

        m, num, den = _band_attention(to_res(q), to_res(k), to_res(v), window // d)
        ms.append(from_res(m)); nums.append(from_res(num)); dens.append(from_res(den))
    m_all = jnp.stack(ms)
    w = jnp.exp(m_all - jnp.max(m_all, axis=0, keepdims=True))
    num = jnp.sum(w[..., None] * jnp.stack(nums), axis=0)
    den = jnp.sum(w * jnp.stack(dens), axis=0)
    return num / den[..., None]


def _causal_conv(x, w):
    S = x.shape[1]
    xp = jnp.pad(x, ((0, 0), (CONV_WIDTH - 1, 0), (0, 0)))
    return sum(w[i] * xp[:, i:i + S] for i in range(CONV_WIDTH))


def _gated_delta_rule(q, k, v, g, beta):
    B, S, H, Dk = q.shape
    Dv = v.shape[-1]
    N = S // CHUNK

    def chunks(t):
        return jnp.swapaxes(t.reshape(B, N, CHUNK, H, *t.shape[3:]), 2, 3)

    q, k, v, g, beta = (chunks(t.astype(jnp.float32)) for t in (q, k, v, g, beta))
    gc = jnp.cumsum(g, axis=-1)
    idx = jnp.arange(CHUNK)
    incl = idx[:, None] >= idx[None, :]
    strict = idx[:, None] > idx[None, :]
    decay = jnp.exp(jnp.where(incl, gc[..., :, None] - gc[..., None, :], -jnp.inf))
    kb = k * beta[..., None]
    kk = jnp.einsum('bnhid,bnhjd->bnhij', kb, k)
    a = jnp.where(strict, kk * decay, 0.0) + jnp.eye(CHUNK, dtype=jnp.float32)
    rhs = jnp.concatenate([kb * jnp.exp(gc)[..., None], v * beta[..., None]], axis=-1)
    sol = lax.linalg.triangular_solve(a, rhs, left_side=True, lower=True, unit_diagonal=True)
    w_c, u_c = sol[..., :Dk], sol[..., Dk:]
    qk = jnp.einsum('bnhid,bnhjd->bnhij', q, k) * decay
    q_dec = q * jnp.exp(gc)[..., None]
    k_dec = k * jnp.exp(gc[..., -1:] - gc)[..., None]
    g_last = jnp.exp(gc[..., -1])

    def step(state, xs):
        wc, uc, qkc, qdc, kdc, glc = xs
        v_new = uc - jnp.einsum('bhcd,bhdv->bhcv', wc, state)
        o = jnp.einsum('bhcd,bhdv->bhcv', qdc, state) + jnp.einsum('bhij,bhjv->bhiv', qkc, v_new)
        state = state * glc[..., None, None] + jnp.einsum('bhcd,bhcv->bhdv', kdc, v_new)
        return state, o

    xs = tuple(jnp.moveaxis(t, 1, 0) for t in (w_c, u_c, qk, q_dec, k_dec, g_last))
    state0 = jnp.zeros((B, H, Dk, Dv), jnp.float32)
    _, o = lax.scan(step, state0, xs)
    return o.transpose(1, 0, 3, 2, 4).reshape(B, S, H, Dv)


def _hybrid_mixer(h, w_in, conv_w, a_log, dt_bias, dn_norm, w_out):
    B, S, _ = h.shape
    proj = h @ w_in
    cuts = [int(c) for c in np.cumsum(IN_SPLITS)[:-1]]
    aq, ak, av, dqkv, dz, db, da = jnp.split(proj, cuts, axis=-1)
    heads_a = lambda t: t.reshape(B, S, N_ATTN_HEADS, ATTN_HEAD_DIM)
    attn = _dilated_attention(heads_a(aq), heads_a(ak), heads_a(av))
    attn = attn.reshape(B, S, D_ATTN).astype(h.dtype)
    dqkv = jax.nn.silu(_causal_conv(dqkv, conv_w))
    dq, dk, dv = jnp.split(dqkv, 3, axis=-1)
    heads_b = lambda t: t.reshape(B, S, N_DN_HEADS, DN_HEAD_DIM).astype(jnp.float32)
    dq, dk, dv = heads_b(dq), heads_b(dk), heads_b(dv)
    l2 = lambda t: t * lax.rsqrt(jnp.sum(t * t, axis=-1, keepdims=True) + EPS)
    dq = l2(dq) * (DN_HEAD_DIM ** -0.5)
    dk = l2(dk)
    beta = jax.nn.sigmoid(db.astype(jnp.float32))
    g = -jnp.exp(a_log.astype(jnp.float32)) * jax.nn.softplus(da.astype(jnp.float32) + dt_bias.astype(jnp.float32))
    o = _gated_delta_rule(dq, dk, dv, g, beta)
    o = o * lax.rsqrt(jnp.mean(o * o, axis=-1, keepdims=True) + EPS) * dn_norm.astype(jnp.float32)
    o = o * jax.nn.silu(heads_b(dz))
    dn = o.reshape(B, S, D_DN).astype(h.dtype)
    return jnp.concatenate([attn, dn], axis=-1) @ w_out


def _fwd_setup_inputs(seed: int = 0) -> dict:
    key = jax.random.key(seed)
    ks = jax.random.split(key, 20)
    f32 = jnp.float32
    nrm = lambda k, shape, fan_in: jax.random.normal(k, shape, f32) * fan_in ** -0.5
    gain = lambda k, n: 1.0 + 0.05 * jax.random.normal(k, (n,), f32)
    dt = jnp.exp(jax.random.uniform(ks[9], (N_DN_HEADS,), f32, math.log(1e-3), math.log(1e-1)))
    return {
        "x": jax.random.normal(ks[0], (BATCH, SEQ, D_MODEL), f32),
        "ffn1_norm": gain(ks[1], D_MODEL),
        "ffn1_w_gate": nrm(ks[2], (D_MODEL, D_FF), D_MODEL),
        "ffn1_w_up": nrm(ks[3], (D_MODEL, D_FF), D_MODEL),
        "ffn1_w_down": nrm(ks[4], (D_FF, D_MODEL), D_FF),
        "mix_norm": gain(ks[5], D_MODEL),
        "w_in": nrm(ks[6], (D_MODEL, D_IN_PROJ), D_MODEL),
        "conv_w": nrm(ks[7], (CONV_WIDTH, 3 * D_DN), CONV_WIDTH),
        "a_log": jnp.log(jax.random.uniform(ks[8], (N_DN_HEADS,), f32, 1.0, 16.0)),
        "dt_bias": dt + jnp.log(-jnp.expm1(-dt)),
        "dn_norm": gain(ks[10], DN_HEAD_DIM),
        "w_out": nrm(ks[11], (D_MIX, D_MODEL), D_MIX),
        "ffn2_norm": gain(ks[12], D_MODEL),
        "ffn2_w_gate": nrm(ks[13], (D_MODEL, D_FF), D_MODEL),
        "ffn2_w_up": nrm(ks[14], (D_MODEL, D_FF), D_MODEL),
        "ffn2_w_down": nrm(ks[15], (D_FF, D_MODEL), D_FF),
        "final_norm": gain(ks[16], D_MODEL),
    }


def _fwd_reference(x, ffn1_norm, ffn1_w_gate, ffn1_w_up, ffn1_w_down, mix_norm, w_in, conv_w,
              a_log, dt_bias, dn_norm, w_out, ffn2_norm, ffn2_w_gate, ffn2_w_up, ffn2_w_down,
              final_norm):
    h = x
    for _ in range(DEPTH):
        h = h + 0.5 * _swiglu(_rmsnorm(h, ffn1_norm), ffn1_w_gate, ffn1_w_up, ffn1_w_down)
        h = h + _hybrid_mixer(_rmsnorm(h, mix_norm), w_in, conv_w, a_log, dt_bias, dn_norm, w_out)
        h = h + 0.5 * _swiglu(_rmsnorm(h, ffn2_norm), ffn2_w_gate, ffn2_w_up, ffn2_w_down)
    return _rmsnorm(h, final_norm)


import jax as _jax
import jax.numpy as _jnp

TWIN_FORMAT = 'train_step'
FWD_PARAMS = ['x', 'ffn1_norm', 'ffn1_w_gate', 'ffn1_w_up', 'ffn1_w_down', 'mix_norm', 'w_in', 'conv_w', 'a_log', 'dt_bias', 'dn_norm', 'w_out', 'ffn2_norm', 'ffn2_w_gate', 'ffn2_w_up', 'ffn2_w_down', 'final_norm']
TWIN_WEIGHTS = ['ffn1_norm', 'ffn1_w_gate', 'ffn1_w_up', 'ffn1_w_down', 'mix_norm', 'w_in', 'conv_w', 'a_log', 'dt_bias', 'dn_norm', 'w_out', 'ffn2_norm', 'ffn2_w_gate', 'ffn2_w_up', 'ffn2_w_down', 'final_norm']
TWIN_DIFF_INPUT = 'x'
TWIN_INPUTS = ['x', 'ffn1_norm', 'ffn1_w_gate', 'ffn1_w_up', 'ffn1_w_down', 'mix_norm', 'w_in', 'conv_w', 'a_log', 'dt_bias', 'dn_norm', 'w_out', 'ffn2_norm', 'ffn2_w_gate', 'ffn2_w_up', 'ffn2_w_down', 'final_norm', 'loss_target', 'm_ffn1_norm', 'm_ffn1_w_gate', 'm_ffn1_w_up', 'm_ffn1_w_down', 'm_mix_norm', 'm_w_in', 'm_conv_w', 'm_a_log', 'm_dt_bias', 'm_dn_norm', 'm_w_out', 'm_ffn2_norm', 'm_ffn2_w_gate', 'm_ffn2_w_up', 'm_ffn2_w_down', 'm_final_norm', 'v_ffn1_norm', 'v_ffn1_w_gate', 'v_ffn1_w_up', 'v_ffn1_w_down', 'v_mix_norm', 'v_w_in', 'v_conv_w', 'v_a_log', 'v_dt_bias', 'v_dn_norm', 'v_w_out', 'v_ffn2_norm', 'v_ffn2_w_gate', 'v_ffn2_w_up', 'v_ffn2_w_down', 'v_final_norm']
TWIN_OUTPUTS = ['loss', 'grad_x', 'grad_ffn1_norm', 'grad_ffn1_w_gate', 'grad_ffn1_w_up', 'grad_ffn1_w_down', 'grad_mix_norm', 'grad_w_in', 'grad_conv_w', 'grad_a_log', 'grad_dt_bias', 'grad_dn_norm', 'grad_w_out', 'grad_ffn2_norm', 'grad_ffn2_w_gate', 'grad_ffn2_w_up', 'grad_ffn2_w_down', 'grad_final_norm', 'delta_ffn1_norm', 'delta_ffn1_w_gate', 'delta_ffn1_w_up', 'delta_ffn1_w_down', 'delta_mix_norm', 'delta_w_in', 'delta_conv_w', 'delta_a_log', 'delta_dt_bias', 'delta_dn_norm', 'delta_w_out', 'delta_ffn2_norm', 'delta_ffn2_w_gate', 'delta_ffn2_w_up', 'delta_ffn2_w_down', 'delta_final_norm', 'new_m_ffn1_norm', 'new_m_ffn1_w_gate', 'new_m_ffn1_w_up', 'new_m_ffn1_w_down', 'new_m_mix_norm', 'new_m_w_in', 'new_m_conv_w', 'new_m_a_log', 'new_m_dt_bias', 'new_m_dn_norm', 'new_m_w_out', 'new_m_ffn2_norm', 'new_m_ffn2_w_gate', 'new_m_ffn2_w_up', 'new_m_ffn2_w_down', 'new_m_final_norm', 'new_v_ffn1_norm', 'new_v_ffn1_w_gate', 'new_v_ffn1_w_up', 'new_v_ffn1_w_down', 'new_v_mix_norm', 'new_v_w_in', 'new_v_conv_w', 'new_v_a_log', 'new_v_dt_bias', 'new_v_dn_norm', 'new_v_w_out', 'new_v_ffn2_norm', 'new_v_ffn2_w_gate', 'new_v_ffn2_w_up', 'new_v_ffn2_w_down', 'new_v_final_norm']
TWIN_LEAF_KINDS = {'loss': 'loss', 'grad_x': 'grad_x', 'grad_ffn1_norm': 'grad_w', 'grad_ffn1_w_gate': 'grad_w', 'grad_ffn1_w_up': 'grad_w', 'grad_ffn1_w_down': 'grad_w', 'grad_mix_norm': 'grad_w', 'grad_w_in': 'grad_w', 'grad_conv_w': 'grad_w', 'grad_a_log': 'grad_w', 'grad_dt_bias': 'grad_w', 'grad_dn_norm': 'grad_w', 'grad_w_out': 'grad_w', 'grad_ffn2_norm': 'grad_w', 'grad_ffn2_w_gate': 'grad_w', 'grad_ffn2_w_up': 'grad_w', 'grad_ffn2_w_down': 'grad_w', 'grad_final_norm': 'grad_w', 'delta_ffn1_norm': 'delta_w', 'delta_ffn1_w_gate': 'delta_w', 'delta_ffn1_w_up': 'delta_w', 'delta_ffn1_w_down': 'delta_w', 'delta_mix_norm': 'delta_w', 'delta_w_in': 'delta_w', 'delta_conv_w': 'delta_w', 'delta_a_log': 'delta_w', 'delta_dt_bias': 'delta_w', 'delta_dn_norm': 'delta_w', 'delta_w_out': 'delta_w', 'delta_ffn2_norm': 'delta_w', 'delta_ffn2_w_gate': 'delta_w', 'delta_ffn2_w_up': 'delta_w', 'delta_ffn2_w_down': 'delta_w', 'delta_final_norm': 'delta_w', 'new_m_ffn1_norm': 'new_m', 'new_m_ffn1_w_gate': 'new_m', 'new_m_ffn1_w_up': 'new_m', 'new_m_ffn1_w_down': 'new_m', 'new_m_mix_norm': 'new_m', 'new_m_w_in': 'new_m', 'new_m_conv_w': 'new_m', 'new_m_a_log': 'new_m', 'new_m_dt_bias': 'new_m', 'new_m_dn_norm': 'new_m', 'new_m_w_out': 'new_m', 'new_m_ffn2_norm': 'new_m', 'new_m_ffn2_w_gate': 'new_m', 'new_m_ffn2_w_up': 'new_m', 'new_m_ffn2_w_down': 'new_m', 'new_m_final_norm': 'new_m', 'new_v_ffn1_norm': 'new_v', 'new_v_ffn1_w_gate': 'new_v', 'new_v_ffn1_w_up': 'new_v', 'new_v_ffn1_w_down': 'new_v', 'new_v_mix_norm': 'new_v', 'new_v_w_in': 'new_v', 'new_v_conv_w': 'new_v', 'new_v_a_log': 'new_v', 'new_v_dt_bias': 'new_v', 'new_v_dn_norm': 'new_v', 'new_v_w_out': 'new_v', 'new_v_ffn2_norm': 'new_v', 'new_v_ffn2_w_gate': 'new_v', 'new_v_ffn2_w_up': 'new_v', 'new_v_ffn2_w_down': 'new_v', 'new_v_final_norm': 'new_v'}


def _forward(args):
    return _fwd_reference(*[args[k] for k in FWD_PARAMS])


def _output_shape():
    def fwd():
        inp = _fwd_setup_inputs(0)
        return _fwd_reference(*[inp[k] for k in FWD_PARAMS])
    out = _jax.eval_shape(fwd)
    return out.shape, out.dtype

N_MICROBATCH = 1
ADAM_LR = 0.001
ADAM_B1 = 0.9
ADAM_B2 = 0.999
ADAM_EPS = 1e-08
ADAM_WD = 0.01
ADAM_STEP = 10
PER_EXAMPLE_BATCH_AXIS = {'x': 0, 'loss_target': 0}
SHARED_INPUTS = []
_WEIGHT_DTYPES = {'ffn1_norm': _jnp.float32, 'ffn1_w_gate': _jnp.float32, 'ffn1_w_up': _jnp.float32, 'ffn1_w_down': _jnp.float32, 'mix_norm': _jnp.float32, 'w_in': _jnp.float32, 'conv_w': _jnp.float32, 'a_log': _jnp.float32, 'dt_bias': _jnp.float32, 'dn_norm': _jnp.float32, 'w_out': _jnp.float32, 'ffn2_norm': _jnp.float32, 'ffn2_w_gate': _jnp.float32, 'ffn2_w_up': _jnp.float32, 'ffn2_w_down': _jnp.float32, 'final_norm': _jnp.float32}
MOMENT_SCALE = {'ffn1_norm': 2.150510e-02, 'ffn1_w_gate': 9.384473e-03, 'ffn1_w_up': 9.084924e-03, 'ffn1_w_down': 1.490441e-02, 'mix_norm': 3.056253e-02, 'w_in': 1.605822e-02, 'conv_w': 1.898194e-02, 'a_log': 9.909432e-02, 'dt_bias': 9.283886e-02, 'dn_norm': 1.022899e-01, 'w_out': 1.798179e-02, 'ffn2_norm': 1.636196e-02, 'ffn2_w_gate': 7.232778e-03, 'ffn2_w_up': 7.018780e-03, 'ffn2_w_down': 1.151010e-02, 'final_norm': 8.006102e+00}


def _to_microbatches(a, axis):
    t = _jnp.moveaxis(a, axis, 0)
    t = t.reshape((N_MICROBATCH, t.shape[0] // N_MICROBATCH) + t.shape[1:])
    return _jnp.moveaxis(t, 1, axis + 1)


def setup_inputs(seed: int = 0) -> dict:
    inp = _fwd_setup_inputs(seed)
    key = _jax.random.fold_in(_jax.random.key(seed), 7919)
    shape, _ = _output_shape()
    out = dict(inp)
    out["loss_target"] = _jax.random.normal(_jax.random.fold_in(key, 0), shape, _jnp.float32)
    for i, name in enumerate(TWIN_WEIGHTS):
        w = inp[name].astype(_jnp.float32)
        if MOMENT_SCALE is None:
            s = _jnp.sqrt(_jnp.mean(_jnp.square(w)) + 1e-30)
        else:
            s = MOMENT_SCALE[name]
        km, kv = _jax.random.split(_jax.random.fold_in(key, i + 1))
        out[name] = w
        out["m_" + name] = s * _jax.random.normal(km, w.shape, _jnp.float32)
        out["v_" + name] = (s * s) * _jax.random.uniform(kv, w.shape, _jnp.float32, 0.5, 1.5)
    if N_MICROBATCH > 1:
        for name, axis in PER_EXAMPLE_BATCH_AXIS.items():
            out[name] = _to_microbatches(out[name], axis)
    return {'x': out['x'], 'ffn1_norm': out['ffn1_norm'], 'ffn1_w_gate': out['ffn1_w_gate'], 'ffn1_w_up': out['ffn1_w_up'], 'ffn1_w_down': out['ffn1_w_down'], 'mix_norm': out['mix_norm'], 'w_in': out['w_in'], 'conv_w': out['conv_w'], 'a_log': out['a_log'], 'dt_bias': out['dt_bias'], 'dn_norm': out['dn_norm'], 'w_out': out['w_out'], 'ffn2_norm': out['ffn2_norm'], 'ffn2_w_gate': out['ffn2_w_gate'], 'ffn2_w_up': out['ffn2_w_up'], 'ffn2_w_down': out['ffn2_w_down'], 'final_norm': out['final_norm'], 'loss_target': out['loss_target'], 'm_ffn1_norm': out['m_ffn1_norm'], 'm_ffn1_w_gate': out['m_ffn1_w_gate'], 'm_ffn1_w_up': out['m_ffn1_w_up'], 'm_ffn1_w_down': out['m_ffn1_w_down'], 'm_mix_norm': out['m_mix_norm'], 'm_w_in': out['m_w_in'], 'm_conv_w': out['m_conv_w'], 'm_a_log': out['m_a_log'], 'm_dt_bias': out['m_dt_bias'], 'm_dn_norm': out['m_dn_norm'], 'm_w_out': out['m_w_out'], 'm_ffn2_norm': out['m_ffn2_norm'], 'm_ffn2_w_gate': out['m_ffn2_w_gate'], 'm_ffn2_w_up': out['m_ffn2_w_up'], 'm_ffn2_w_down': out['m_ffn2_w_down'], 'm_final_norm': out['m_final_norm'], 'v_ffn1_norm': out['v_ffn1_norm'], 'v_ffn1_w_gate': out['v_ffn1_w_gate'], 'v_ffn1_w_up': out['v_ffn1_w_up'], 'v_ffn1_w_down': out['v_ffn1_w_down'], 'v_mix_norm': out['v_mix_norm'], 'v_w_in': out['v_w_in'], 'v_conv_w': out['v_conv_w'], 'v_a_log': out['v_a_log'], 'v_dt_bias': out['v_dt_bias'], 'v_dn_norm': out['v_dn_norm'], 'v_w_out': out['v_w_out'], 'v_ffn2_norm': out['v_ffn2_norm'], 'v_ffn2_w_gate': out['v_ffn2_w_gate'], 'v_ffn2_w_up': out['v_ffn2_w_up'], 'v_ffn2_w_down': out['v_ffn2_w_down'], 'v_final_norm': out['v_final_norm']}


def _loss(weights, diff, rest, loss_target):
    with _jax.named_scope("forward"):
        args = {**rest, TWIN_DIFF_INPUT: diff, **{k: w.astype(_WEIGHT_DTYPES[k]) for k, w in weights.items()}}
        y = _forward(args)
    with _jax.named_scope("loss_head"):
        err = _jnp.square(y.astype(_jnp.float32) - loss_target)
        return 0.5 * _jnp.sum(_jnp.mean(err, axis=-1)) if err.ndim else 0.5 * err


def _adamw(w, g, m, v):
    m = ADAM_B1 * m + (1.0 - ADAM_B1) * g
    v = ADAM_B2 * v + (1.0 - ADAM_B2) * _jnp.square(g)
    m_hat = m / (1.0 - ADAM_B1 ** ADAM_STEP)
    v_hat = v / (1.0 - ADAM_B2 ** ADAM_STEP)
    delta = -ADAM_LR * (m_hat / (_jnp.sqrt(v_hat) + ADAM_EPS) + ADAM_WD * w)
    return delta, m, v


def reference(x, ffn1_norm, ffn1_w_gate, ffn1_w_up, ffn1_w_down, mix_norm, w_in, conv_w, a_log, dt_bias, dn_norm, w_out, ffn2_norm, ffn2_w_gate, ffn2_w_up, ffn2_w_down, final_norm, loss_target, m_ffn1_norm, m_ffn1_w_gate, m_ffn1_w_up, m_ffn1_w_down, m_mix_norm, m_w_in, m_conv_w, m_a_log, m_dt_bias, m_dn_norm, m_w_out, m_ffn2_norm, m_ffn2_w_gate, m_ffn2_w_up, m_ffn2_w_down, m_final_norm, v_ffn1_norm, v_ffn1_w_gate, v_ffn1_w_up, v_ffn1_w_down, v_mix_norm, v_w_in, v_conv_w, v_a_log, v_dt_bias, v_dn_norm, v_w_out, v_ffn2_norm, v_ffn2_w_gate, v_ffn2_w_up, v_ffn2_w_down, v_final_norm):
    given = dict(x=x, ffn1_norm=ffn1_norm, ffn1_w_gate=ffn1_w_gate, ffn1_w_up=ffn1_w_up, ffn1_w_down=ffn1_w_down, mix_norm=mix_norm, w_in=w_in, conv_w=conv_w, a_log=a_log, dt_bias=dt_bias, dn_norm=dn_norm, w_out=w_out, ffn2_norm=ffn2_norm, ffn2_w_gate=ffn2_w_gate, ffn2_w_up=ffn2_w_up, ffn2_w_down=ffn2_w_down, final_norm=final_norm, loss_target=loss_target, m_ffn1_norm=m_ffn1_norm, m_ffn1_w_gate=m_ffn1_w_gate, m_ffn1_w_up=m_ffn1_w_up, m_ffn1_w_down=m_ffn1_w_down, m_mix_norm=m_mix_norm, m_w_in=m_w_in, m_conv_w=m_conv_w, m_a_log=m_a_log, m_dt_bias=m_dt_bias, m_dn_norm=m_dn_norm, m_w_out=m_w_out, m_ffn2_norm=m_ffn2_norm, m_ffn2_w_gate=m_ffn2_w_gate, m_ffn2_w_up=m_ffn2_w_up, m_ffn2_w_down=m_ffn2_w_down, m_final_norm=m_final_norm, v_ffn1_norm=v_ffn1_norm, v_ffn1_w_gate=v_ffn1_w_gate, v_ffn1_w_up=v_ffn1_w_up, v_ffn1_w_down=v_ffn1_w_down, v_mix_norm=v_mix_norm, v_w_in=v_w_in, v_conv_w=v_conv_w, v_a_log=v_a_log, v_dt_bias=v_dt_bias, v_dn_norm=v_dn_norm, v_w_out=v_w_out, v_ffn2_norm=v_ffn2_norm, v_ffn2_w_gate=v_ffn2_w_gate, v_ffn2_w_up=v_ffn2_w_up, v_ffn2_w_down=v_ffn2_w_down, v_final_norm=v_final_norm)
    weights = {n: given[n] for n in TWIN_WEIGHTS}
    shared = {n: given[n] for n in SHARED_INPUTS}
    per_example = {n: given[n] for n in ['x']}
    grad_fn = _jax.value_and_grad(_loss, argnums=(0, 1))

    def one_microbatch(ex, loss_target):
        ex = dict(ex)
        diff = ex.pop(TWIN_DIFF_INPUT)
        return grad_fn(weights, diff, {**shared, **ex}, loss_target)

    if N_MICROBATCH == 1:
        loss, (grad_w, grad_x) = one_microbatch(per_example, given["loss_target"])
    else:
        def body(carry, xs):
            loss_sum, grad_sum = carry
            l_k, (gw_k, gx_k) = one_microbatch(xs[0], xs[1])
            with _jax.named_scope("update"):
                return (loss_sum + l_k, _jax.tree.map(_jnp.add, grad_sum, gw_k)), gx_k

        init = (_jnp.zeros((), _jnp.float32), _jax.tree.map(_jnp.zeros_like, weights))
        (loss, grad_w), grad_x = _jax.lax.scan(body, init, (per_example, given["loss_target"]))
    with _jax.named_scope("update"):
        delta_w, new_m, new_v = {}, {}, {}
        for n in TWIN_WEIGHTS:
            delta_w[n], new_m[n], new_v[n] = _adamw(weights[n], grad_w[n], given["m_" + n], given["v_" + n])
    return (loss, grad_x, *[grad_w[n] for n in TWIN_WEIGHTS], *[delta_w[n] for n in TWIN_WEIGHTS],
            *[new_m[n] for n in TWIN_WEIGHTS], *[new_v[n] for n in TWIN_WEIGHTS])
```

```python
import functools
import math

import jax
import jax.numpy as jnp
import numpy as np
from jax import lax
from jax.experimental import pallas as pl
from jax.experimental.pallas import tpu as pltpu

F32 = jnp.float32
BF16 = jnp.bfloat16
MESH = pl.DeviceIdType.MESH
ANY = pl.BlockSpec(memory_space=pl.ANY)

LANE = 128
N_CHIPS = 4
N_DEV = 8
EPS = 1e-6
HEAD_DIM = 128
CONV_WIDTH = 4
CHUNK = 64
ATTN_BLOCK = 128
DILATED_CONFIGS = ((128, 1), (512, 4), (2048, 16))
VMEM_LIMIT = 52 * 1024 * 1024

ADAM_LR = 0.001
ADAM_B1 = 0.9
ADAM_B2 = 0.999
ADAM_EPS = 1e-08
ADAM_WD = 0.01
ADAM_STEP = 10

NN = (((1,), (0,)), ((), ()))
NT = (((1,), (1,)), ((), ()))
TN = (((0,), (0,)), ((), ()))


def _ceil_to(v, m):
    return -(-v // m) * m


def _params(vmem=VMEM_LIMIT):
    return pltpu.CompilerParams(vmem_limit_bytes=vmem)


def _gemm(name, grid, pairs, dn, acc_shape, n_acc, extras, outs, epilogue, n_prefetch=0, prefetch=()):
    n_pairs, n_ex, n_out = len(pairs), len(extras), len(outs)
    kax = len(grid) - 1
    nk = grid[kax]

    def body(*refs):
        refs = refs[n_prefetch:]
        ins = refs[: 2 * n_pairs]
        ex = refs[2 * n_pairs: 2 * n_pairs + n_ex]
        out_refs = refs[2 * n_pairs + n_ex: 2 * n_pairs + n_ex + n_out]
        accs = refs[2 * n_pairs + n_ex + n_out:]
        k = pl.program_id(kax)

        @pl.when(k == 0)
        def _():
            for acc in accs:
                acc[...] = jnp.zeros(acc.shape, F32)

        for q in range(n_pairs):
            a = ins[2 * q][...]
            b = ins[2 * q + 1][...]
            if a.dtype != BF16:
                a = a.astype(BF16)
            if b.dtype != BF16:
                b = b.astype(BF16)
            accs[pairs[q][4]][...] += lax.dot_general(a, b, dn, preferred_element_type=F32)

        @pl.when(k == nk - 1)
        def _():
            res = epilogue([acc[...] for acc in accs], [e[...] for e in ex])
            for o, r in zip(out_refs, res):
                o[...] = r.astype(o.dtype)

    in_specs = []
    args = []
    for a, a_spec, b, b_spec, _ in pairs:
        in_specs += [a_spec, b_spec]
        args += [a, b]
    for e, e_spec in extras:
        in_specs.append(e_spec)
        args.append(e)
    out_shape = [o for o, _ in outs]
    out_specs = [s for _, s in outs]
    scratch = [pltpu.VMEM(acc_shape, F32) for _ in range(n_acc)]
    if n_prefetch:
        gs = pltpu.PrefetchScalarGridSpec(num_scalar_prefetch=n_prefetch, grid=grid, in_specs=in_specs,
                                          out_specs=out_specs, scratch_shapes=scratch)
        return pl.pallas_call(body, name=name, grid_spec=gs, out_shape=out_shape,
                              compiler_params=_params())(*prefetch, *args)
    return pl.pallas_call(body, name=name, grid=grid, in_specs=in_specs, out_specs=out_specs,
                          out_shape=out_shape, scratch_shapes=scratch, compiler_params=_params())(*args)


def _pick(n, prefs):
    for p in prefs:
        if n % p == 0:
            return p
    return n


def _sigmoid(v):
    return 1.0 / (1.0 + jnp.exp(-v))


def _ffn_up(n, wg, wu, name):
    t, d = n.shape
    npieces, _, hp = wg.shape
    tm = _pick(t, (512, 256, 128, 64, 32, 16))
    tk = _pick(d, (1024, 512, 256, 128))
    grid = (t // tm, npieces, d // tk)
    a_spec = pl.BlockSpec((tm, tk), lambda i, p, k: (i, k))
    w_spec = pl.BlockSpec((None, tk, hp), lambda i, p, k: (p, k, 0))
    o_spec = pl.BlockSpec((tm, hp), lambda i, p, k: (i, p))
    osd = jax.ShapeDtypeStruct((t, npieces * hp), BF16)

    def epi(accs, ex):
        a, b = accs
        return a, b, a * _sigmoid(a) * b

    return _gemm(name, grid, [(n, a_spec, wg, w_spec, 0), (n, a_spec, wu, w_spec, 1)], NN, (tm, hp), 2, [],
                 [(osd, o_spec)] * 3, epi)


def _mm_pieces_resid(a, w, resid, scale, amap, name):
    t = a.shape[0]
    npieces, kp, n = w.shape
    tm = _pick(t, (512, 256, 128, 64, 32, 16))
    tn = _pick(n, (1024, 512, 256, 128))
    grid = (t // tm, n // tn, npieces)
    a_spec = pl.BlockSpec((tm, kp), lambda i, j, p: (i, amap(p)))
    w_spec = pl.BlockSpec((None, kp, tn), lambda i, j, p: (p, 0, j))
    r_spec = pl.BlockSpec((tm, tn), lambda i, j, p: (i, j))

    def epi(accs, ex):
        return (ex[0] + scale * accs[0],)

    return _gemm(name, grid, [(a, a_spec, w, w_spec, 0)], NN, (tm, tn), 1, [(resid, r_spec)],
                 [(jax.ShapeDtypeStruct((t, n), F32), r_spec)], epi)[0]


def _ffn_bwd_hidden(dh, wd, a, b, name):
    t, d = dh.shape
    npieces, hp, _ = wd.shape
    tm = _pick(t, (512, 256, 128, 64, 32, 16))
    tk = _pick(d, (1024, 512, 256, 128))
    grid = (t // tm, npieces, d // tk)
    a_spec = pl.BlockSpec((tm, tk), lambda i, p, k: (i, k))
    w_spec = pl.BlockSpec((None, hp, tk), lambda i, p, k: (p, 0, k))
    o_spec = pl.BlockSpec((tm, hp), lambda i, p, k: (i, p))
    osd = jax.ShapeDtypeStruct((t, npieces * hp), BF16)

    def epi(accs, ex):
        ds = 0.5 * accs[0]
        av = ex[0].astype(F32)
        bv = ex[1].astype(F32)
        sg = _sigmoid(av)
        da = ds * bv * (sg * (1.0 + av * (1.0 - sg)))
        db = ds * (av * sg)
        return da, db

    return _gemm(name, grid, [(dh, a_spec, wd, w_spec, 0)], NT, (tm, hp), 1, [(a, o_spec), (b, o_spec)],
                 [(osd, o_spec)] * 2, epi)


def _mm_nt_pieces_out(dh, w, omap, name):
    t, d = dh.shape
    npieces, npp, _ = w.shape
    tm = _pick(t, (512, 256, 128, 64, 32, 16))
    tk = _pick(d, (1024, 512, 256, 128))
    grid = (t // tm, npieces, d // tk)
    a_spec = pl.BlockSpec((tm, tk), lambda i, p, k: (i, k))
    w_spec = pl.BlockSpec((None, npp, tk), lambda i, p, k: (p, 0, k))
    o_spec = pl.BlockSpec((tm, npp), lambda i, p, k: (i, omap(p)))
    return _gemm(name, grid, [(dh, a_spec, w, w_spec, 0)], NT, (tm, npp), 1, [],
                 [(jax.ShapeDtypeStruct((t, npieces * npp), BF16), o_spec)], lambda accs, ex: (accs[0],))[0]


def _grad_rows_pieces(x, dy, scale, amap, npieces, name):
    t, n = dy.shape
    mp = x.shape[1] // npieces
    tn = _pick(n, (1024, 512, 256, 128))
    tk = _pick(t, (512, 256, 128, 64, 32, 16))
    grid = (npieces, n // tn, t // tk)
    x_spec = pl.BlockSpec((tk, mp), lambda p, j, k: (k, amap(p)))
    y_spec = pl.BlockSpec((tk, tn), lambda p, j, k: (k, j))
    o_spec = pl.BlockSpec((None, mp, tn), lambda p, j, k: (p, 0, j))
    return _gemm(name, grid, [(x, x_spec, dy, y_spec, 0)], TN, (mp, tn), 1, [],
                 [(jax.ShapeDtypeStruct((npieces, mp, n), BF16), o_spec)], lambda accs, ex: (scale * accs[0],))[0]


def _grad_cols_pieces(n, da, db, npieces, name):
    t, d = n.shape
    hp = da.shape[1] // npieces
    tm = _pick(d, (512, 256, 128))
    tk = _pick(t, (512, 256, 128, 64, 32, 16))
    grid = (npieces, d // tm, t // tk)
    n_spec = pl.BlockSpec((tk, tm), lambda p, i, k: (k, i))
    g_spec = pl.BlockSpec((tk, hp), lambda p, i, k: (k, p))
    o_spec = pl.BlockSpec((None, tm, hp), lambda p, i, k: (p, i, 0))
    osd = jax.ShapeDtypeStruct((npieces, d, hp), BF16)
    return _gemm(name, grid, [(n, n_spec, da, g_spec, 0), (n, n_spec, db, g_spec, 1)], TN, (tm, hp), 2, [],
                 [(osd, o_spec)] * 2, lambda accs, ex: (accs[0], accs[1]))


def _ffn_bwd_input(da, db, wg, wu, name):
    t = da.shape[0]
    npieces, d, hp = wg.shape
    tm = _pick(t, (512, 256, 128, 64, 32, 16))
    tn = _pick(d, (1024, 512, 256, 128))
    grid = (t // tm, d // tn, npieces)
    g_spec = pl.BlockSpec((tm, hp), lambda i, j, p: (i, p))
    w_spec = pl.BlockSpec((None, tn, hp), lambda i, j, p: (p, j, 0))
    o_spec = pl.BlockSpec((tm, tn), lambda i, j, p: (i, j))
    return _gemm(name, grid, [(da, g_spec, wg, w_spec, 0), (db, g_spec, wu, w_spec, 0)], NT, (tm, tn), 1, [],
                 [(jax.ShapeDtypeStruct((t, d), F32), o_spec)], lambda accs, ex: (accs[0],))[0]


def _mm2d(a, b, dn, out_dtype, name):
    if dn == NN:
        m, kk = a.shape
        n = b.shape[1]
    elif dn == NT:
        m, kk = a.shape
        n = b.shape[0]
    else:
        kk, m = a.shape
        n = b.shape[1]
    tm = _pick(m, (512, 256, 128, 64, 32, 16))
    tn = _pick(n, (768, 1024, 512, 256, 128))
    tk = _pick(kk, (768, 1024, 512, 256, 128, 64, 32, 16))
    grid = (m // tm, n // tn, kk // tk)
    if dn == TN:
        a_spec = pl.BlockSpec((tk, tm), lambda i, j, k: (k, i))
    else:
        a_spec = pl.BlockSpec((tm, tk), lambda i, j, k: (i, k))
    if dn == NT:
        b_spec = pl.BlockSpec((tn, tk), lambda i, j, k: (j, k))
    else:
        b_spec = pl.BlockSpec((tk, tn), lambda i, j, k: (k, j))
    o_spec = pl.BlockSpec((tm, tn), lambda i, j, k: (i, j))
    return _gemm(name, grid, [(a, a_spec, b, b_spec, 0)], dn, (tm, tn), 1, [],
                 [(jax.ShapeDtypeStruct((m, n), out_dtype), o_spec)], lambda accs, ex: (accs[0],))[0]


def _row_tile(t):
    return _pick(t, (256, 128, 64, 32, 16, 8))


def _rms_fwd(x, w, name):
    t, d = x.shape
    tm = _row_tile(t)

    def body(x_ref, w_ref, o_ref):
        xv = x_ref[...]
        r = lax.rsqrt(jnp.mean(xv * xv, axis=-1, keepdims=True) + EPS)
        o_ref[...] = (xv * r * w_ref[...]).astype(BF16)

    return pl.pallas_call(
        body, name=name, grid=(t // tm,),
        in_specs=[pl.BlockSpec((tm, d), lambda i: (i, 0)), pl.BlockSpec((1, d), lambda i: (0, 0))],
        out_specs=pl.BlockSpec((tm, d), lambda i: (i, 0)),
        out_shape=jax.ShapeDtypeStruct((t, d), BF16), compiler_params=_params())(x, w.reshape(1, d))


def _rms_bwd(dn, x, w, dres, name):
    t, d = x.shape
    tm = _row_tile(t)

    def body(dn_ref, x_ref, w_ref, r_ref, o_ref, ob_ref, dw_ref):
        i = pl.program_id(0)
        xv = x_ref[...]
        r = lax.rsqrt(jnp.mean(xv * xv, axis=-1, keepdims=True) + EPS)
        xh = xv * r
        dy = dn_ref[...].astype(F32)
        g = dy * w_ref[...]
        dx = r * (g - xh * jnp.mean(g * xh, axis=-1, keepdims=True))
        tot = r_ref[...] + dx
        o_ref[...] = tot
        ob_ref[...] = tot.astype(BF16)
        part = (dy * xh).reshape(tm // 8, 8, d).sum(axis=0)

        @pl.when(i == 0)
        def _():
            dw_ref[...] = part

        @pl.when(i > 0)
        def _():
            dw_ref[...] += part

    row = pl.BlockSpec((tm, d), lambda i: (i, 0))
    return pl.pallas_call(
        body, name=name, grid=(t // tm,),
        in_specs=[row, row, pl.BlockSpec((1, d), lambda i: (0, 0)), row],
        out_specs=[row, row, pl.BlockSpec((8, d), lambda i: (0, 0))],
        out_shape=[jax.ShapeDtypeStruct((t, d), F32), jax.ShapeDtypeStruct((t, d), BF16),
                   jax.ShapeDtypeStruct((8, d), F32)],
        compiler_params=_params())(dn, x, w.reshape(1, d), dres)


def _final_loss(h, w, target, name):
    t, d = h.shape
    tm = _row_tile(t)

    def body(h_ref, w_ref, t_ref, o_ref, ob_ref, dw_ref, ls_ref):
        i = pl.program_id(0)
        xv = h_ref[...]
        r = lax.rsqrt(jnp.mean(xv * xv, axis=-1, keepdims=True) + EPS)
        xh = xv * r
        err = xh * w_ref[...] - t_ref[...]
        dy = err * (1.0 / d)
        g = dy * w_ref[...]
        dx = r * (g - xh * jnp.mean(g * xh, axis=-1, keepdims=True))
        o_ref[...] = dx
        ob_ref[...] = dx.astype(BF16)
        part = (dy * xh).reshape(tm // 8, 8, d).sum(axis=0)
        lpart = (err * err).reshape(tm // 8, 8, d).sum(axis=0)

        @pl.when(i == 0)
        def _():
            dw_ref[...] = part
            ls_ref[...] = lpart

        @pl.when(i > 0)
        def _():
            dw_ref[...] += part
            ls_ref[...] += lpart

    row = pl.BlockSpec((tm, d), lambda i: (i, 0))
    acc = pl.BlockSpec((8, d), lambda i: (0, 0))
    return pl.pallas_call(
        body, name=name, grid=(t // tm,),
        in_specs=[row, pl.BlockSpec((1, d), lambda i: (0, 0)), row],
        out_specs=[row, row, acc, acc],
        out_shape=[jax.ShapeDtypeStruct((t, d), F32), jax.ShapeDtypeStruct((t, d), BF16),
                   jax.ShapeDtypeStruct((8, d), F32), jax.ShapeDtypeStruct((8, d), F32)],
        compiler_params=_params())(h, w.reshape(1, d), target)


def _cast_split_cols(w, hp, name):
    r, fs = w.shape
    v1 = fs - hp
    tm = _pick(r, (256, 128, 64, 32, 16))

    def body(w_ref, o_ref):
        o_ref[0] = w_ref[:, :hp].astype(BF16)
        if v1 < hp:
            o_ref[1] = jnp.zeros((tm, hp), BF16)
        o_ref[1, :, :v1] = w_ref[:, hp:].astype(BF16)

    return pl.pallas_call(
        body, name=name, grid=(r // tm,),
        in_specs=[pl.BlockSpec((tm, fs), lambda i: (i, 0))],
        out_specs=pl.BlockSpec((2, tm, hp), lambda i: (0, i, 0)),
        out_shape=jax.ShapeDtypeStruct((2, r, hp), BF16), compiler_params=_params())(w)


def _cast_split_rows(w, hp, tr, name):
    fs, c = w.shape
    nvalid = fs // tr
    nblk = 2 * hp // tr

    def body(w_ref, o_ref):
        i = pl.program_id(0)

        @pl.when(i < nvalid)
        def _():
            o_ref[...] = w_ref[...].astype(BF16)

        @pl.when(i >= nvalid)
        def _():
            o_ref[...] = jnp.zeros(o_ref.shape, BF16)

    return pl.pallas_call(
        body, name=name, grid=(nblk,),
        in_specs=[pl.BlockSpec((tr, c), lambda i: (jnp.minimum(i, nvalid - 1), 0))],
        out_specs=pl.BlockSpec((tr, c), lambda i: (i, 0)),
        out_shape=jax.ShapeDtypeStruct((2 * hp, c), BF16), compiler_params=_params())(w)


def _combine_windows(wall, tables, n_tiles, name):
    _, _, d, wh = wall.shape
    tpw = wh // LANE

    def body(tab_ref, a_ref, b_ref, o_ref):
        t = pl.program_id(0)
        both = tab_ref[6, t] == 1
        av = a_ref[...]
        bv = b_ref[...]
        o_ref[...] = jnp.where(both, av + bv, av)

    def amap(t, tab):
        return (tab[0, t], tab[1, t], 0, tab[2, t])

    def bmap(t, tab):
        return (tab[3, t], tab[4, t], 0, tab[5, t])

    gs = pltpu.PrefetchScalarGridSpec(
        num_scalar_prefetch=1, grid=(n_tiles,),
        in_specs=[pl.BlockSpec((None, None, d, LANE), amap), pl.BlockSpec((None, None, d, LANE), bmap)],
        out_specs=pl.BlockSpec((d, LANE), lambda t, tab: (0, t)))
    del tpw
    return pl.pallas_call(body, name=name, grid_spec=gs, out_shape=jax.ShapeDtypeStruct((d, n_tiles * LANE), BF16),
                          compiler_params=_params())(tables, wall, wall)


def _coords():
    return lax.axis_index("x"), lax.axis_index("y"), lax.axis_index("c")


def _remote(src, dst, ssem, rsem, dev):
    return pltpu.make_async_remote_copy(src_ref=src, dst_ref=dst, send_sem=ssem, recv_sem=rsem, device_id=dev,
                                        device_id_type=MESH)


def _all_gather(srcs, name):
    n = len(srcs)

    def body(*refs):
        src, out = refs[:n], refs[n:2 * n]
        ssem, rsem, fssem, frsem, lsem = refs[2 * n:]
        x, y, c = _coords()
        me = 2 * x + y
        sib = (x, y, 1 - c)
        chips = [(1 - x, y), (x, 1 - y), (1 - x, 1 - y)]
        started = []
        local = []
        for i in range(n):
            for h in range(2):
                cp = pltpu.make_async_copy(src[i].at[h], out[i].at[h, me], lsem.at[i, h])
                cp.start()
                local.append(cp)
            for j, (px, py) in enumerate(chips):
                cp = _remote(src[i].at[c], out[i].at[c, me], ssem.at[i, j], rsem.at[i, j], (px, py, c))
                cp.start()
                started.append(cp)
        for i in range(n):
            for j, (px, py) in enumerate(chips):
                slot = out[i].at[c, 2 * px + py]
                _remote(slot, slot, ssem.at[i, j], rsem.at[i, j], (px, py, c)).wait_recv()
                cp = _remote(slot, slot, fssem.at[i, j], frsem.at[i, j], sib)
                cp.start()
                started.append(cp)
        for i in range(n):
            for j, (px, py) in enumerate(chips):
                slot = out[i].at[1 - c, 2 * px + py]
                _remote(slot, slot, fssem.at[i, j], frsem.at[i, j], sib).wait_recv()
        for cp in started:
            cp.wait_send()
        for cp in local:
            cp.wait()

    out_shape = [jax.ShapeDtypeStruct((2, N_CHIPS) + s.shape[1:], s.dtype) for s in srcs]
    return pl.pallas_call(
        body, name=name, in_specs=[ANY] * n, out_specs=[ANY] * n, out_shape=out_shape,
        scratch_shapes=[pltpu.SemaphoreType.DMA((n, 3)), pltpu.SemaphoreType.DMA((n, 3)),
                        pltpu.SemaphoreType.DMA((n, 3)), pltpu.SemaphoreType.DMA((n, 3)),
                        pltpu.SemaphoreType.DMA((n, 2))])(*srcs)


def _sibling_take(gs, name):
    n = len(gs)

    def body(*refs):
        g, out = refs[:n], refs[n:2 * n]
        ssem, rsem = refs[2 * n:]
        x, y, c = _coords()
        sib = (x, y, 1 - c)
        cps = []
        for i in range(n):
            cp = _remote(g[i].at[1 - c], out[i], ssem.at[i], rsem.at[i], sib)
            cp.start()
            cps.append(cp)
        for cp in cps:
            cp.wait()

    out_shape = [jax.ShapeDtypeStruct(s.shape[1:], s.dtype) for s in gs]
    return pl.pallas_call(
        body, name=name, in_specs=[ANY] * n, out_specs=[ANY] * n, out_shape=out_shape,
        scratch_shapes=[pltpu.SemaphoreType.DMA((n,)), pltpu.SemaphoreType.DMA((n,))])(*gs)


def _chip_all_to_all(ps, name):
    n = len(ps)

    def body(*refs):
        p, out = refs[:n], refs[n:2 * n]
        ssem, rsem, lsem = refs[2 * n:]
        x, y, c = _coords()
        me = 2 * x + y
        chips = [(1 - x, y), (x, 1 - y), (1 - x, 1 - y)]
        cps = []
        local = []
        for i in range(n):
            cp = pltpu.make_async_copy(p[i].at[me], out[i].at[me], lsem.at[i])
            cp.start()
            local.append(cp)
            for j, (px, py) in enumerate(chips):
                cp = _remote(p[i].at[2 * px + py], out[i].at[me], ssem.at[i, j], rsem.at[i, j], (px, py, c))
                cp.start()
                cps.append(cp)
        for i in range(n):
            for j, (px, py) in enumerate(chips):
                slot = out[i].at[2 * px + py]
                _remote(slot, slot, ssem.at[i, j], rsem.at[i, j], (px, py, c)).wait_recv()
        for cp in cps:
            cp.wait_send()
        for cp in local:
            cp.wait()

    out_shape = [jax.ShapeDtypeStruct(s.shape, s.dtype) for s in ps]
    return pl.pallas_call(
        body, name=name, in_specs=[ANY] * n, out_specs=[ANY] * n, out_shape=out_shape,
        scratch_shapes=[pltpu.SemaphoreType.DMA((n, 3)), pltpu.SemaphoreType.DMA((n, 3)),
                        pltpu.SemaphoreType.DMA((n,))])(*ps)


def _sibling_join(fs, name):
    n = len(fs)

    def body(*refs):
        f, out = refs[:n], refs[n:2 * n]
        ssem, rsem, lsem = refs[2 * n:]
        x, y, c = _coords()
        sib = (x, y, 1 - c)
        cps = []
        local = []
        for i in range(n):
            cp = pltpu.make_async_copy(f[i], out[i].at[c], lsem.at[i])
            cp.start()
            local.append(cp)
            cp = _remote(f[i], out[i].at[c], ssem.at[i], rsem.at[i], sib)
            cp.start()
            cps.append(cp)
        for i in range(n):
            slot = out[i].at[1 - c]
            _remote(slot, slot, ssem.at[i], rsem.at[i], sib).wait_recv()
        for cp in cps:
            cp.wait_send()
        for cp in local:
            cp.wait()

    out_shape = [jax.ShapeDtypeStruct((2,) + s.shape, s.dtype) for s in fs]
    return pl.pallas_call(
        body, name=name, in_specs=[ANY] * n, out_specs=[ANY] * n, out_shape=out_shape,
        scratch_shapes=[pltpu.SemaphoreType.DMA((n,)), pltpu.SemaphoreType.DMA((n,)),
                        pltpu.SemaphoreType.DMA((n,))])(*fs)


def _allreduce_small(vec, name):
    r = vec.shape[0]

    def body(v_ref, o_ref, buf, ssem, rsem):
        x, y, c = _coords()
        my = 4 * x + 2 * y + c
        buf[my] = v_ref[...]
        cps = []
        for dd in range(1, N_DEV):
            px = 1 - x if (dd >> 2) & 1 else x
            py = 1 - y if (dd >> 1) & 1 else y
            pc = 1 - c if dd & 1 else c
            cp = _remote(v_ref, buf.at[my], ssem.at[dd - 1], rsem.at[dd - 1], (px, py, pc))
            cp.start()
            cps.append(cp)
        for dd in range(1, N_DEV):
            px = 1 - x if (dd >> 2) & 1 else x
            py = 1 - y if (dd >> 1) & 1 else y
            pc = 1 - c if dd & 1 else c
            slot = buf.at[4 * px + 2 * py + pc]
            _remote(slot, slot, ssem.at[dd - 1], rsem.at[dd - 1], (px, py, pc)).wait_recv()
        tot = buf[0]
        for k in range(1, N_DEV):
            tot = tot + buf[k]
        o_ref[...] = tot
        for cp in cps:
            cp.wait_send()

    vm = pl.BlockSpec(memory_space=pltpu.VMEM)
    return pl.pallas_call(
        body, name=name, in_specs=[vm], out_specs=vm, out_shape=jax.ShapeDtypeStruct((r, LANE), F32),
        scratch_shapes=[pltpu.VMEM((N_DEV, r, LANE), F32), pltpu.SemaphoreType.DMA((N_DEV - 1,)),
                        pltpu.SemaphoreType.DMA((N_DEV - 1,))])(vec)


def _pair_sum(g, l1, cidx, name):
    _, r, c = g.shape
    tr = _pick(r, (512, 256, 128, 64, 32, 16))

    def body(c_ref, g_ref, l_ref, o_ref):
        o_ref[...] = (g_ref[...].astype(F32) + l_ref[...].astype(F32)).astype(BF16)

    gs = pltpu.PrefetchScalarGridSpec(
        num_scalar_prefetch=1, grid=(r // tr,),
        in_specs=[pl.BlockSpec((None, tr, c), lambda i, cr: (cr[0], i, 0)), pl.BlockSpec((tr, c), lambda i, cr: (i, 0))],
        out_specs=pl.BlockSpec((tr, c), lambda i, cr: (i, 0)))
    return pl.pallas_call(body, name=name, grid_spec=gs, out_shape=jax.ShapeDtypeStruct((r, c), BF16),
                          compiler_params=_params())(cidx, g, l1)


def _chip_sum(l2, name):
    _, r, c = l2.shape
    tr = _pick(r, (256, 128, 64, 32, 16))

    def body(l_ref, o_ref):
        tot = l_ref[0].astype(F32) + l_ref[1].astype(F32)
        tot = tot + l_ref[2].astype(F32)
        o_ref[...] = tot + l_ref[3].astype(F32)

    return pl.pallas_call(
        body, name=name, grid=(r // tr,),
        in_specs=[pl.BlockSpec((N_CHIPS, tr, c), lambda i: (0, i, 0))],
        out_specs=pl.BlockSpec((tr, c), lambda i: (i, 0)),
        out_shape=jax.ShapeDtypeStruct((r, c), F32), compiler_params=_params())(l2)


def _adamw(g, w, m, v, name):
    r, c = w.shape
    tr = r
    if r * c * 4 > (2 << 20):
        tr = next(p for p in (256, 128, 64, 32, 16, 8) if r % p == 0 and (p * c * 4 <= (2 << 20) or p == 8))

    def body(g_ref, w_ref, m_ref, v_ref, d_ref, nm_ref, nv_ref):
        gv = g_ref[...]
        mn = ADAM_B1 * m_ref[...] + (1.0 - ADAM_B1) * gv
        vn = ADAM_B2 * v_ref[...] + (1.0 - ADAM_B2) * (gv * gv)
        m_hat = mn / (1.0 - ADAM_B1 ** ADAM_STEP)
        v_hat = vn / (1.0 - ADAM_B2 ** ADAM_STEP)
        d_ref[...] = -ADAM_LR * (m_hat / (jnp.sqrt(v_hat) + ADAM_EPS) + ADAM_WD * w_ref[...])
        nm_ref[...] = mn
        nv_ref[...] = vn

    blk = pl.BlockSpec((tr, c), lambda i: (i, 0))
    osd = jax.ShapeDtypeStruct((r, c), F32)
    return pl.pallas_call(body, name=name, grid=(r // tr,), in_specs=[blk] * 4, out_specs=[blk] * 3,
                          out_shape=[osd] * 3, compiler_params=_params())(g, w, m, v)


def _band_attention(q, k, v, steps):
    g_, l_, h_, dh = q.shape
    nb = -(-l_ // ATTN_BLOCK)
    lp = nb * ATTN_BLOCK
    qb = jnp.pad(q, ((0, 0), (0, lp - l_), (0, 0), (0, 0))).reshape(g_, nb, ATTN_BLOCK, h_, dh)

    def band(t):
        t = jnp.pad(t, ((0, 0), (ATTN_BLOCK, lp - l_), (0, 0), (0, 0)))
        t = t.reshape(g_, nb + 1, ATTN_BLOCK, h_, dh)
        return jnp.concatenate([t[:, :-1], t[:, 1:]], axis=2)

    kw, vw = band(k), band(v)
    s = jnp.einsum('gnqhd,gnkhd->gnhqk', qb, kw, preferred_element_type=F32) * (dh ** -0.5)
    i = jnp.arange(ATTN_BLOCK)[:, None]
    j = jnp.arange(2 * ATTN_BLOCK)[None, :]
    dist = i + ATTN_BLOCK - j
    kpos = jnp.arange(nb)[:, None, None] * ATTN_BLOCK - ATTN_BLOCK + j
    valid = (dist >= 0) & (dist <= steps) & (kpos >= 0)
    s = jnp.where(valid[:, None], s, -jnp.inf)
    m = jnp.max(s, axis=-1, keepdims=True)
    p = jnp.exp(s - m)
    den = jnp.sum(p, axis=-1)
    num = jnp.einsum('gnhqk,gnkhd->gnqhd', p, vw.astype(F32))
    m = m[..., 0].transpose(0, 1, 3, 2).reshape(g_, lp, h_)[:, :l_]
    den = den.transpose(0, 1, 3, 2).reshape(g_, lp, h_)[:, :l_]
    num = num.reshape(g_, lp, h_, dh)[:, :l_]
    return m, num, den


def _dilated_attention(q, k, v):
    b_, s_, h_, dh = q.shape
    ms, nums, dens = [], [], []
    for window, dil in DILATED_CONFIGS:
        l_ = s_ // dil

        def to_res(t):
            return t.reshape(b_, l_, dil, h_, dh).transpose(0, 2, 1, 3, 4).reshape(b_ * dil, l_, h_, dh)

        def from_res(t):
            rest = t.shape[2:]
            return jnp.swapaxes(t.reshape(b_, dil, l_, *rest), 1, 2).reshape(b_, s_, *rest)

        m, num, den = _band_attention(to_res(q), to_res(k), to_res(v), window // dil)
        ms.append(from_res(m))
        nums.append(from_res(num))
        dens.append(from_res(den))
    m_all = jnp.stack(ms)
    w = jnp.exp(m_all - jnp.max(m_all, axis=0, keepdims=True))
    num = jnp.sum(w[..., None] * jnp.stack(nums), axis=0)
    den = jnp.sum(w * jnp.stack(dens), axis=0)
    return num / den[..., None]


def _causal_conv(x, w):
    s_ = x.shape[1]
    xp = jnp.pad(x, ((0, 0), (CONV_WIDTH - 1, 0), (0, 0)))
    return sum(w[i] * xp[:, i:i + s_] for i in range(CONV_WIDTH))


def _gated_delta_rule(q, k, v, g, beta):
    b_, s_, h_, dk = q.shape
    n_ = s_ // CHUNK

    def chunks(t):
        return jnp.swapaxes(t.reshape(b_, n_, CHUNK, h_, *t.shape[3:]), 2, 3)

    q, k, v, g, beta = (chunks(t.astype(F32)) for t in (q, k, v, g, beta))
    gc = jnp.cumsum(g, axis=-1)
    idx = jnp.arange(CHUNK)
    incl = idx[:, None] >= idx[None, :]
    strict = idx[:, None] > idx[None, :]
    decay = jnp.exp(jnp.where(incl, gc[..., :, None] - gc[..., None, :], -jnp.inf))
    kb = k * beta[..., None]
    kk = jnp.einsum('bnhid,bnhjd->bnhij', kb, k)
    a = jnp.where(strict, kk * decay, 0.0) + jnp.eye(CHUNK, dtype=F32)
    rhs = jnp.concatenate([kb * jnp.exp(gc)[..., None], v * beta[..., None]], axis=-1)
    sol = lax.linalg.triangular_solve(a, rhs, left_side=True, lower=True, unit_diagonal=True)
    w_c, u_c = sol[..., :dk], sol[..., dk:]
    qk = jnp.einsum('bnhid,bnhjd->bnhij', q, k) * decay
    q_dec = q * jnp.exp(gc)[..., None]
    k_dec = k * jnp.exp(gc[..., -1:] - gc)[..., None]
    g_last = jnp.exp(gc[..., -1])

    def step(state, xs):
        wc, uc, qkc, qdc, kdc, glc = xs
        v_new = uc - jnp.einsum('bhcd,bhdv->bhcv', wc, state)
        o = jnp.einsum('bhcd,bhdv->bhcv', qdc, state) + jnp.einsum('bhij,bhjv->bhiv', qkc, v_new)
        state = state * glc[..., None, None] + jnp.einsum('bhcd,bhcv->bhdv', kdc, v_new)
        return state, o

    xs = tuple(jnp.moveaxis(t, 1, 0) for t in (w_c, u_c, qk, q_dec, k_dec, g_last))
    state0 = jnp.zeros((b_, h_, dk, v.shape[-1]), F32)
    _, o = lax.scan(step, state0, xs)
    return o.transpose(1, 0, 3, 2, 4).reshape(b_, s_, h_, v.shape[-1])


def _mixer_core(proj, conv_w, a_log, dt_bias, dn_norm, d_attn, d_dn):
    b_, s_, _ = proj.shape
    nh_a, nh_d = d_attn // HEAD_DIM, d_dn // HEAD_DIM
    o0 = 0
    aq = proj[..., o0:o0 + d_attn]; o0 += d_attn
    ak = proj[..., o0:o0 + d_attn]; o0 += d_attn
    av = proj[..., o0:o0 + d_attn]; o0 += d_attn
    dqkv = proj[..., o0:o0 + 3 * d_dn]; o0 += 3 * d_dn
    dz = proj[..., o0:o0 + d_dn]; o0 += d_dn
    db = proj[..., o0:o0 + nh_d]; o0 += nh_d
    da = proj[..., o0:o0 + nh_d]
    heads_a = lambda t: t.reshape(b_, s_, nh_a, HEAD_DIM)
    attn = _dilated_attention(heads_a(aq), heads_a(ak), heads_a(av)).reshape(b_, s_, d_attn)
    dqkv = jax.nn.silu(_causal_conv(dqkv, conv_w))
    dq, dk, dv = jnp.split(dqkv, 3, axis=-1)
    heads_b = lambda t: t.reshape(b_, s_, nh_d, HEAD_DIM)
    dq, dk, dv = heads_b(dq), heads_b(dk), heads_b(dv)
    l2 = lambda t: t * lax.rsqrt(jnp.sum(t * t, axis=-1, keepdims=True) + EPS)
    dq = l2(dq) * (HEAD_DIM ** -0.5)
    dk = l2(dk)
    beta = jax.nn.sigmoid(db)
    g = -jnp.exp(a_log) * jax.nn.softplus(da + dt_bias)
    o = _gated_delta_rule(dq, dk, dv, g, beta)
    o = o * lax.rsqrt(jnp.mean(o * o, axis=-1, keepdims=True) + EPS) * dn_norm
    o = o * jax.nn.silu(heads_b(dz))
    return jnp.concatenate([attn, o.reshape(b_, s_, d_dn)], axis=-1)


def _w_in_windows(ws):
    w0 = [(ws * k) // LANE * LANE for k in range(N_CHIPS)]
    sh = [ws * k - w0[k] for k in range(N_CHIPS)]
    ww = _ceil_to(max(sh) + ws, 2 * LANE)
    n_tiles = (w0[-1] + ww) // LANE
    tpw = ww // LANE
    tph = tpw // 2
    tab = np.zeros((7, n_tiles), np.int32)
    for t in range(n_tiles):
        ks = [k for k in range(N_CHIPS) if w0[k] // LANE <= t < w0[k] // LANE + tpw]
        k1 = ks[-1]
        lt = t - w0[k1] // LANE
        tab[0, t], tab[1, t], tab[2, t] = lt // tph, k1, lt % tph
        k2 = ks[0] if len(ks) > 1 else k1
        lt2 = t - w0[k2] // LANE
        tab[3, t], tab[4, t], tab[5, t] = lt2 // tph, k2, lt2 % tph
        tab[6, t] = 1 if len(ks) > 1 else 0
        assert len(ks) <= 2
    return w0, sh, ww, n_tiles, tab


def kernel(x, ffn1_norm, ffn1_w_gate, ffn1_w_up, ffn1_w_down, mix_norm, w_in, conv_w, a_log, dt_bias, dn_norm, w_out, ffn2_norm, ffn2_w_gate, ffn2_w_up, ffn2_w_down, final_norm, loss_target, m_ffn1_norm, m_ffn1_w_gate, m_ffn1_w_up, m_ffn1_w_down, m_mix_norm, m_w_in, m_conv_w, m_a_log, m_dt_bias, m_dn_norm, m_w_out, m_ffn2_norm, m_ffn2_w_gate, m_ffn2_w_up, m_ffn2_w_down, m_final_norm, v_ffn1_norm, v_ffn1_w_gate, v_ffn1_w_up, v_ffn1_w_down, v_mix_norm, v_w_in, v_conv_w, v_a_log, v_dt_bias, v_dn_norm, v_w_out, v_ffn2_norm, v_ffn2_w_gate, v_ffn2_w_up, v_ffn2_w_down, v_final_norm):
    bl, s_, d = x.shape
    t = bl * s_
    fs = ffn1_w_gate.shape[1]
    hp = _ceil_to(-(-fs // 2), LANE)
    ws = w_in.shape[1]
    d_mix = w_out.shape[0] * N_CHIPS
    d_attn = d_dn = d_mix // 2
    nh_d = d_dn // HEAD_DIM
    d_in = 3 * d_attn + 4 * d_dn + 2 * nh_d
    cs = conv_w.shape[1]
    assert ws * N_CHIPS == d_in and cs * N_CHIPS == 3 * d_dn

    xi, yi, ci = lax.axis_index("x"), lax.axis_index("y"), lax.axis_index("c")
    me = 2 * xi + yi
    cidx = jnp.reshape(ci, (1,)).astype(jnp.int32)

    w0, sh, ww, n_tiles, tab = _w_in_windows(ws)
    shift = (ws * me) % LANE
    w_in_win = lax.dynamic_update_slice(jnp.zeros((d, ww), F32), w_in, (jnp.int32(0), shift))
    rows_tr = math.gcd(hp, fs)
    pieces = [
        _cast_split_cols(ffn1_w_gate, hp, "cast_g1"),
        _cast_split_cols(ffn1_w_up, hp, "cast_u1"),
        _cast_split_rows(ffn1_w_down, hp, rows_tr, "cast_d1").reshape(2, hp, d),
        _cast_split_cols(w_in_win, ww // 2, "cast_in"),
        _cast_split_rows(w_out, w_out.shape[0] // 2, w_out.shape[0] // 2, "cast_out").reshape(2, w_out.shape[0] // 2, d),
        _cast_split_cols(ffn2_w_gate, hp, "cast_g2"),
        _cast_split_cols(ffn2_w_up, hp, "cast_u2"),
        _cast_split_rows(ffn2_w_down, hp, rows_tr, "cast_d2").reshape(2, hp, d),
        jnp.pad(conv_w, ((0, 8 - CONV_WIDTH), (0, 0))).reshape(8, 2, cs // 2).transpose(1, 0, 2),
    ]
    gathered = _all_gather(pieces, "all_gather_weights")
    wg1, wu1, wd1, win_all, wout, wg2, wu2, wd2, conv_all = gathered
    wg1, wu1, wg2, wu2 = (a.reshape(8, d, hp) for a in (wg1, wu1, wg2, wu2))
    wd1, wd2 = (a.reshape(8, hp, d) for a in (wd1, wd2))
    wout = wout.reshape(8, w_out.shape[0] // 2, d)
    win_full = _combine_windows(win_all, jnp.asarray(tab), n_tiles, "combine_w_in")
    conv_full = conv_all.transpose(2, 1, 0, 3).reshape(8, 3 * d_dn)[:CONV_WIDTH]
    npc = 8
    ident = lambda p: p
    cat_map = lambda p: 2 * (p % N_CHIPS) + p // N_CHIPS

    h0 = x.reshape(t, d)
    n1 = _rms_fwd(h0, ffn1_norm, "rms1")
    a1, b1, s1 = _ffn_up(n1, wg1, wu1, "ffn1_up")
    h1 = _mm_pieces_resid(s1, wd1, h0, 0.5, ident, "ffn1_down")
    n2 = _rms_fwd(h1, mix_norm, "rms2")
    proj = _mm2d(n2, win_full, NN, F32, "in_proj")
    core = lambda pr, cw, al, dtb, dnn: _mixer_core(pr.reshape(bl, s_, -1), cw, al, dtb, dnn, d_attn, d_dn)
    cat, core_vjp = jax.vjp(core, proj, conv_full, a_log, dt_bias, dn_norm)
    cat_b = cat.reshape(t, d_mix).astype(BF16)
    h2 = _mm_pieces_resid(cat_b, wout, h1, 1.0, cat_map, "out_proj")
    n3 = _rms_fwd(h2, ffn2_norm, "rms3")
    a3, b3, s3 = _ffn_up(n3, wg2, wu2, "ffn2_up")
    h3 = _mm_pieces_resid(s3, wd2, h2, 0.5, ident, "ffn2_down")

    dh3, dh3b, dwf_p, lsq_p = _final_loss(h3, final_norm, loss_target.reshape(t, d), "final_loss")
    da3, db3 = _ffn_bwd_hidden(dh3b, wd2, a3, b3, "ffn2_bwd_hidden")
    g_wd2 = _grad_rows_pieces(s3, dh3b, 0.5, ident, npc, "ffn2_grad_down")
    g_wg2, g_wu2 = _grad_cols_pieces(n3, da3, db3, npc, "ffn2_grad_up")
    dn3 = _ffn_bwd_input(da3, db3, wg2, wu2, "ffn2_bwd_input")
    dh2, dh2b, dw3_p = _rms_bwd(dn3, h2, ffn2_norm, dh3, "rms3_bwd")

    dcat = _mm_nt_pieces_out(dh2b, wout, cat_map, "out_proj_bwd")
    g_wout = _grad_rows_pieces(cat_b, dh2b, 1.0, cat_map, npc, "out_proj_grad")
    dproj, dconv, dalog, ddtb, ddnn = core_vjp(dcat.astype(F32).reshape(bl, s_, d_mix))
    dproj_b = dproj.astype(BF16)
    g_win_full = _mm2d(n2, dproj_b, TN, BF16, "in_proj_grad")
    dn2 = _mm2d(dproj_b, win_full, NT, F32, "in_proj_bwd")
    dh1, dh1b, dwm_p = _rms_bwd(dn2, h1, mix_norm, dh2, "rms2_bwd")

    da1, db1 = _ffn_bwd_hidden(dh1b, wd1, a1, b1, "ffn1_bwd_hidden")
    g_wd1 = _grad_rows_pieces(s1, dh1b, 0.5, ident, npc, "ffn1_grad_down")
    g_wg1, g_wu1 = _grad_cols_pieces(n1, da1, db1, npc, "ffn1_grad_up")
    dn1 = _ffn_bwd_input(da1, db1, wg1, wu1, "ffn1_bwd_input")
    dh0, _, dw1_p = _rms_bwd(dn1, h0, ffn1_norm, dh1, "rms1_bwd")
    grad_x = dh0.reshape(bl, s_, d)

    wh = ww // 2
    g_win = jnp.stack([jnp.stack([g_win_full[:, w0[k] + wh * h: w0[k] + wh * (h + 1)] for k in range(N_CHIPS)])
                       for h in range(2)])
    big = [g_wg1, g_wu1, g_wd1, g_win, g_wout, g_wg2, g_wu2, g_wd2]
    big = [g.reshape((2, N_CHIPS * g.shape[-2], g.shape[-1])) for g in big]
    from_sib = _sibling_take(big, "rs_sibling_take")
    pair = [_pair_sum(g, l, cidx, f"rs_pair_sum_{i}") for i, (g, l) in enumerate(zip(big, from_sib))]
    pair = [p.reshape(N_CHIPS, p.shape[0] // N_CHIPS, p.shape[1]) for p in pair]
    from_chips = _chip_all_to_all(pair, "rs_chip_all_to_all")
    halves = [_chip_sum(l, f"rs_chip_sum_{i}") for i, l in enumerate(from_chips)]
    full = _sibling_join(halves, "rs_sibling_join")
    f_wg1, f_wu1, f_wd1, f_win, f_wout, f_wg2, f_wu2, f_wd2 = full

    unpad_cols = lambda f: jnp.concatenate([f[0], f[1][:, :fs - hp]], axis=1)
    unpad_rows = lambda f: f.reshape(2 * f.shape[1], f.shape[2])[:fs]
    gw = {
        "ffn1_w_gate": unpad_cols(f_wg1), "ffn1_w_up": unpad_cols(f_wu1), "ffn1_w_down": unpad_rows(f_wd1),
        "w_in": lax.dynamic_slice(jnp.concatenate([f_win[0], f_win[1]], axis=1), (jnp.int32(0), shift), (d, ws)),
        "w_out": f_wout.reshape(w_out.shape),
        "ffn2_w_gate": unpad_cols(f_wg2), "ffn2_w_up": unpad_cols(f_wu2), "ffn2_w_down": unpad_rows(f_wd2),
    }

    def lanes(v):
        v = v.reshape(-1)
        return jnp.pad(v, (0, _ceil_to(v.shape[0], LANE) - v.shape[0])).reshape(-1, LANE)

    small = [dw1_p.sum(0), dwm_p.sum(0), dw3_p.sum(0), dwf_p.sum(0), ddnn, dalog, ddtb,
             (0.5 / d) * jnp.sum(lsq_p).reshape(1), dconv]
    rows = [lanes(v) for v in small]
    offs = np.cumsum([0] + [r.shape[0] for r in rows])
    packed = jnp.concatenate(rows, axis=0)
    packed = jnp.pad(packed, ((0, _ceil_to(packed.shape[0], 8) - packed.shape[0]), (0, 0)))
    red = _allreduce_small(packed, "allreduce_small")
    take = lambda i, shape: red[offs[i]:offs[i + 1]].reshape(-1)[:int(np.prod(shape))].reshape(shape)
    gw["ffn1_norm"] = take(0, (d,))
    gw["mix_norm"] = take(1, (d,))
    gw["ffn2_norm"] = take(2, (d,))
    gw["final_norm"] = take(3, (d,))
    gw["dn_norm"] = take(4, dn_norm.shape)
    gw["a_log"] = take(5, a_log.shape)
    gw["dt_bias"] = take(6, dt_bias.shape)
    loss = take(7, (1,)).reshape(())
    gw["conv_w"] = lax.dynamic_slice(take(8, (CONV_WIDTH, 3 * d_dn)), (jnp.int32(0), me * cs), (CONV_WIDTH, cs))

    names = ['ffn1_norm', 'ffn1_w_gate', 'ffn1_w_up', 'ffn1_w_down', 'mix_norm', 'w_in', 'conv_w', 'a_log', 'dt_bias',
             'dn_norm', 'w_out', 'ffn2_norm', 'ffn2_w_gate', 'ffn2_w_up', 'ffn2_w_down', 'final_norm']
    wv = dict(zip(names, (ffn1_norm, ffn1_w_gate, ffn1_w_up, ffn1_w_down, mix_norm, w_in, conv_w, a_log, dt_bias,
                          dn_norm, w_out, ffn2_norm, ffn2_w_gate, ffn2_w_up, ffn2_w_down, final_norm)))
    mv = dict(zip(names, (m_ffn1_norm, m_ffn1_w_gate, m_ffn1_w_up, m_ffn1_w_down, m_mix_norm, m_w_in, m_conv_w, m_a_log,
                          m_dt_bias, m_dn_norm, m_w_out, m_ffn2_norm, m_ffn2_w_gate, m_ffn2_w_up, m_ffn2_w_down,
                          m_final_norm)))
    vv = dict(zip(names, (v_ffn1_norm, v_ffn1_w_gate, v_ffn1_w_up, v_ffn1_w_down, v_mix_norm, v_w_in, v_conv_w, v_a_log,
                          v_dt_bias, v_dn_norm, v_w_out, v_ffn2_norm, v_ffn2_w_gate, v_ffn2_w_up, v_ffn2_w_down,
                          v_final_norm)))
    delta, new_m, new_v = {}, {}, {}
    small_names = [n for n in names if wv[n].ndim == 1 or n == "conv_w"]
    for n in names:
        if n in small_names:
            continue
        delta[n], new_m[n], new_v[n] = _adamw(gw[n], wv[n], mv[n], vv[n], f"adamw_{n}")
    srows = {n: lanes(gw[n]).shape[0] for n in small_names}
    soffs = np.cumsum([0] + [srows[n] for n in small_names])
    stot = _ceil_to(int(soffs[-1]), 8)

    def pack(dct):
        p = jnp.concatenate([lanes(dct[n]) for n in small_names], axis=0)
        return jnp.pad(p, ((0, stot - p.shape[0]), (0, 0)))

    sd, sm, sv = _adamw(pack(gw), pack(wv), pack(mv), pack(vv), "adamw_small")
    for i, n in enumerate(small_names):
        cut = lambda p: p[soffs[i]:soffs[i + 1]].reshape(-1)[:wv[n].size].reshape(wv[n].shape)
        delta[n], new_m[n], new_v[n] = cut(sd), cut(sm), cut(sv)

    return (loss, grad_x, *[gw[n] for n in names], *[delta[n] for n in names], *[new_m[n] for n in names],
            *[new_v[n] for n in names])
```

```python
import functools
import math

import jax
import jax.numpy as jnp
import numpy as np
from jax import lax
from jax.experimental import pallas as pl
from jax.experimental.pallas import tpu as pltpu

F32 = jnp.float32
BF16 = jnp.bfloat16
MESH = pl.DeviceIdType.MESH
ANY = pl.BlockSpec(memory_space=pl.ANY)

LANE = 128
N_CHIPS = 4
N_DEV = 8
EPS = 1e-6
HEAD_DIM = 128
CONV_WIDTH = 4
CHUNK = 64
ATTN_BLOCK = 128
DILATED_CONFIGS = ((128, 1), (512, 4), (2048, 16))
VMEM_LIMIT = 52 * 1024 * 1024

ADAM_LR = 0.001
ADAM_B1 = 0.9
ADAM_B2 = 0.999
ADAM_EPS = 1e-08
ADAM_WD = 0.01
ADAM_STEP = 10

NN = (((1,), (0,)), ((), ()))
NT = (((1,), (1,)), ((), ()))
TN = (((0,), (0,)), ((), ()))


def _ceil_to(v, m):
    return -(-v // m) * m


def _params(vmem=VMEM_LIMIT):
    return pltpu.CompilerParams(vmem_limit_bytes=vmem)


def _gemm(name, grid, pairs, dn, acc_shape, n_acc, extras, outs, epilogue, n_prefetch=0, prefetch=()):
    n_pairs, n_ex, n_out = len(pairs), len(extras), len(outs)
    kax = len(grid) - 1
    nk = grid[kax]

    def body(*refs):
        refs = refs[n_prefetch:]
        ins = refs[: 2 * n_pairs]
        ex = refs[2 * n_pairs: 2 * n_pairs + n_ex]
        out_refs = refs[2 * n_pairs + n_ex: 2 * n_pairs + n_ex + n_out]
        accs = refs[2 * n_pairs + n_ex + n_out:]
        k = pl.program_id(kax)

        @pl.when(k == 0)
        def _():
            for acc in accs:
                acc[...] = jnp.zeros(acc.shape, F32)

        for q in range(n_pairs):
            a = ins[2 * q][...]
            b = ins[2 * q + 1][...]
            if a.dtype != BF16:
                a = a.astype(BF16)
            if b.dtype != BF16:
                b = b.astype(BF16)
            accs[pairs[q][4]][...] += lax.dot_general(a, b, dn, preferred_element_type=F32)

        @pl.when(k == nk - 1)
        def _():
            res = epilogue([acc[...] for acc in accs], [e[...] for e in ex])
            for o, r in zip(out_refs, res):
                o[...] = r.astype(o.dtype)

    in_specs = []
    args = []
    for a, a_spec, b, b_spec, _ in pairs:
        in_specs += [a_spec, b_spec]
        args += [a, b]
    for e, e_spec in extras:
        in_specs.append(e_spec)
        args.append(e)
    out_shape = [o for o, _ in outs]
    out_specs = [s for _, s in outs]
    scratch = [pltpu.VMEM(acc_shape, F32) for _ in range(n_acc)]
    if n_prefetch:
        gs = pltpu.PrefetchScalarGridSpec(num_scalar_prefetch=n_prefetch, grid=grid, in_specs=in_specs,
                                          out_specs=out_specs, scratch_shapes=scratch)
        return pl.pallas_call(body, name=name, grid_spec=gs, out_shape=out_shape,
                              compiler_params=_params())(*prefetch, *args)
    return pl.pallas_call(body, name=name, grid=grid, in_specs=in_specs, out_specs=out_specs,
                          out_shape=out_shape, scratch_shapes=scratch, compiler_params=_params())(*args)


def _pick(n, prefs):
    for p in prefs:
        if n % p == 0:
            return p
    return n


def _sigmoid(v):
    return 1.0 / (1.0 + jnp.exp(-v))


def _ffn_up(n, wg, wu, name):
    t, d = n.shape
    npieces, _, hp = wg.shape
    tm = _pick(t, (512, 256, 128, 64, 32, 16))
    tk = _pick(d, (1024, 512, 256, 128))
    grid = (t // tm, npieces, d // tk)
    a_spec = pl.BlockSpec((tm, tk), lambda i, p, k: (i, k))
    w_spec = pl.BlockSpec((None, tk, hp), lambda i, p, k: (p, k, 0))
    o_spec = pl.BlockSpec((tm, hp), lambda i, p, k: (i, p))
    osd = jax.ShapeDtypeStruct((t, npieces * hp), BF16)

    def epi(accs, ex):
        a, b = accs
        return a, b, a * _sigmoid(a) * b

    return _gemm(name, grid, [(n, a_spec, wg, w_spec, 0), (n, a_spec, wu, w_spec, 1)], NN, (tm, hp), 2, [],
                 [(osd, o_spec)] * 3, epi)


def _mm_pieces_resid(a, w, resid, scale, amap, name):
    t = a.shape[0]
    npieces, kp, n = w.shape
    tm = _pick(t, (512, 256, 128, 64, 32, 16))
    tn = _pick(n, (1024, 512, 256, 128))
    grid = (t // tm, n // tn, npieces)
    a_spec = pl.BlockSpec((tm, kp), lambda i, j, p: (i, amap(p)))
    w_spec = pl.BlockSpec((None, kp, tn), lambda i, j, p: (p, 0, j))
    r_spec = pl.BlockSpec((tm, tn), lambda i, j, p: (i, j))

    def epi(accs, ex):
        return (ex[0] + scale * accs[0],)

    return _gemm(name, grid, [(a, a_spec, w, w_spec, 0)], NN, (tm, tn), 1, [(resid, r_spec)],
                 [(jax.ShapeDtypeStruct((t, n), F32), r_spec)], epi)[0]


def _ffn_bwd_hidden(dh, wd, a, b, name):
    t, d = dh.shape
    npieces, hp, _ = wd.shape
    tm = _pick(t, (512, 256, 128, 64, 32, 16))
    tk = _pick(d, (1024, 512, 256, 128))
    grid = (t // tm, npieces, d // tk)
    a_spec = pl.BlockSpec((tm, tk), lambda i, p, k: (i, k))
    w_spec = pl.BlockSpec((None, hp, tk), lambda i, p, k: (p, 0, k))
    o_spec = pl.BlockSpec((tm, hp), lambda i, p, k: (i, p))
    osd = jax.ShapeDtypeStruct((t, npieces * hp), BF16)

    def epi(accs, ex):
        ds = 0.5 * accs[0]
        av = ex[0].astype(F32)
        bv = ex[1].astype(F32)
        sg = _sigmoid(av)
        da = ds * bv * (sg * (1.0 + av * (1.0 - sg)))
        db = ds * (av * sg)
        return da, db

    return _gemm(name, grid, [(dh, a_spec, wd, w_spec, 0)], NT, (tm, hp), 1, [(a, o_spec), (b, o_spec)],
                 [(osd, o_spec)] * 2, epi)


def _mm_nt_pieces_out(dh, w, omap, name):
    t, d = dh.shape
    npieces, npp, _ = w.shape
    tm = _pick(t, (512, 256, 128, 64, 32, 16))
    tk = _pick(d, (1024, 512, 256, 128))
    grid = (t // tm, npieces, d // tk)
    a_spec = pl.BlockSpec((tm, tk), lambda i, p, k: (i, k))
    w_spec = pl.BlockSpec((None, npp, tk), lambda i, p, k: (p, 0, k))
    o_spec = pl.BlockSpec((tm, npp), lambda i, p, k: (i, omap(p)))
    return _gemm(name, grid, [(dh, a_spec, w, w_spec, 0)], NT, (tm, npp), 1, [],
                 [(jax.ShapeDtypeStruct((t, npieces * npp), BF16), o_spec)], lambda accs, ex: (accs[0],))[0]


def _grad_rows_pieces(x, dy, scale, amap, npieces, name):
    t, n = dy.shape
    mp = x.shape[1] // npieces
    tn = _pick(n, (1024, 512, 256, 128))
    tk = _pick(t, (512, 256, 128, 64, 32, 16))
    grid = (npieces, n // tn, t // tk)
    x_spec = pl.BlockSpec((tk, mp), lambda p, j, k: (k, amap(p)))
    y_spec = pl.BlockSpec((tk, tn), lambda p, j, k: (k, j))
    o_spec = pl.BlockSpec((None, mp, tn), lambda p, j, k: (p, 0, j))
    return _gemm(name, grid, [(x, x_spec, dy, y_spec, 0)], TN, (mp, tn), 1, [],
                 [(jax.ShapeDtypeStruct((npieces, mp, n), BF16), o_spec)], lambda accs, ex: (scale * accs[0],))[0]


def _grad_cols_pieces(n, da, db, npieces, name):
    t, d = n.shape
    hp = da.shape[1] // npieces
    tm = _pick(d, (512, 256, 128))
    tk = _pick(t, (512, 256, 128, 64, 32, 16))
    grid = (npieces, d // tm, t // tk)
    n_spec = pl.BlockSpec((tk, tm), lambda p, i, k: (k, i))
    g_spec = pl.BlockSpec((tk, hp), lambda p, i, k: (k, p))
    o_spec = pl.BlockSpec((None, tm, hp), lambda p, i, k: (p, i, 0))
    osd = jax.ShapeDtypeStruct((npieces, d, hp), BF16)
    return _gemm(name, grid, [(n, n_spec, da, g_spec, 0), (n, n_spec, db, g_spec, 1)], TN, (tm, hp), 2, [],
                 [(osd, o_spec)] * 2, lambda accs, ex: (accs[0], accs[1]))


def _ffn_bwd_input(da, db, wg, wu, name):
    t = da.shape[0]
    npieces, d, hp = wg.shape
    tm = _pick(t, (512, 256, 128, 64, 32, 16))
    tn = _pick(d, (1024, 512, 256, 128))
    grid = (t // tm, d // tn, npieces)
    g_spec = pl.BlockSpec((tm, hp), lambda i, j, p: (i, p))
    w_spec = pl.BlockSpec((None, tn, hp), lambda i, j, p: (p, j, 0))
    o_spec = pl.BlockSpec((tm, tn), lambda i, j, p: (i, j))
    return _gemm(name, grid, [(da, g_spec, wg, w_spec, 0), (db, g_spec, wu, w_spec, 0)], NT, (tm, tn), 1, [],
                 [(jax.ShapeDtypeStruct((t, d), F32), o_spec)], lambda accs, ex: (accs[0],))[0]


def _mm2d(a, b, dn, out_dtype, name):
    if dn == NN:
        m, kk = a.shape
        n = b.shape[1]
    elif dn == NT:
        m, kk = a.shape
        n = b.shape[0]
    else:
        kk, m = a.shape
        n = b.shape[1]
    tm = _pick(m, (512, 256, 128, 64, 32, 16))
    tn = _pick(n, (768, 1024, 512, 256, 128))
    tk = _pick(kk, (768, 1024, 512, 256, 128, 64, 32, 16))
    grid = (m // tm, n // tn, kk // tk)
    if dn == TN:
        a_spec = pl.BlockSpec((tk, tm), lambda i, j, k: (k, i))
    else:
        a_spec = pl.BlockSpec((tm, tk), lambda i, j, k: (i, k))
    if dn == NT:
        b_spec = pl.BlockSpec((tn, tk), lambda i, j, k: (j, k))
    else:
        b_spec = pl.BlockSpec((tk, tn), lambda i, j, k: (k, j))
    o_spec = pl.BlockSpec((tm, tn), lambda i, j, k: (i, j))
    return _gemm(name, grid, [(a, a_spec, b, b_spec, 0)], dn, (tm, tn), 1, [],
                 [(jax.ShapeDtypeStruct((m, n), out_dtype), o_spec)], lambda accs, ex: (accs[0],))[0]


def _row_tile(t):
    return _pick(t, (256, 128, 64, 32, 16, 8))


def _rms_fwd(x, w, name):
    t, d = x.shape
    tm = _row_tile(t)

    def body(x_ref, w_ref, o_ref):
        xv = x_ref[...]
        r = lax.rsqrt(jnp.mean(xv * xv, axis=-1, keepdims=True) + EPS)
        o_ref[...] = (xv * r * w_ref[...]).astype(BF16)

    return pl.pallas_call(
        body, name=name, grid=(t // tm,),
        in_specs=[pl.BlockSpec((tm, d), lambda i: (i, 0)), pl.BlockSpec((1, d), lambda i: (0, 0))],
        out_specs=pl.BlockSpec((tm, d), lambda i: (i, 0)),
        out_shape=jax.ShapeDtypeStruct((t, d), BF16), compiler_params=_params())(x, w.reshape(1, d))


def _rms_bwd(dn, x, w, dres, name):
    t, d = x.shape
    tm = _row_tile(t)

    def body(dn_ref, x_ref, w_ref, r_ref, o_ref, ob_ref, dw_ref):
        i = pl.program_id(0)
        xv = x_ref[...]
        r = lax.rsqrt(jnp.mean(xv * xv, axis=-1, keepdims=True) + EPS)
        xh = xv * r
        dy = dn_ref[...].astype(F32)
        g = dy * w_ref[...]
        dx = r * (g - xh * jnp.mean(g * xh, axis=-1, keepdims=True))
        tot = r_ref[...] + dx
        o_ref[...] = tot
        ob_ref[...] = tot.astype(BF16)
        part = (dy * xh).reshape(tm // 8, 8, d).sum(axis=0)

        @pl.when(i == 0)
        def _():
            dw_ref[...] = part

        @pl.when(i > 0)
        def _():
            dw_ref[...] += part

    row = pl.BlockSpec((tm, d), lambda i: (i, 0))
    return pl.pallas_call(
        body, name=name, grid=(t // tm,),
        in_specs=[row, row, pl.BlockSpec((1, d), lambda i: (0, 0)), row],
        out_specs=[row, row, pl.BlockSpec((8, d), lambda i: (0, 0))],
        out_shape=[jax.ShapeDtypeStruct((t, d), F32), jax.ShapeDtypeStruct((t, d), BF16),
                   jax.ShapeDtypeStruct((8, d), F32)],
        compiler_params=_params())(dn, x, w.reshape(1, d), dres)


def _final_loss(h, w, target, name):
    t, d = h.shape
    tm = _row_tile(t)

    def body(h_ref, w_ref, t_ref, o_ref, ob_ref, dw_ref, ls_ref):
        i = pl.program_id(0)
        xv = h_ref[...]
        r = lax.rsqrt(jnp.mean(xv * xv, axis=-1, keepdims=True) + EPS)
        xh = xv * r
        err = xh * w_ref[...] - t_ref[...]
        dy = err * (1.0 / d)
        g = dy * w_ref[...]
        dx = r * (g - xh * jnp.mean(g * xh, axis=-1, keepdims=True))
        o_ref[...] = dx
        ob_ref[...] = dx.astype(BF16)
        part = (dy * xh).reshape(tm // 8, 8, d).sum(axis=0)
        lpart = (err * err).reshape(tm // 8, 8, d).sum(axis=0)

        @pl.when(i == 0)
        def _():
            dw_ref[...] = part
            ls_ref[...] = lpart

        @pl.when(i > 0)
        def _():
            dw_ref[...] += part
            ls_ref[...] += lpart

    row = pl.BlockSpec((tm, d), lambda i: (i, 0))
    acc = pl.BlockSpec((8, d), lambda i: (0, 0))
    return pl.pallas_call(
        body, name=name, grid=(t // tm,),
        in_specs=[row, pl.BlockSpec((1, d), lambda i: (0, 0)), row],
        out_specs=[row, row, acc, acc],
        out_shape=[jax.ShapeDtypeStruct((t, d), F32), jax.ShapeDtypeStruct((t, d), BF16),
                   jax.ShapeDtypeStruct((8, d), F32), jax.ShapeDtypeStruct((8, d), F32)],
        compiler_params=_params())(h, w.reshape(1, d), target)


def _cast_split_cols(w, hp, me, name):
    r, fs = w.shape
    v1 = fs - hp
    tm = _pick(r, (256, 128, 64, 32, 16))

    def body(me_ref, w_ref, o_ref):
        o_ref[0] = w_ref[:, :hp].astype(BF16)
        if v1 < hp:
            o_ref[1] = jnp.zeros((tm, hp), BF16)
        o_ref[1, :, :v1] = w_ref[:, hp:].astype(BF16)

    gs = pltpu.PrefetchScalarGridSpec(
        num_scalar_prefetch=1, grid=(r // tm,),
        in_specs=[pl.BlockSpec((tm, fs), lambda i, mr: (i, 0))],
        out_specs=pl.BlockSpec((2, None, tm, hp), lambda i, mr: (0, mr[0], i, 0)))
    return pl.pallas_call(body, name=name, grid_spec=gs, out_shape=jax.ShapeDtypeStruct((2, N_CHIPS, r, hp), BF16),
                          compiler_params=_params())(me, w)


def _cast_split_rows(w, hp, tr, me, name):
    fs, c = w.shape
    nvalid = fs // tr
    per = hp // tr

    def body(me_ref, w_ref, o_ref):
        i = pl.program_id(0)

        @pl.when(i < nvalid)
        def _():
            o_ref[...] = w_ref[...].astype(BF16)

        @pl.when(i >= nvalid)
        def _():
            o_ref[...] = jnp.zeros(o_ref.shape, BF16)

    gs = pltpu.PrefetchScalarGridSpec(
        num_scalar_prefetch=1, grid=(2 * per,),
        in_specs=[pl.BlockSpec((tr, c), lambda i, mr: (jnp.minimum(i, nvalid - 1), 0))],
        out_specs=pl.BlockSpec((None, None, tr, c), lambda i, mr: (i // per, mr[0], i % per, 0)))
    return pl.pallas_call(body, name=name, grid_spec=gs, out_shape=jax.ShapeDtypeStruct((2, N_CHIPS, hp, c), BF16),
                          compiler_params=_params())(me, w)


def _combine_windows(wall, tables, n_tiles, name):
    _, _, d, wh = wall.shape
    tpw = wh // LANE

    def body(tab_ref, a_ref, b_ref, o_ref):
        t = pl.program_id(0)
        both = tab_ref[6, t] == 1
        av = a_ref[...]
        bv = b_ref[...]
        o_ref[...] = jnp.where(both, av + bv, av)

    def amap(t, tab):
        return (tab[0, t], tab[1, t], 0, tab[2, t])

    def bmap(t, tab):
        return (tab[3, t], tab[4, t], 0, tab[5, t])

    gs = pltpu.PrefetchScalarGridSpec(
        num_scalar_prefetch=1, grid=(n_tiles,),
        in_specs=[pl.BlockSpec((None, None, d, LANE), amap), pl.BlockSpec((None, None, d, LANE), bmap)],
        out_specs=pl.BlockSpec((d, LANE), lambda t, tab: (0, t)))
    del tpw
    return pl.pallas_call(body, name=name, grid_spec=gs, out_shape=jax.ShapeDtypeStruct((d, n_tiles * LANE), BF16),
                          compiler_params=_params())(tables, wall, wall)


def _coords():
    return lax.axis_index("x"), lax.axis_index("y"), lax.axis_index("c")


def _remote(src, dst, ssem, rsem, dev):
    return pltpu.make_async_remote_copy(src_ref=src, dst_ref=dst, send_sem=ssem, recv_sem=rsem, device_id=dev,
                                        device_id_type=MESH)


def _all_gather(bufs, name):
    n = len(bufs)

    def body(*refs):
        out = refs[n:2 * n]
        ssem, rsem, fssem, frsem = refs[2 * n:]
        x, y, c = _coords()
        me = 2 * x + y
        sib = (x, y, 1 - c)
        chips = [(1 - x, y), (x, 1 - y), (1 - x, 1 - y)]
        started = []
        for i in range(n):
            for j, (px, py) in enumerate(chips):
                mine = out[i].at[c, me]
                cp = _remote(mine, mine, ssem.at[i, j], rsem.at[i, j], (px, py, c))
                cp.start()
                started.append(cp)
        for i in range(n):
            for j, (px, py) in enumerate(chips):
                slot = out[i].at[c, 2 * px + py]
                _remote(slot, slot, ssem.at[i, j], rsem.at[i, j], (px, py, c)).wait_recv()
                cp = _remote(slot, slot, fssem.at[i, j], frsem.at[i, j], sib)
                cp.start()
                started.append(cp)
        for i in range(n):
            for j, (px, py) in enumerate(chips):
                slot = out[i].at[1 - c, 2 * px + py]
                _remote(slot, slot, fssem.at[i, j], frsem.at[i, j], sib).wait_recv()
        for cp in started:
            cp.wait_send()

    out_shape = [jax.ShapeDtypeStruct(b.shape, b.dtype) for b in bufs]
    return pl.pallas_call(
        body, name=name, in_specs=[ANY] * n, out_specs=[ANY] * n, out_shape=out_shape,
        input_output_aliases={i: i for i in range(n)},
        scratch_shapes=[pltpu.SemaphoreType.DMA((n, 3)), pltpu.SemaphoreType.DMA((n, 3)),
                        pltpu.SemaphoreType.DMA((n, 3)), pltpu.SemaphoreType.DMA((n, 3))])(*bufs)


def _sibling_take(gs, name):
    n = len(gs)

    def body(*refs):
        g, out = refs[:n], refs[n:2 * n]
        ssem, rsem = refs[2 * n:]
        x, y, c = _coords()
        sib = (x, y, 1 - c)
        cps = []
        for i in range(n):
            cp = _remote(g[i].at[1 - c], out[i], ssem.at[i], rsem.at[i], sib)
            cp.start()
            cps.append(cp)
        for cp in cps:
            cp.wait()

    out_shape = [jax.ShapeDtypeStruct(s.shape[1:], s.dtype) for s in gs]
    return pl.pallas_call(
        body, name=name, in_specs=[ANY] * n, out_specs=[ANY] * n, out_shape=out_shape,
        scratch_shapes=[pltpu.SemaphoreType.DMA((n,)), pltpu.SemaphoreType.DMA((n,))])(*gs)


def _chip_all_to_all(ps, name):
    n = len(ps)

    def body(*refs):
        p, out = refs[:n], refs[n:2 * n]
        ssem, rsem = refs[2 * n:]
        x, y, c = _coords()
        me = 2 * x + y
        chips = [(1 - x, y), (x, 1 - y), (1 - x, 1 - y)]
        cps = []
        for i in range(n):
            for j, (px, py) in enumerate(chips):
                cp = _remote(p[i].at[2 * px + py], out[i].at[me], ssem.at[i, j], rsem.at[i, j], (px, py, c))
                cp.start()
                cps.append(cp)
        for i in range(n):
            for j, (px, py) in enumerate(chips):
                slot = out[i].at[2 * px + py]
                _remote(slot, slot, ssem.at[i, j], rsem.at[i, j], (px, py, c)).wait_recv()
        for cp in cps:
            cp.wait_send()

    out_shape = [jax.ShapeDtypeStruct(s.shape, s.dtype) for s in ps]
    return pl.pallas_call(
        body, name=name, in_specs=[ANY] * n, out_specs=[ANY] * n, out_shape=out_shape,
        scratch_shapes=[pltpu.SemaphoreType.DMA((n, 3)), pltpu.SemaphoreType.DMA((n, 3))])(*ps)


def _sibling_join(bufs, name):
    n = len(bufs)

    def body(*refs):
        out = refs[n:2 * n]
        ssem, rsem = refs[2 * n:]
        x, y, c = _coords()
        sib = (x, y, 1 - c)
        cps = []
        for i in range(n):
            mine = out[i].at[c]
            cp = _remote(mine, mine, ssem.at[i], rsem.at[i], sib)
            cp.start()
            cps.append(cp)
        for i in range(n):
            slot = out[i].at[1 - c]
            _remote(slot, slot, ssem.at[i], rsem.at[i], sib).wait_recv()
        for cp in cps:
            cp.wait_send()

    out_shape = [jax.ShapeDtypeStruct(b.shape, b.dtype) for b in bufs]
    return pl.pallas_call(
        body, name=name, in_specs=[ANY] * n, out_specs=[ANY] * n, out_shape=out_shape,
        input_output_aliases={i: i for i in range(n)},
        scratch_shapes=[pltpu.SemaphoreType.DMA((n,)), pltpu.SemaphoreType.DMA((n,))])(*bufs)


def _allreduce_small(vec, name):
    r = vec.shape[0]

    def body(v_ref, o_ref, buf, ssem, rsem):
        x, y, c = _coords()
        my = 4 * x + 2 * y + c
        buf[my] = v_ref[...]
        cps = []
        for dd in range(1, N_DEV):
            px = 1 - x if (dd >> 2) & 1 else x
            py = 1 - y if (dd >> 1) & 1 else y
            pc = 1 - c if dd & 1 else c
            cp = _remote(v_ref, buf.at[my], ssem.at[dd - 1], rsem.at[dd - 1], (px, py, pc))
            cp.start()
            cps.append(cp)
        for dd in range(1, N_DEV):
            px = 1 - x if (dd >> 2) & 1 else x
            py = 1 - y if (dd >> 1) & 1 else y
            pc = 1 - c if dd & 1 else c
            slot = buf.at[4 * px + 2 * py + pc]
            _remote(slot, slot, ssem.at[dd - 1], rsem.at[dd - 1], (px, py, pc)).wait_recv()
        tot = buf[0]
        for k in range(1, N_DEV):
            tot = tot + buf[k]
        o_ref[...] = tot
        for cp in cps:
            cp.wait_send()

    vm = pl.BlockSpec(memory_space=pltpu.VMEM)
    return pl.pallas_call(
        body, name=name, in_specs=[vm], out_specs=vm, out_shape=jax.ShapeDtypeStruct((r, LANE), F32),
        scratch_shapes=[pltpu.VMEM((N_DEV, r, LANE), F32), pltpu.SemaphoreType.DMA((N_DEV - 1,)),
                        pltpu.SemaphoreType.DMA((N_DEV - 1,))])(vec)


def _pair_sum(g, l1, cidx, name):
    _, r, c = g.shape
    tr = _pick(r, (512, 256, 128, 64, 32, 16))

    def body(c_ref, g_ref, l_ref, o_ref):
        o_ref[...] = (g_ref[...].astype(F32) + l_ref[...].astype(F32)).astype(BF16)

    gs = pltpu.PrefetchScalarGridSpec(
        num_scalar_prefetch=1, grid=(r // tr,),
        in_specs=[pl.BlockSpec((None, tr, c), lambda i, cr: (cr[0], i, 0)), pl.BlockSpec((tr, c), lambda i, cr: (i, 0))],
        out_specs=pl.BlockSpec((tr, c), lambda i, cr: (i, 0)))
    return pl.pallas_call(body, name=name, grid_spec=gs, out_shape=jax.ShapeDtypeStruct((r, c), BF16),
                          compiler_params=_params())(cidx, g, l1)


def _chip_sum(p, l2, mc, name):
    _, r, c = l2.shape
    tr = _pick(r, (256, 128, 64, 32, 16))

    def body(mc_ref, p_ref, l0, l1, l2_, l3, o_ref):
        me = mc_ref[0]
        pv = p_ref[...].astype(F32)
        tot = None
        for k, lr in enumerate((l0, l1, l2_, l3)):
            term = jnp.where(me == k, pv, lr[...].astype(F32))
            tot = term if tot is None else tot + term
        o_ref[...] = tot

    def other(k):
        return lambda i, mr: (jnp.where(mr[0] == k, (k + 1) % N_CHIPS, k), i, 0)

    gs = pltpu.PrefetchScalarGridSpec(
        num_scalar_prefetch=1, grid=(r // tr,),
        in_specs=[pl.BlockSpec((None, tr, c), lambda i, mr: (mr[0], i, 0))]
        + [pl.BlockSpec((None, tr, c), other(k)) for k in range(N_CHIPS)],
        out_specs=pl.BlockSpec((None, tr, c), lambda i, mr: (mr[1], i, 0)))
    return pl.pallas_call(body, name=name, grid_spec=gs, out_shape=jax.ShapeDtypeStruct((2, r, c), F32),
                          compiler_params=_params())(mc, p, l2, l2, l2, l2)


def _adamw(g, w, m, v, name):
    r, c = w.shape
    tr = r
    if r * c * 4 > (2 << 20):
        tr = next(p for p in (256, 128, 64, 32, 16, 8) if r % p == 0 and (p * c * 4 <= (2 << 20) or p == 8))

    def body(g_ref, w_ref, m_ref, v_ref, d_ref, nm_ref, nv_ref):
        gv = g_ref[...]
        mn = ADAM_B1 * m_ref[...] + (1.0 - ADAM_B1) * gv
        vn = ADAM_B2 * v_ref[...] + (1.0 - ADAM_B2) * (gv * gv)
        m_hat = mn / (1.0 - ADAM_B1 ** ADAM_STEP)
        v_hat = vn / (1.0 - ADAM_B2 ** ADAM_STEP)
        d_ref[...] = -ADAM_LR * (m_hat / (jnp.sqrt(v_hat) + ADAM_EPS) + ADAM_WD * w_ref[...])
        nm_ref[...] = mn
        nv_ref[...] = vn

    blk = pl.BlockSpec((tr, c), lambda i: (i, 0))
    osd = jax.ShapeDtypeStruct((r, c), F32)
    return pl.pallas_call(body, name=name, grid=(r // tr,), in_specs=[blk] * 4, out_specs=[blk] * 3,
                          out_shape=[osd] * 3, compiler_params=_params())(g, w, m, v)


def _attn_probs(q, k, q0, s_len):
    tq = q.shape[0]
    sc = lax.dot_general(q, k, NT, preferred_element_type=F32) * (HEAD_DIM ** -0.5)
    dlt = (q0 + lax.broadcasted_iota(jnp.int32, (tq, s_len), 0)) - lax.broadcasted_iota(jnp.int32, (tq, s_len), 1)
    cnt = jnp.zeros((tq, s_len), F32)
    for window, dil in DILATED_CONFIGS:
        seen = (dlt >= 0) & (dlt <= window) & ((dlt & (dil - 1)) == 0)
        cnt = cnt + jnp.where(seen, 1.0, 0.0)
    live = cnt > 0.0
    m = jnp.max(jnp.where(live, sc, -jnp.inf), axis=-1, keepdims=True)
    p = cnt * jnp.exp(jnp.where(live, sc - m, -jnp.inf))
    return p / jnp.sum(p, axis=-1, keepdims=True)


def _attn_fwd(proj, nh, s_len, name):
    t = proj.shape[0]
    tq = min(256, s_len)
    nq = s_len // tq

    def body(q_ref, k_ref, v_ref, o_ref):
        qi = pl.program_id(2)
        p = _attn_probs(q_ref[...].astype(BF16), k_ref[...].astype(BF16), qi * tq, s_len)
        o_ref[...] = jnp.dot(p.astype(BF16), v_ref[...].astype(BF16), preferred_element_type=F32)

    q_spec = pl.BlockSpec((tq, HEAD_DIM), lambda b, h, qi: (b * nq + qi, h))
    return pl.pallas_call(
        body, name=name, grid=(t // s_len, nh, nq),
        in_specs=[q_spec, pl.BlockSpec((s_len, HEAD_DIM), lambda b, h, qi: (b, nh + h)),
                  pl.BlockSpec((s_len, HEAD_DIM), lambda b, h, qi: (b, 2 * nh + h))],
        out_specs=q_spec, out_shape=jax.ShapeDtypeStruct((t, nh * HEAD_DIM), F32),
        compiler_params=_params())(proj, proj, proj)


def _attn_bwd(proj, o, do, nh, s_len, name):
    t = proj.shape[0]
    tq = min(256, s_len)
    nq = s_len // tq
    scale = HEAD_DIM ** -0.5

    def body(q_ref, k_ref, v_ref, o_ref, do_ref, dq_ref, dk_ref, dv_ref, dk_acc, dv_acc):
        qi = pl.program_id(2)
        q = q_ref[...].astype(BF16)
        k = k_ref[...].astype(BF16)
        p = _attn_probs(q, k, qi * tq, s_len)
        dob = do_ref[...]
        dp = lax.dot_general(dob, v_ref[...].astype(BF16), NT, preferred_element_type=F32)
        delta = jnp.sum(dob.astype(F32) * o_ref[...], axis=-1, keepdims=True)
        ds = (p * (dp - delta)).astype(BF16)
        dq_ref[...] = (jnp.dot(ds, k, preferred_element_type=F32) * scale).astype(BF16)
        dk_part = lax.dot_general(ds, q, TN, preferred_element_type=F32) * scale
        dv_part = lax.dot_general(p.astype(BF16), dob, TN, preferred_element_type=F32)

        @pl.when(qi == 0)
        def _():
            dk_acc[...] = dk_part
            dv_acc[...] = dv_part

        @pl.when(qi > 0)
        def _():
            dk_acc[...] += dk_part
            dv_acc[...] += dv_part

        @pl.when(qi == nq - 1)
        def _():
            dk_ref[...] = dk_acc[...].astype(BF16)
            dv_ref[...] = dv_acc[...].astype(BF16)

    q_spec = pl.BlockSpec((tq, HEAD_DIM), lambda b, h, qi: (b * nq + qi, h))
    kv_out = pl.BlockSpec((s_len, HEAD_DIM), lambda b, h, qi: (b, h))
    osd = jax.ShapeDtypeStruct((t, nh * HEAD_DIM), BF16)
    return pl.pallas_call(
        body, name=name, grid=(t // s_len, nh, nq),
        in_specs=[q_spec, pl.BlockSpec((s_len, HEAD_DIM), lambda b, h, qi: (b, nh + h)),
                  pl.BlockSpec((s_len, HEAD_DIM), lambda b, h, qi: (b, 2 * nh + h)), q_spec, q_spec],
        out_specs=[q_spec, kv_out, kv_out], out_shape=[osd, osd, osd],
        scratch_shapes=[pltpu.VMEM((s_len, HEAD_DIM), F32), pltpu.VMEM((s_len, HEAD_DIM), F32)],
        compiler_params=_params())(proj, proj, proj, o, do)


def _causal_conv(x, w):
    s_ = x.shape[1]
    xp = jnp.pad(x, ((0, 0), (CONV_WIDTH - 1, 0), (0, 0)))
    return sum(w[i] * xp[:, i:i + s_] for i in range(CONV_WIDTH))


def _gated_delta_rule(q, k, v, g, beta):
    b_, s_, h_, dk = q.shape
    n_ = s_ // CHUNK

    def chunks(t):
        return jnp.swapaxes(t.reshape(b_, n_, CHUNK, h_, *t.shape[3:]), 2, 3)

    q, k, v, g, beta = (chunks(t.astype(F32)) for t in (q, k, v, g, beta))
    gc = jnp.cumsum(g, axis=-1)
    idx = jnp.arange(CHUNK)
    incl = idx[:, None] >= idx[None, :]
    strict = idx[:, None] > idx[None, :]
    decay = jnp.exp(jnp.where(incl, gc[..., :, None] - gc[..., None, :], -jnp.inf))
    kb = k * beta[..., None]
    kk = jnp.einsum('bnhid,bnhjd->bnhij', kb, k)
    a = jnp.where(strict, kk * decay, 0.0) + jnp.eye(CHUNK, dtype=F32)
    rhs = jnp.concatenate([kb * jnp.exp(gc)[..., None], v * beta[..., None]], axis=-1)
    sol = lax.linalg.triangular_solve(a, rhs, left_side=True, lower=True, unit_diagonal=True)
    w_c, u_c = sol[..., :dk], sol[..., dk:]
    qk = jnp.einsum('bnhid,bnhjd->bnhij', q, k) * decay
    q_dec = q * jnp.exp(gc)[..., None]
    k_dec = k * jnp.exp(gc[..., -1:] - gc)[..., None]
    g_last = jnp.exp(gc[..., -1])

    def step(state, xs):
        wc, uc, qkc, qdc, kdc, glc = xs
        v_new = uc - jnp.einsum('bhcd,bhdv->bhcv', wc, state)
        o = jnp.einsum('bhcd,bhdv->bhcv', qdc, state) + jnp.einsum('bhij,bhjv->bhiv', qkc, v_new)
        state = state * glc[..., None, None] + jnp.einsum('bhcd,bhcv->bhdv', kdc, v_new)
        return state, o

    xs = tuple(jnp.moveaxis(t, 1, 0) for t in (w_c, u_c, qk, q_dec, k_dec, g_last))
    state0 = jnp.zeros((b_, h_, dk, v.shape[-1]), F32)
    _, o = lax.scan(step, state0, xs)
    return o.transpose(1, 0, 3, 2, 4).reshape(b_, s_, h_, v.shape[-1])


def _deltanet_core(proj, conv_w, a_log, dt_bias, dn_norm, d_dn):
    b_, s_, _ = proj.shape
    nh_d = d_dn // HEAD_DIM
    o0 = 0
    dqkv = proj[..., o0:o0 + 3 * d_dn]; o0 += 3 * d_dn
    dz = proj[..., o0:o0 + d_dn]; o0 += d_dn
    db = proj[..., o0:o0 + nh_d]; o0 += nh_d
    da = proj[..., o0:o0 + nh_d]
    dqkv = jax.nn.silu(_causal_conv(dqkv, conv_w))
    dq, dk, dv = jnp.split(dqkv, 3, axis=-1)
    heads_b = lambda t: t.reshape(b_, s_, nh_d, HEAD_DIM)
    dq, dk, dv = heads_b(dq), heads_b(dk), heads_b(dv)
    l2 = lambda t: t * lax.rsqrt(jnp.sum(t * t, axis=-1, keepdims=True) + EPS)
    dq = l2(dq) * (HEAD_DIM ** -0.5)
    dk = l2(dk)
    beta = jax.nn.sigmoid(db)
    g = -jnp.exp(a_log) * jax.nn.softplus(da + dt_bias)
    o = _gated_delta_rule(dq, dk, dv, g, beta)
    o = o * lax.rsqrt(jnp.mean(o * o, axis=-1, keepdims=True) + EPS) * dn_norm
    o = o * jax.nn.silu(heads_b(dz))
    return o.reshape(b_, s_, d_dn)


def _w_in_windows(ws):
    w0 = [(ws * k) // LANE * LANE for k in range(N_CHIPS)]
    sh = [ws * k - w0[k] for k in range(N_CHIPS)]
    ww = _ceil_to(max(sh) + ws, 2 * LANE)
    n_tiles = (w0[-1] + ww) // LANE
    tpw = ww // LANE
    tph = tpw // 2
    tab = np.zeros((7, n_tiles), np.int32)
    for t in range(n_tiles):
        ks = [k for k in range(N_CHIPS) if w0[k] // LANE <= t < w0[k] // LANE + tpw]
        k1 = ks[-1]
        lt = t - w0[k1] // LANE
        tab[0, t], tab[1, t], tab[2, t] = lt // tph, k1, lt % tph
        k2 = ks[0] if len(ks) > 1 else k1
        lt2 = t - w0[k2] // LANE
        tab[3, t], tab[4, t], tab[5, t] = lt2 // tph, k2, lt2 % tph
        tab[6, t] = 1 if len(ks) > 1 else 0
        assert len(ks) <= 2
    return w0, sh, ww, n_tiles, tab


def kernel(x, ffn1_norm, ffn1_w_gate, ffn1_w_up, ffn1_w_down, mix_norm, w_in, conv_w, a_log, dt_bias, dn_norm, w_out, ffn2_norm, ffn2_w_gate, ffn2_w_up, ffn2_w_down, final_norm, loss_target, m_ffn1_norm, m_ffn1_w_gate, m_ffn1_w_up, m_ffn1_w_down, m_mix_norm, m_w_in, m_conv_w, m_a_log, m_dt_bias, m_dn_norm, m_w_out, m_ffn2_norm, m_ffn2_w_gate, m_ffn2_w_up, m_ffn2_w_down, m_final_norm, v_ffn1_norm, v_ffn1_w_gate, v_ffn1_w_up, v_ffn1_w_down, v_mix_norm, v_w_in, v_conv_w, v_a_log, v_dt_bias, v_dn_norm, v_w_out, v_ffn2_norm, v_ffn2_w_gate, v_ffn2_w_up, v_ffn2_w_down, v_final_norm):
    bl, s_, d = x.shape
    t = bl * s_
    fs = ffn1_w_gate.shape[1]
    hp = _ceil_to(-(-fs // 2), LANE)
    ws = w_in.shape[1]
    d_mix = w_out.shape[0] * N_CHIPS
    d_attn = d_dn = d_mix // 2
    nh_d = d_dn // HEAD_DIM
    d_in = 3 * d_attn + 4 * d_dn + 2 * nh_d
    cs = conv_w.shape[1]
    assert ws * N_CHIPS == d_in and cs * N_CHIPS == 3 * d_dn

    xi, yi, ci = lax.axis_index("x"), lax.axis_index("y"), lax.axis_index("c")
    me = 2 * xi + yi
    cidx = jnp.reshape(ci, (1,)).astype(jnp.int32)
    meidx = jnp.reshape(me, (1,)).astype(jnp.int32)
    mcidx = jnp.stack([me, ci]).astype(jnp.int32)

    w0, sh, ww, n_tiles, tab = _w_in_windows(ws)
    shift = (ws * me) % LANE
    w_in_win = lax.dynamic_update_slice(jnp.zeros((d, ww), F32), w_in, (jnp.int32(0), shift))
    rows_tr = math.gcd(hp, fs)
    conv_piece = jnp.pad(conv_w, ((0, 8 - CONV_WIDTH), (0, 0))).reshape(8, 2, cs // 2).transpose(1, 0, 2)
    z0 = jnp.int32(0)
    pieces = [
        _cast_split_cols(ffn1_w_gate, hp, meidx, "cast_g1"),
        _cast_split_cols(ffn1_w_up, hp, meidx, "cast_u1"),
        _cast_split_rows(ffn1_w_down, hp, rows_tr, meidx, "cast_d1"),
        _cast_split_cols(w_in_win, ww // 2, meidx, "cast_in"),
        _cast_split_rows(w_out, w_out.shape[0] // 2, w_out.shape[0] // 2, meidx, "cast_out"),
        _cast_split_cols(ffn2_w_gate, hp, meidx, "cast_g2"),
        _cast_split_cols(ffn2_w_up, hp, meidx, "cast_u2"),
        _cast_split_rows(ffn2_w_down, hp, rows_tr, meidx, "cast_d2"),
        lax.dynamic_update_slice(jnp.zeros((2, N_CHIPS, 8, cs // 2), F32), conv_piece[:, None], (z0, me, z0, z0)),
    ]
    gathered = _all_gather(pieces, "all_gather_weights")
    wg1, wu1, wd1, win_all, wout, wg2, wu2, wd2, conv_all = gathered
    wg1, wu1, wg2, wu2 = (a.reshape(8, d, hp) for a in (wg1, wu1, wg2, wu2))
    wd1, wd2 = (a.reshape(8, hp, d) for a in (wd1, wd2))
    wout = wout.reshape(8, w_out.shape[0] // 2, d)
    win_full = _combine_windows(win_all, jnp.asarray(tab), n_tiles, "combine_w_in")
    conv_full = conv_all.transpose(2, 1, 0, 3).reshape(8, 3 * d_dn)[:CONV_WIDTH]
    npc = 8
    ident = lambda p: p
    cat_map = lambda p: 2 * (p % N_CHIPS) + p // N_CHIPS

    h0 = x.reshape(t, d)
    n1 = _rms_fwd(h0, ffn1_norm, "rms1")
    a1, b1, s1 = _ffn_up(n1, wg1, wu1, "ffn1_up")
    h1 = _mm_pieces_resid(s1, wd1, h0, 0.5, ident, "ffn1_down")
    n2 = _rms_fwd(h1, mix_norm, "rms2")
    proj = _mm2d(n2, win_full, NN, F32, "in_proj")
    nh_a = d_attn // HEAD_DIM
    attn = _attn_fwd(proj, nh_a, s_, "attn_fwd")
    core = lambda pr, cw, al, dtb, dnn: _deltanet_core(pr.reshape(bl, s_, -1), cw, al, dtb, dnn, d_dn)
    dn_out, core_vjp = jax.vjp(core, proj[:, 3 * d_attn:], conv_full, a_log, dt_bias, dn_norm)
    cat_b = jnp.concatenate([attn, dn_out.reshape(t, d_dn)], axis=1).astype(BF16)
    h2 = _mm_pieces_resid(cat_b, wout, h1, 1.0, cat_map, "out_proj")
    n3 = _rms_fwd(h2, ffn2_norm, "rms3")
    a3, b3, s3 = _ffn_up(n3, wg2, wu2, "ffn2_up")
    h3 = _mm_pieces_resid(s3, wd2, h2, 0.5, ident, "ffn2_down")

    dh3, dh3b, dwf_p, lsq_p = _final_loss(h3, final_norm, loss_target.reshape(t, d), "final_loss")
    da3, db3 = _ffn_bwd_hidden(dh3b, wd2, a3, b3, "ffn2_bwd_hidden")
    g_wd2 = _grad_rows_pieces(s3, dh3b, 0.5, ident, npc, "ffn2_grad_down")
    g_wg2, g_wu2 = _grad_cols_pieces(n3, da3, db3, npc, "ffn2_grad_up")
    dn3 = _ffn_bwd_input(da3, db3, wg2, wu2, "ffn2_bwd_input")
    dh2, dh2b, dw3_p = _rms_bwd(dn3, h2, ffn2_norm, dh3, "rms3_bwd")

    dcat = _mm_nt_pieces_out(dh2b, wout, cat_map, "out_proj_bwd")
    g_wout = _grad_rows_pieces(cat_b, dh2b, 1.0, cat_map, npc, "out_proj_grad")
    dq_a, dk_a, dv_a = _attn_bwd(proj, attn, dcat, nh_a, s_, "attn_bwd")
    dproj_d, dconv, dalog, ddtb, ddnn = core_vjp(dcat[:, d_attn:].astype(F32).reshape(bl, s_, d_dn))
    dproj_b = jnp.concatenate([dq_a, dk_a, dv_a, dproj_d.astype(BF16)], axis=1)
    g_win_full = _mm2d(n2, dproj_b, TN, BF16, "in_proj_grad")
    dn2 = _mm2d(dproj_b, win_full, NT, F32, "in_proj_bwd")
    dh1, dh1b, dwm_p = _rms_bwd(dn2, h1, mix_norm, dh2, "rms2_bwd")

    da1, db1 = _ffn_bwd_hidden(dh1b, wd1, a1, b1, "ffn1_bwd_hidden")
    g_wd1 = _grad_rows_pieces(s1, dh1b, 0.5, ident, npc, "ffn1_grad_down")
    g_wg1, g_wu1 = _grad_cols_pieces(n1, da1, db1, npc, "ffn1_grad_up")
    dn1 = _ffn_bwd_input(da1, db1, wg1, wu1, "ffn1_bwd_input")
    dh0, _, dw1_p = _rms_bwd(dn1, h0, ffn1_norm, dh1, "rms1_bwd")
    grad_x = dh0.reshape(bl, s_, d)

    wh = ww // 2
    g_win = jnp.stack([jnp.stack([g_win_full[:, w0[k] + wh * h: w0[k] + wh * (h + 1)] for k in range(N_CHIPS)])
                       for h in range(2)])
    big = [g_wg1, g_wu1, g_wd1, g_win, g_wout, g_wg2, g_wu2, g_wd2]
    big = [g.reshape((2, N_CHIPS * g.shape[-2], g.shape[-1])) for g in big]
    from_sib = _sibling_take(big, "rs_sibling_take")
    pair = [_pair_sum(g, l, cidx, f"rs_pair_sum_{i}") for i, (g, l) in enumerate(zip(big, from_sib))]
    pair = [p.reshape(N_CHIPS, p.shape[0] // N_CHIPS, p.shape[1]) for p in pair]
    from_chips = _chip_all_to_all(pair, "rs_chip_all_to_all")
    halves = [_chip_sum(p, l, mcidx, f"rs_chip_sum_{i}") for i, (p, l) in enumerate(zip(pair, from_chips))]
    full = _sibling_join(halves, "rs_sibling_join")
    f_wg1, f_wu1, f_wd1, f_win, f_wout, f_wg2, f_wu2, f_wd2 = full

    unpad_cols = lambda f: jnp.concatenate([f[0], f[1][:, :fs - hp]], axis=1)
    unpad_rows = lambda f: f.reshape(2 * f.shape[1], f.shape[2])[:fs]
    gw = {
        "ffn1_w_gate": unpad_cols(f_wg1), "ffn1_w_up": unpad_cols(f_wu1), "ffn1_w_down": unpad_rows(f_wd1),
        "w_in": lax.dynamic_slice(jnp.concatenate([f_win[0], f_win[1]], axis=1), (jnp.int32(0), shift), (d, ws)),
        "w_out": f_wout.reshape(w_out.shape),
        "ffn2_w_gate": unpad_cols(f_wg2), "ffn2_w_up": unpad_cols(f_wu2), "ffn2_w_down": unpad_rows(f_wd2),
    }

    def lanes(v):
        v = v.reshape(-1)
        return jnp.pad(v, (0, _ceil_to(v.shape[0], LANE) - v.shape[0])).reshape(-1, LANE)

    small = [dw1_p.sum(0), dwm_p.sum(0), dw3_p.sum(0), dwf_p.sum(0), ddnn, dalog, ddtb,
             (0.5 / d) * jnp.sum(lsq_p).reshape(1), dconv]
    rows = [lanes(v) for v in small]
    offs = np.cumsum([0] + [r.shape[0] for r in rows])
    packed = jnp.concatenate(rows, axis=0)
    packed = jnp.pad(packed, ((0, _ceil_to(packed.shape[0], 8) - packed.shape[0]), (0, 0)))
    red = _allreduce_small(packed, "allreduce_small")
    take = lambda i, shape: red[offs[i]:offs[i + 1]].reshape(-1)[:int(np.prod(shape))].reshape(shape)
    gw["ffn1_norm"] = take(0, (d,))
    gw["mix_norm"] = take(1, (d,))
    gw["ffn2_norm"] = take(2, (d,))
    gw["final_norm"] = take(3, (d,))
    gw["dn_norm"] = take(4, dn_norm.shape)
    gw["a_log"] = take(5, a_log.shape)
    gw["dt_bias"] = take(6, dt_bias.shape)
    loss = take(7, (1,)).reshape(())
    gw["conv_w"] = lax.dynamic_slice(take(8, (CONV_WIDTH, 3 * d_dn)), (jnp.int32(0), me * cs), (CONV_WIDTH, cs))

    names = ['ffn1_norm', 'ffn1_w_gate', 'ffn1_w_up', 'ffn1_w_down', 'mix_norm', 'w_in', 'conv_w', 'a_log', 'dt_bias',
             'dn_norm', 'w_out', 'ffn2_norm', 'ffn2_w_gate', 'ffn2_w_up', 'ffn2_w_down', 'final_norm']
    wv = dict(zip(names, (ffn1_norm, ffn1_w_gate, ffn1_w_up, ffn1_w_down, mix_norm, w_in, conv_w, a_log, dt_bias,
                          dn_norm, w_out, ffn2_norm, ffn2_w_gate, ffn2_w_up, ffn2_w_down, final_norm)))
    mv = dict(zip(names, (m_ffn1_norm, m_ffn1_w_gate, m_ffn1_w_up, m_ffn1_w_down, m_mix_norm, m_w_in, m_conv_w, m_a_log,
                          m_dt_bias, m_dn_norm, m_w_out, m_ffn2_norm, m_ffn2_w_gate, m_ffn2_w_up, m_ffn2_w_down,
                          m_final_norm)))
    vv = dict(zip(names, (v_ffn1_norm, v_ffn1_w_gate, v_ffn1_w_up, v_ffn1_w_down, v_mix_norm, v_w_in, v_conv_w, v_a_log,
                          v_dt_bias, v_dn_norm, v_w_out, v_ffn2_norm, v_ffn2_w_gate, v_ffn2_w_up, v_ffn2_w_down,
                          v_final_norm)))
    delta, new_m, new_v = {}, {}, {}
    small_names = [n for n in names if wv[n].ndim == 1 or n == "conv_w"]
    for n in names:
        if n in small_names:
            continue
        delta[n], new_m[n], new_v[n] = _adamw(gw[n], wv[n], mv[n], vv[n], f"adamw_{n}")
    srows = {n: lanes(gw[n]).shape[0] for n in small_names}
    soffs = np.cumsum([0] + [srows[n] for n in small_names])
    stot = _ceil_to(int(soffs[-1]), 8)

    def pack(dct):
        p = jnp.concatenate([lanes(dct[n]) for n in small_names], axis=0)
        return jnp.pad(p, ((0, stot - p.shape[0]), (0, 0)))

    sd, sm, sv = _adamw(pack(gw), pack(wv), pack(mv), pack(vv), "adamw_small")
    for i, n in enumerate(small_names):
        cut = lambda p: p[soffs[i]:soffs[i + 1]].reshape(-1)[:wv[n].size].reshape(wv[n].shape)
        delta[n], new_m[n], new_v[n] = cut(sd), cut(sm), cut(sv)

    return (loss, grad_x, *[gw[n] for n in names], *[delta[n] for n in names], *[new_m[n] for n in names],
            *[new_v[n] for n in names])
```

```python
import functools
import math

import jax
import jax.numpy as jnp
import numpy as np
from jax import lax
from jax.experimental import pallas as pl
from jax.experimental.pallas import tpu as pltpu

F32 = jnp.float32
BF16 = jnp.bfloat16
MESH = pl.DeviceIdType.MESH
ANY = pl.BlockSpec(memory_space=pl.ANY)

LANE = 128
N_CHIPS = 4
N_DEV = 8
EPS = 1e-6
HEAD_DIM = 128
CONV_WIDTH = 4
CHUNK = 64
ATTN_BLOCK = 128
DILATED_CONFIGS = ((128, 1), (512, 4), (2048, 16))
VMEM_LIMIT = 52 * 1024 * 1024

ADAM_LR = 0.001
ADAM_B1 = 0.9
ADAM_B2 = 0.999
ADAM_EPS = 1e-08
ADAM_WD = 0.01
ADAM_STEP = 10

NN = (((1,), (0,)), ((), ()))
NT = (((1,), (1,)), ((), ()))
TN = (((0,), (0,)), ((), ()))


def _ceil_to(v, m):
    return -(-v // m) * m


def _params(vmem=VMEM_LIMIT):
    return pltpu.CompilerParams(vmem_limit_bytes=vmem)


def _gemm(name, grid, pairs, dn, acc_shape, n_acc, extras, outs, epilogue, n_prefetch=0, prefetch=()):
    n_pairs, n_ex, n_out = len(pairs), len(extras), len(outs)
    kax = len(grid) - 1
    nk = grid[kax]

    def body(*refs):
        refs = refs[n_prefetch:]
        ins = refs[: 2 * n_pairs]
        ex = refs[2 * n_pairs: 2 * n_pairs + n_ex]
        out_refs = refs[2 * n_pairs + n_ex: 2 * n_pairs + n_ex + n_out]
        accs = refs[2 * n_pairs + n_ex + n_out:]
        k = pl.program_id(kax)

        @pl.when(k == 0)
        def _():
            for acc in accs:
                acc[...] = jnp.zeros(acc.shape, F32)

        for q in range(n_pairs):
            a = ins[2 * q][...]
            b = ins[2 * q + 1][...]
            if a.dtype != BF16:
                a = a.astype(BF16)
            if b.dtype != BF16:
                b = b.astype(BF16)
            accs[pairs[q][4]][...] += lax.dot_general(a, b, dn, preferred_element_type=F32)

        @pl.when(k == nk - 1)
        def _():
            res = epilogue([acc[...] for acc in accs], [e[...] for e in ex])
            for o, r in zip(out_refs, res):
                o[...] = r.astype(o.dtype)

    in_specs = []
    args = []
    for a, a_spec, b, b_spec, _ in pairs:
        in_specs += [a_spec, b_spec]
        args += [a, b]
    for e, e_spec in extras:
        in_specs.append(e_spec)
        args.append(e)
    out_shape = [o for o, _ in outs]
    out_specs = [s for _, s in outs]
    scratch = [pltpu.VMEM(acc_shape, F32) for _ in range(n_acc)]
    if n_prefetch:
        gs = pltpu.PrefetchScalarGridSpec(num_scalar_prefetch=n_prefetch, grid=grid, in_specs=in_specs,
                                          out_specs=out_specs, scratch_shapes=scratch)
        return pl.pallas_call(body, name=name, grid_spec=gs, out_shape=out_shape,
                              compiler_params=_params())(*prefetch, *args)
    return pl.pallas_call(body, name=name, grid=grid, in_specs=in_specs, out_specs=out_specs,
                          out_shape=out_shape, scratch_shapes=scratch, compiler_params=_params())(*args)


def _pick(n, prefs):
    for p in prefs:
        if n % p == 0:
            return p
    return n


def _sigmoid(v):
    return 1.0 / (1.0 + jnp.exp(-v))


def _ffn_up(n, wg, wu, name):
    t, d = n.shape
    npieces, _, hp = wg.shape
    tm = _pick(t, (512, 256, 128, 64, 32, 16))
    tk = _pick(d, (1024, 512, 256, 128))
    grid = (t // tm, npieces, d // tk)
    a_spec = pl.BlockSpec((tm, tk), lambda i, p, k: (i, k))
    w_spec = pl.BlockSpec((None, tk, hp), lambda i, p, k: (p, k, 0))
    o_spec = pl.BlockSpec((tm, hp), lambda i, p, k: (i, p))
    osd = jax.ShapeDtypeStruct((t, npieces * hp), BF16)

    def epi(accs, ex):
        a, b = accs
        return a, b, a * _sigmoid(a) * b

    return _gemm(name, grid, [(n, a_spec, wg, w_spec, 0), (n, a_spec, wu, w_spec, 1)], NN, (tm, hp), 2, [],
                 [(osd, o_spec)] * 3, epi)


def _mm_pieces_resid(a, w, resid, scale, amap, name):
    t = a.shape[0]
    npieces, kp, n = w.shape
    tm = _pick(t, (512, 256, 128, 64, 32, 16))
    tn = _pick(n, (1024, 512, 256, 128))
    grid = (t // tm, n // tn, npieces)
    a_spec = pl.BlockSpec((tm, kp), lambda i, j, p: (i, amap(p)))
    w_spec = pl.BlockSpec((None, kp, tn), lambda i, j, p: (p, 0, j))
    r_spec = pl.BlockSpec((tm, tn), lambda i, j, p: (i, j))

    def epi(accs, ex):
        return (ex[0] + scale * accs[0],)

    return _gemm(name, grid, [(a, a_spec, w, w_spec, 0)], NN, (tm, tn), 1, [(resid, r_spec)],
                 [(jax.ShapeDtypeStruct((t, n), F32), r_spec)], epi)[0]


def _ffn_bwd_hidden(dh, wd, a, b, name):
    t, d = dh.shape
    npieces, hp, _ = wd.shape
    tm = _pick(t, (512, 256, 128, 64, 32, 16))
    tk = _pick(d, (1024, 512, 256, 128))
    grid = (t // tm, npieces, d // tk)
    a_spec = pl.BlockSpec((tm, tk), lambda i, p, k: (i, k))
    w_spec = pl.BlockSpec((None, hp, tk), lambda i, p, k: (p, 0, k))
    o_spec = pl.BlockSpec((tm, hp), lambda i, p, k: (i, p))
    osd = jax.ShapeDtypeStruct((t, npieces * hp), BF16)

    def epi(accs, ex):
        ds = 0.5 * accs[0]
        av = ex[0].astype(F32)
        bv = ex[1].astype(F32)
        sg = _sigmoid(av)
        da = ds * bv * (sg * (1.0 + av * (1.0 - sg)))
        db = ds * (av * sg)
        return da, db

    return _gemm(name, grid, [(dh, a_spec, wd, w_spec, 0)], NT, (tm, hp), 1, [(a, o_spec), (b, o_spec)],
                 [(osd, o_spec)] * 2, epi)


def _mm_nt_pieces_out(dh, w, omap, name):
    t, d = dh.shape
    npieces, npp, _ = w.shape
    tm = _pick(t, (512, 256, 128, 64, 32, 16))
    tk = _pick(d, (1024, 512, 256, 128))
    grid = (t // tm, npieces, d // tk)
    a_spec = pl.BlockSpec((tm, tk), lambda i, p, k: (i, k))
    w_spec = pl.BlockSpec((None, npp, tk), lambda i, p, k: (p, 0, k))
    o_spec = pl.BlockSpec((tm, npp), lambda i, p, k: (i, omap(p)))
    return _gemm(name, grid, [(dh, a_spec, w, w_spec, 0)], NT, (tm, npp), 1, [],
                 [(jax.ShapeDtypeStruct((t, npieces * npp), BF16), o_spec)], lambda accs, ex: (accs[0],))[0]


def _grad_rows_pieces(x, dy, scale, amap, npieces, name):
    t, n = dy.shape
    mp = x.shape[1] // npieces
    tn = _pick(n, (1024, 512, 256, 128))
    tk = _pick(t, (512, 256, 128, 64, 32, 16))
    grid = (npieces, n // tn, t // tk)
    x_spec = pl.BlockSpec((tk, mp), lambda p, j, k: (k, amap(p)))
    y_spec = pl.BlockSpec((tk, tn), lambda p, j, k: (k, j))
    o_spec = pl.BlockSpec((None, mp, tn), lambda p, j, k: (p, 0, j))
    return _gemm(name, grid, [(x, x_spec, dy, y_spec, 0)], TN, (mp, tn), 1, [],
                 [(jax.ShapeDtypeStruct((npieces, mp, n), BF16), o_spec)], lambda accs, ex: (scale * accs[0],))[0]


def _grad_cols_pieces(n, da, db, npieces, name):
    t, d = n.shape
    hp = da.shape[1] // npieces
    tm = _pick(d, (512, 256, 128))
    tk = _pick(t, (512, 256, 128, 64, 32, 16))
    grid = (npieces, d // tm, t // tk)
    n_spec = pl.BlockSpec((tk, tm), lambda p, i, k: (k, i))
    g_spec = pl.BlockSpec((tk, hp), lambda p, i, k: (k, p))
    o_spec = pl.BlockSpec((None, tm, hp), lambda p, i, k: (p, i, 0))
    osd = jax.ShapeDtypeStruct((npieces, d, hp), BF16)
    return _gemm(name, grid, [(n, n_spec, da, g_spec, 0), (n, n_spec, db, g_spec, 1)], TN, (tm, hp), 2, [],
                 [(osd, o_spec)] * 2, lambda accs, ex: (accs[0], accs[1]))


def _ffn_bwd_input(da, db, wg, wu, name):
    t = da.shape[0]
    npieces, d, hp = wg.shape
    tm = _pick(t, (512, 256, 128, 64, 32, 16))
    tn = _pick(d, (1024, 512, 256, 128))
    grid = (t // tm, d // tn, npieces)
    g_spec = pl.BlockSpec((tm, hp), lambda i, j, p: (i, p))
    w_spec = pl.BlockSpec((None, tn, hp), lambda i, j, p: (p, j, 0))
    o_spec = pl.BlockSpec((tm, tn), lambda i, j, p: (i, j))
    return _gemm(name, grid, [(da, g_spec, wg, w_spec, 0), (db, g_spec, wu, w_spec, 0)], NT, (tm, tn), 1, [],
                 [(jax.ShapeDtypeStruct((t, d), F32), o_spec)], lambda accs, ex: (accs[0],))[0]


def _mm2d(a, b, dn, out_dtype, name):
    if dn == NN:
        m, kk = a.shape
        n = b.shape[1]
    elif dn == NT:
        m, kk = a.shape
        n = b.shape[0]
    else:
        kk, m = a.shape
        n = b.shape[1]
    tm = _pick(m, (512, 256, 128, 64, 32, 16))
    tn = _pick(n, (768, 1024, 512, 256, 128))
    tk = _pick(kk, (768, 1024, 512, 256, 128, 64, 32, 16))
    grid = (m // tm, n // tn, kk // tk)
    if dn == TN:
        a_spec = pl.BlockSpec((tk, tm), lambda i, j, k: (k, i))
    else:
        a_spec = pl.BlockSpec((tm, tk), lambda i, j, k: (i, k))
    if dn == NT:
        b_spec = pl.BlockSpec((tn, tk), lambda i, j, k: (j, k))
    else:
        b_spec = pl.BlockSpec((tk, tn), lambda i, j, k: (k, j))
    o_spec = pl.BlockSpec((tm, tn), lambda i, j, k: (i, j))
    return _gemm(name, grid, [(a, a_spec, b, b_spec, 0)], dn, (tm, tn), 1, [],
                 [(jax.ShapeDtypeStruct((m, n), out_dtype), o_spec)], lambda accs, ex: (accs[0],))[0]


def _row_tile(t):
    return _pick(t, (256, 128, 64, 32, 16, 8))


def _rms_fwd(x, w, name):
    t, d = x.shape
    tm = _row_tile(t)

    def body(x_ref, w_ref, o_ref):
        xv = x_ref[...]
        r = lax.rsqrt(jnp.mean(xv * xv, axis=-1, keepdims=True) + EPS)
        o_ref[...] = (xv * r * w_ref[...]).astype(BF16)

    return pl.pallas_call(
        body, name=name, grid=(t // tm,),
        in_specs=[pl.BlockSpec((tm, d), lambda i: (i, 0)), pl.BlockSpec((1, d), lambda i: (0, 0))],
        out_specs=pl.BlockSpec((tm, d), lambda i: (i, 0)),
        out_shape=jax.ShapeDtypeStruct((t, d), BF16), compiler_params=_params())(x, w.reshape(1, d))


def _rms_bwd(dn, x, w, dres, name):
    t, d = x.shape
    tm = _row_tile(t)

    def body(dn_ref, x_ref, w_ref, r_ref, o_ref, ob_ref, dw_ref):
        i = pl.program_id(0)
        xv = x_ref[...]
        r = lax.rsqrt(jnp.mean(xv * xv, axis=-1, keepdims=True) + EPS)
        xh = xv * r
        dy = dn_ref[...].astype(F32)
        g = dy * w_ref[...]
        dx = r * (g - xh * jnp.mean(g * xh, axis=-1, keepdims=True))
        tot = r_ref[...] + dx
        o_ref[...] = tot
        ob_ref[...] = tot.astype(BF16)
        part = (dy * xh).reshape(tm // 8, 8, d).sum(axis=0)

        @pl.when(i == 0)
        def _():
            dw_ref[...] = part

        @pl.when(i > 0)
        def _():
            dw_ref[...] += part

    row = pl.BlockSpec((tm, d), lambda i: (i, 0))
    return pl.pallas_call(
        body, name=name, grid=(t // tm,),
        in_specs=[row, row, pl.BlockSpec((1, d), lambda i: (0, 0)), row],
        out_specs=[row, row, pl.BlockSpec((8, d), lambda i: (0, 0))],
        out_shape=[jax.ShapeDtypeStruct((t, d), F32), jax.ShapeDtypeStruct((t, d), BF16),
                   jax.ShapeDtypeStruct((8, d), F32)],
        compiler_params=_params())(dn, x, w.reshape(1, d), dres)


def _final_loss(h, w, target, name):
    t, d = h.shape
    tm = _row_tile(t)

    def body(h_ref, w_ref, t_ref, o_ref, ob_ref, dw_ref, ls_ref):
        i = pl.program_id(0)
        xv = h_ref[...]
        r = lax.rsqrt(jnp.mean(xv * xv, axis=-1, keepdims=True) + EPS)
        xh = xv * r
        err = xh * w_ref[...] - t_ref[...]
        dy = err * (1.0 / d)
        g = dy * w_ref[...]
        dx = r * (g - xh * jnp.mean(g * xh, axis=-1, keepdims=True))
        o_ref[...] = dx
        ob_ref[...] = dx.astype(BF16)
        part = (dy * xh).reshape(tm // 8, 8, d).sum(axis=0)
        lpart = (err * err).reshape(tm // 8, 8, d).sum(axis=0)

        @pl.when(i == 0)
        def _():
            dw_ref[...] = part
            ls_ref[...] = lpart

        @pl.when(i > 0)
        def _():
            dw_ref[...] += part
            ls_ref[...] += lpart

    row = pl.BlockSpec((tm, d), lambda i: (i, 0))
    acc = pl.BlockSpec((8, d), lambda i: (0, 0))
    return pl.pallas_call(
        body, name=name, grid=(t // tm,),
        in_specs=[row, pl.BlockSpec((1, d), lambda i: (0, 0)), row],
        out_specs=[row, row, acc, acc],
        out_shape=[jax.ShapeDtypeStruct((t, d), F32), jax.ShapeDtypeStruct((t, d), BF16),
                   jax.ShapeDtypeStruct((8, d), F32), jax.ShapeDtypeStruct((8, d), F32)],
        compiler_params=_params())(h, w.reshape(1, d), target)


def _cast_split_cols(w, hp, me, name):
    r, fs = w.shape
    v1 = fs - hp
    tm = _pick(r, (256, 128, 64, 32, 16))

    def body(me_ref, w_ref, o_ref):
        o_ref[0] = w_ref[:, :hp].astype(BF16)
        if v1 < hp:
            o_ref[1] = jnp.zeros((tm, hp), BF16)
        o_ref[1, :, :v1] = w_ref[:, hp:].astype(BF16)

    gs = pltpu.PrefetchScalarGridSpec(
        num_scalar_prefetch=1, grid=(r // tm,),
        in_specs=[pl.BlockSpec((tm, fs), lambda i, mr: (i, 0))],
        out_specs=pl.BlockSpec((2, None, tm, hp), lambda i, mr: (0, mr[0], i, 0)))
    return pl.pallas_call(body, name=name, grid_spec=gs, out_shape=jax.ShapeDtypeStruct((2, N_CHIPS, r, hp), BF16),
                          compiler_params=_params())(me, w)


def _cast_split_rows(w, hp, tr, me, name):
    fs, c = w.shape
    nvalid = fs // tr
    per = hp // tr

    def body(me_ref, w_ref, o_ref):
        i = pl.program_id(0)

        @pl.when(i < nvalid)
        def _():
            o_ref[...] = w_ref[...].astype(BF16)

        @pl.when(i >= nvalid)
        def _():
            o_ref[...] = jnp.zeros(o_ref.shape, BF16)

    gs = pltpu.PrefetchScalarGridSpec(
        num_scalar_prefetch=1, grid=(2 * per,),
        in_specs=[pl.BlockSpec((tr, c), lambda i, mr: (jnp.minimum(i, nvalid - 1), 0))],
        out_specs=pl.BlockSpec((None, None, tr, c), lambda i, mr: (i // per, mr[0], i % per, 0)))
    return pl.pallas_call(body, name=name, grid_spec=gs, out_shape=jax.ShapeDtypeStruct((2, N_CHIPS, hp, c), BF16),
                          compiler_params=_params())(me, w)


def _combine_windows(wall, tables, n_tiles, name):
    _, _, d, wh = wall.shape
    tpw = wh // LANE

    def body(tab_ref, a_ref, b_ref, o_ref):
        t = pl.program_id(0)
        both = tab_ref[6, t] == 1
        av = a_ref[...]
        bv = b_ref[...]
        o_ref[...] = jnp.where(both, av + bv, av)

    def amap(t, tab):
        return (tab[0, t], tab[1, t], 0, tab[2, t])

    def bmap(t, tab):
        return (tab[3, t], tab[4, t], 0, tab[5, t])

    gs = pltpu.PrefetchScalarGridSpec(
        num_scalar_prefetch=1, grid=(n_tiles,),
        in_specs=[pl.BlockSpec((None, None, d, LANE), amap), pl.BlockSpec((None, None, d, LANE), bmap)],
        out_specs=pl.BlockSpec((d, LANE), lambda t, tab: (0, t)))
    del tpw
    return pl.pallas_call(body, name=name, grid_spec=gs, out_shape=jax.ShapeDtypeStruct((d, n_tiles * LANE), BF16),
                          compiler_params=_params())(tables, wall, wall)


def _coords():
    return lax.axis_index("x"), lax.axis_index("y"), lax.axis_index("c")


def _remote(src, dst, ssem, rsem, dev):
    return pltpu.make_async_remote_copy(src_ref=src, dst_ref=dst, send_sem=ssem, recv_sem=rsem, device_id=dev,
                                        device_id_type=MESH)


def _all_gather(bufs, name):
    n = len(bufs)

    def body(*refs):
        out = refs[n:2 * n]
        ssem, rsem, fssem, frsem = refs[2 * n:]
        x, y, c = _coords()
        me = 2 * x + y
        sib = (x, y, 1 - c)
        chips = [(1 - x, y), (x, 1 - y), (1 - x, 1 - y)]
        started = []
        for i in range(n):
            for j, (px, py) in enumerate(chips):
                mine = out[i].at[c, me]
                cp = _remote(mine, mine, ssem.at[i, j], rsem.at[i, j], (px, py, c))
                cp.start()
                started.append(cp)
        for i in range(n):
            for j, (px, py) in enumerate(chips):
                slot = out[i].at[c, 2 * px + py]
                _remote(slot, slot, ssem.at[i, j], rsem.at[i, j], (px, py, c)).wait_recv()
                cp = _remote(slot, slot, fssem.at[i, j], frsem.at[i, j], sib)
                cp.start()
                started.append(cp)
        for i in range(n):
            for j, (px, py) in enumerate(chips):
                slot = out[i].at[1 - c, 2 * px + py]
                _remote(slot, slot, fssem.at[i, j], frsem.at[i, j], sib).wait_recv()
        for cp in started:
            cp.wait_send()

    out_shape = [jax.ShapeDtypeStruct(b.shape, b.dtype) for b in bufs]
    return pl.pallas_call(
        body, name=name, in_specs=[ANY] * n, out_specs=[ANY] * n, out_shape=out_shape,
        input_output_aliases={i: i for i in range(n)},
        scratch_shapes=[pltpu.SemaphoreType.DMA((n, 3)), pltpu.SemaphoreType.DMA((n, 3)),
                        pltpu.SemaphoreType.DMA((n, 3)), pltpu.SemaphoreType.DMA((n, 3))])(*bufs)


def _sibling_take(gs, name):
    n = len(gs)

    def body(*refs):
        g, out = refs[:n], refs[n:2 * n]
        ssem, rsem = refs[2 * n:]
        x, y, c = _coords()
        sib = (x, y, 1 - c)
        cps = []
        for i in range(n):
            cp = _remote(g[i].at[1 - c], out[i], ssem.at[i], rsem.at[i], sib)
            cp.start()
            cps.append(cp)
        for cp in cps:
            cp.wait()

    out_shape = [jax.ShapeDtypeStruct(s.shape[1:], s.dtype) for s in gs]
    return pl.pallas_call(
        body, name=name, in_specs=[ANY] * n, out_specs=[ANY] * n, out_shape=out_shape,
        scratch_shapes=[pltpu.SemaphoreType.DMA((n,)), pltpu.SemaphoreType.DMA((n,))])(*gs)


def _chip_all_to_all(ps, name):
    n = len(ps)

    def body(*refs):
        p, out = refs[:n], refs[n:2 * n]
        ssem, rsem = refs[2 * n:]
        x, y, c = _coords()
        me = 2 * x + y
        chips = [(1 - x, y), (x, 1 - y), (1 - x, 1 - y)]
        cps = []
        for i in range(n):
            for j, (px, py) in enumerate(chips):
                cp = _remote(p[i].at[2 * px + py], out[i].at[me], ssem.at[i, j], rsem.at[i, j], (px, py, c))
                cp.start()
                cps.append(cp)
        for i in range(n):
            for j, (px, py) in enumerate(chips):
                slot = out[i].at[2 * px + py]
                _remote(slot, slot, ssem.at[i, j], rsem.at[i, j], (px, py, c)).wait_recv()
        for cp in cps:
            cp.wait_send()

    out_shape = [jax.ShapeDtypeStruct(s.shape, s.dtype) for s in ps]
    return pl.pallas_call(
        body, name=name, in_specs=[ANY] * n, out_specs=[ANY] * n, out_shape=out_shape,
        scratch_shapes=[pltpu.SemaphoreType.DMA((n, 3)), pltpu.SemaphoreType.DMA((n, 3))])(*ps)


def _sibling_join(bufs, name):
    n = len(bufs)

    def body(*refs):
        out = refs[n:2 * n]
        ssem, rsem = refs[2 * n:]
        x, y, c = _coords()
        sib = (x, y, 1 - c)
        cps = []
        for i in range(n):
            mine = out[i].at[c]
            cp = _remote(mine, mine, ssem.at[i], rsem.at[i], sib)
            cp.start()
            cps.append(cp)
        for i in range(n):
            slot = out[i].at[1 - c]
            _remote(slot, slot, ssem.at[i], rsem.at[i], sib).wait_recv()
        for cp in cps:
            cp.wait_send()

    out_shape = [jax.ShapeDtypeStruct(b.shape, b.dtype) for b in bufs]
    return pl.pallas_call(
        body, name=name, in_specs=[ANY] * n, out_specs=[ANY] * n, out_shape=out_shape,
        input_output_aliases={i: i for i in range(n)},
        scratch_shapes=[pltpu.SemaphoreType.DMA((n,)), pltpu.SemaphoreType.DMA((n,))])(*bufs)


def _allreduce_small(vec, name):
    r = vec.shape[0]

    def body(v_ref, o_ref, buf, ssem, rsem):
        x, y, c = _coords()
        my = 4 * x + 2 * y + c
        buf[my] = v_ref[...]
        cps = []
        for dd in range(1, N_DEV):
            px = 1 - x if (dd >> 2) & 1 else x
            py = 1 - y if (dd >> 1) & 1 else y
            pc = 1 - c if dd & 1 else c
            cp = _remote(v_ref, buf.at[my], ssem.at[dd - 1], rsem.at[dd - 1], (px, py, pc))
            cp.start()
            cps.append(cp)
        for dd in range(1, N_DEV):
            px = 1 - x if (dd >> 2) & 1 else x
            py = 1 - y if (dd >> 1) & 1 else y
            pc = 1 - c if dd & 1 else c
            slot = buf.at[4 * px + 2 * py + pc]
            _remote(slot, slot, ssem.at[dd - 1], rsem.at[dd - 1], (px, py, pc)).wait_recv()
        tot = buf[0]
        for k in range(1, N_DEV):
            tot = tot + buf[k]
        o_ref[...] = tot
        for cp in cps:
            cp.wait_send()

    vm = pl.BlockSpec(memory_space=pltpu.VMEM)
    return pl.pallas_call(
        body, name=name, in_specs=[vm], out_specs=vm, out_shape=jax.ShapeDtypeStruct((r, LANE), F32),
        scratch_shapes=[pltpu.VMEM((N_DEV, r, LANE), F32), pltpu.SemaphoreType.DMA((N_DEV - 1,)),
                        pltpu.SemaphoreType.DMA((N_DEV - 1,))])(vec)


def _pair_sum(g, l1, cidx, name):
    _, r, c = g.shape
    tr = _pick(r, (512, 256, 128, 64, 32, 16))

    def body(c_ref, g_ref, l_ref, o_ref):
        o_ref[...] = (g_ref[...].astype(F32) + l_ref[...].astype(F32)).astype(BF16)

    gs = pltpu.PrefetchScalarGridSpec(
        num_scalar_prefetch=1, grid=(r // tr,),
        in_specs=[pl.BlockSpec((None, tr, c), lambda i, cr: (cr[0], i, 0)), pl.BlockSpec((tr, c), lambda i, cr: (i, 0))],
        out_specs=pl.BlockSpec((tr, c), lambda i, cr: (i, 0)))
    return pl.pallas_call(body, name=name, grid_spec=gs, out_shape=jax.ShapeDtypeStruct((r, c), BF16),
                          compiler_params=_params())(cidx, g, l1)


def _chip_sum(p, l2, mc, name):
    _, r, c = l2.shape
    tr = _pick(r, (256, 128, 64, 32, 16))

    def body(mc_ref, p_ref, l0, l1, l2_, l3, o_ref):
        me = mc_ref[0]
        pv = p_ref[...].astype(F32)
        tot = None
        for k, lr in enumerate((l0, l1, l2_, l3)):
            term = jnp.where(me == k, pv, lr[...].astype(F32))
            tot = term if tot is None else tot + term
        o_ref[...] = tot

    def other(k):
        return lambda i, mr: (jnp.where(mr[0] == k, (k + 1) % N_CHIPS, k), i, 0)

    gs = pltpu.PrefetchScalarGridSpec(
        num_scalar_prefetch=1, grid=(r // tr,),
        in_specs=[pl.BlockSpec((None, tr, c), lambda i, mr: (mr[0], i, 0))]
        + [pl.BlockSpec((None, tr, c), other(k)) for k in range(N_CHIPS)],
        out_specs=pl.BlockSpec((None, tr, c), lambda i, mr: (mr[1], i, 0)))
    return pl.pallas_call(body, name=name, grid_spec=gs, out_shape=jax.ShapeDtypeStruct((2, r, c), F32),
                          compiler_params=_params())(mc, p, l2, l2, l2, l2)


def _adamw(g, w, m, v, name):
    r, c = w.shape
    tr = r
    if r * c * 4 > (2 << 20):
        tr = next(p for p in (256, 128, 64, 32, 16, 8) if r % p == 0 and (p * c * 4 <= (2 << 20) or p == 8))

    def body(g_ref, w_ref, m_ref, v_ref, d_ref, nm_ref, nv_ref):
        gv = g_ref[...]
        mn = ADAM_B1 * m_ref[...] + (1.0 - ADAM_B1) * gv
        vn = ADAM_B2 * v_ref[...] + (1.0 - ADAM_B2) * (gv * gv)
        m_hat = mn / (1.0 - ADAM_B1 ** ADAM_STEP)
        v_hat = vn / (1.0 - ADAM_B2 ** ADAM_STEP)
        d_ref[...] = -ADAM_LR * (m_hat / (jnp.sqrt(v_hat) + ADAM_EPS) + ADAM_WD * w_ref[...])
        nm_ref[...] = mn
        nv_ref[...] = vn

    blk = pl.BlockSpec((tr, c), lambda i: (i, 0))
    osd = jax.ShapeDtypeStruct((r, c), F32)
    return pl.pallas_call(body, name=name, grid=(r // tr,), in_specs=[blk] * 4, out_specs=[blk] * 3,
                          out_shape=[osd] * 3, compiler_params=_params())(g, w, m, v)


def _attn_probs(q, k, q0, s_len):
    tq = q.shape[0]
    sc = lax.dot_general(q, k, NT, preferred_element_type=F32) * (HEAD_DIM ** -0.5)
    dlt = (q0 + lax.broadcasted_iota(jnp.int32, (tq, s_len), 0)) - lax.broadcasted_iota(jnp.int32, (tq, s_len), 1)
    cnt = jnp.zeros((tq, s_len), F32)
    for window, dil in DILATED_CONFIGS:
        seen = (dlt >= 0) & (dlt <= window) & ((dlt & (dil - 1)) == 0)
        cnt = cnt + jnp.where(seen, 1.0, 0.0)
    live = cnt > 0.0
    m = jnp.max(jnp.where(live, sc, -jnp.inf), axis=-1, keepdims=True)
    p = cnt * jnp.exp(jnp.where(live, sc - m, -jnp.inf))
    return p / jnp.sum(p, axis=-1, keepdims=True)


def _attn_fwd(proj, nh, s_len, name):
    t = proj.shape[0]
    tq = min(256, s_len)
    nq = s_len // tq

    def body(q_ref, k_ref, v_ref, o_ref):
        qi = pl.program_id(2)
        p = _attn_probs(q_ref[...].astype(BF16), k_ref[...].astype(BF16), qi * tq, s_len)
        o_ref[...] = jnp.dot(p.astype(BF16), v_ref[...].astype(BF16), preferred_element_type=F32)

    q_spec = pl.BlockSpec((tq, HEAD_DIM), lambda b, h, qi: (b * nq + qi, h))
    return pl.pallas_call(
        body, name=name, grid=(t // s_len, nh, nq),
        in_specs=[q_spec, pl.BlockSpec((s_len, HEAD_DIM), lambda b, h, qi: (b, nh + h)),
                  pl.BlockSpec((s_len, HEAD_DIM), lambda b, h, qi: (b, 2 * nh + h))],
        out_specs=q_spec, out_shape=jax.ShapeDtypeStruct((t, nh * HEAD_DIM), F32),
        compiler_params=_params())(proj, proj, proj)


def _attn_bwd(proj, o, do, nh, s_len, name):
    t = proj.shape[0]
    tq = min(256, s_len)
    nq = s_len // tq
    scale = HEAD_DIM ** -0.5

    def body(q_ref, k_ref, v_ref, o_ref, do_ref, dq_ref, dk_ref, dv_ref, dk_acc, dv_acc):
        qi = pl.program_id(2)
        q = q_ref[...].astype(BF16)
        k = k_ref[...].astype(BF16)
        p = _attn_probs(q, k, qi * tq, s_len)
        dob = do_ref[...]
        dp = lax.dot_general(dob, v_ref[...].astype(BF16), NT, preferred_element_type=F32)
        delta = jnp.sum(dob.astype(F32) * o_ref[...], axis=-1, keepdims=True)
        ds = (p * (dp - delta)).astype(BF16)
        dq_ref[...] = (jnp.dot(ds, k, preferred_element_type=F32) * scale).astype(BF16)
        dk_part = lax.dot_general(ds, q, TN, preferred_element_type=F32) * scale
        dv_part = lax.dot_general(p.astype(BF16), dob, TN, preferred_element_type=F32)

        @pl.when(qi == 0)
        def _():
            dk_acc[...] = dk_part
            dv_acc[...] = dv_part

        @pl.when(qi > 0)
        def _():
            dk_acc[...] += dk_part
            dv_acc[...] += dv_part

        @pl.when(qi == nq - 1)
        def _():
            dk_ref[...] = dk_acc[...].astype(BF16)
            dv_ref[...] = dv_acc[...].astype(BF16)

    q_spec = pl.BlockSpec((tq, HEAD_DIM), lambda b, h, qi: (b * nq + qi, h))
    kv_out = pl.BlockSpec((s_len, HEAD_DIM), lambda b, h, qi: (b, h))
    osd = jax.ShapeDtypeStruct((t, nh * HEAD_DIM), BF16)
    return pl.pallas_call(
        body, name=name, grid=(t // s_len, nh, nq),
        in_specs=[q_spec, pl.BlockSpec((s_len, HEAD_DIM), lambda b, h, qi: (b, nh + h)),
                  pl.BlockSpec((s_len, HEAD_DIM), lambda b, h, qi: (b, 2 * nh + h)), q_spec, q_spec],
        out_specs=[q_spec, kv_out, kv_out], out_shape=[osd, osd, osd],
        scratch_shapes=[pltpu.VMEM((s_len, HEAD_DIM), F32), pltpu.VMEM((s_len, HEAD_DIM), F32)],
        compiler_params=_params())(proj, proj, proj, o, do)


HI = lax.Precision.HIGHEST


def _conv_taps(x, w_ref, s_len):
    row = lax.broadcasted_iota(jnp.int32, x.shape, 0)
    c = w_ref[CONV_WIDTH - 1:CONV_WIDTH, :] * x
    for j in range(1, CONV_WIDTH):
        xs = jnp.where(row >= j, pltpu.roll(x, j, 0), 0.0)
        c = c + w_ref[CONV_WIDTH - 1 - j:CONV_WIDTH - j, :] * xs
    return c


def _conv_fwd(proj, conv8, col0, width, s_len, name):
    t = proj.shape[0]
    cb = _pick(width, (512, 256, 128))
    c0 = col0 // cb

    def body(x_ref, w_ref, o_ref):
        c = _conv_taps(x_ref[...], w_ref, s_len)
        o_ref[...] = c * _sigmoid(c)

    return pl.pallas_call(
        body, name=name, grid=(t // s_len, width // cb),
        in_specs=[pl.BlockSpec((s_len, cb), lambda b, j: (b, c0 + j)), pl.BlockSpec((8, cb), lambda b, j: (0, j))],
        out_specs=pl.BlockSpec((s_len, cb), lambda b, j: (b, j)),
        out_shape=jax.ShapeDtypeStruct((t, width), F32), compiler_params=_params())(proj, conv8)


def _conv_bwd(proj, conv8, du, col0, width, s_len, name):
    t = proj.shape[0]
    cb = _pick(width, (512, 256, 128))
    c0 = col0 // cb

    def body(x_ref, w_ref, du_ref, dx_ref, dw_ref):
        b = pl.program_id(1)
        x = x_ref[...]
        c = _conv_taps(x, w_ref, s_len)
        sg = _sigmoid(c)
        dc = du_ref[...] * (sg * (1.0 + c * (1.0 - sg)))
        row = lax.broadcasted_iota(jnp.int32, x.shape, 0)
        dx = w_ref[CONV_WIDTH - 1:CONV_WIDTH, :] * dc
        rows = [jnp.sum(dc * x, axis=0, keepdims=True)]
        for j in range(1, CONV_WIDTH):
            up = jnp.where(row < s_len - j, pltpu.roll(dc, s_len - j, 0), 0.0)
            dx = dx + w_ref[CONV_WIDTH - 1 - j:CONV_WIDTH - j, :] * up
            xs = jnp.where(row >= j, pltpu.roll(x, j, 0), 0.0)
            rows.append(jnp.sum(dc * xs, axis=0, keepdims=True))
        dx_ref[...] = dx.astype(BF16)
        part = jnp.concatenate(rows[::-1] + [jnp.zeros((8 - CONV_WIDTH, cb), F32)], axis=0)

        @pl.when(b == 0)
        def _():
            dw_ref[...] = part

        @pl.when(b > 0)
        def _():
            dw_ref[...] += part

    return pl.pallas_call(
        body, name=name, grid=(width // cb, t // s_len),
        in_specs=[pl.BlockSpec((s_len, cb), lambda j, b: (b, c0 + j)), pl.BlockSpec((8, cb), lambda j, b: (0, j)),
                  pl.BlockSpec((s_len, cb), lambda j, b: (b, j))],
        out_specs=[pl.BlockSpec((s_len, cb), lambda j, b: (b, j)), pl.BlockSpec((8, cb), lambda j, b: (0, j))],
        out_shape=[jax.ShapeDtypeStruct((t, width), BF16), jax.ShapeDtypeStruct((8, width), F32)],
        compiler_params=_params())(proj, conv8, du)


def _bmm(a, b, prec=None):
    return lax.dot_general(a, b, (((2,), (1,)), ((0,), (0,))), precision=prec, preferred_element_type=F32)


def _bmm_nt(a, b):
    return lax.dot_general(a.astype(BF16), b.astype(BF16), (((2,), (2,)), ((0,), (0,))), preferred_element_type=F32)


def _unit_lower_inverse(nm):
    c = nm.shape[-1]
    eye = (lax.broadcasted_iota(jnp.int32, (c, c), 0) == lax.broadcasted_iota(jnp.int32, (c, c), 1)).astype(F32)
    x = -nm
    inv = eye[None] + x
    p = x
    for _ in range(int(math.log2(c)) - 1):
        p = _bmm(p, p, HI)
        inv = inv + _bmm(inv, p, HI)
    return inv


def _dn_chunk_terms(uq, uk, uv, a_col, b_col, alog, dtb):
    n, c, dh = uq.shape
    q = uq * lax.rsqrt(jnp.sum(uq * uq, axis=-1, keepdims=True) + EPS) * (HEAD_DIM ** -0.5)
    k = uk * lax.rsqrt(jnp.sum(uk * uk, axis=-1, keepdims=True) + EPS)
    beta = _sigmoid(b_col)
    xa = a_col + dtb
    g = -jnp.exp(alog) * (jnp.maximum(xa, 0.0) + jnp.log(1.0 + jnp.exp(-jnp.abs(xa))))
    ri = lax.broadcasted_iota(jnp.int32, (c, c), 0)
    ci = lax.broadcasted_iota(jnp.int32, (c, c), 1)
    incl = ri >= ci
    strict = ri > ci
    l_incl = jnp.broadcast_to(incl.astype(F32)[None], (n, c, c))
    gb = jnp.broadcast_to(g, (n, c, dh))
    gc = _bmm(l_incl, gb, HI)
    gdiff = _bmm(l_incl, jnp.broadcast_to(g, (n, c, c)) * strict.astype(F32)[None], HI)
    decay = jnp.where(incl[None], jnp.exp(jnp.where(incl[None], gdiff, 0.0)), 0.0)
    gtot = _bmm(jnp.ones((n, c, c), F32), gb, HI)
    kb = k * beta
    nm = jnp.where(strict[None], _bmm_nt(kb, k) * decay, 0.0)
    tinv = _unit_lower_inverse(nm)
    w = _bmm(tinv, kb * jnp.exp(gc), HI)
    u = _bmm(tinv, uv * beta, HI)
    qk = _bmm_nt(q, k) * decay
    q_dec = q * jnp.exp(gc)
    k_dec = k * jnp.exp(gtot - gc)
    g_last = jnp.exp(_bmm(jnp.ones((n, dh, c), F32), gb, HI))
    return w, u, qk, q_dec, k_dec, g_last


DN_SUB = 8


def _dn_gather_inputs(uq_ref, uk_ref, uv_ref, ba_ref, prm_ref, h, nh, rows, nb):
    ba = ba_ref[rows, :]
    lane = lax.broadcasted_iota(jnp.int32, ba.shape, 1)
    b_col = jnp.sum(jnp.where(lane == h, ba, 0.0), axis=-1, keepdims=True).reshape(nb, CHUNK, 1)
    a_col = jnp.sum(jnp.where(lane == nh + h, ba, 0.0), axis=-1, keepdims=True).reshape(nb, CHUNK, 1)
    lane1 = lax.broadcasted_iota(jnp.int32, (1, LANE), 1)
    alog = jnp.sum(jnp.where(lane1 == h, prm_ref[1:2, :], 0.0), axis=-1, keepdims=True)
    dtb = jnp.sum(jnp.where(lane1 == h, prm_ref[2:3, :], 0.0), axis=-1, keepdims=True)
    shp = (nb, CHUNK, HEAD_DIM)
    return (uq_ref[rows, :].reshape(shp), uk_ref[rows, :].reshape(shp), uv_ref[rows, :].reshape(shp),
            a_col, b_col, alog, dtb)


def _dn_fill_terms(in_refs, h, nh, n, term_refs):
    nb = min(DN_SUB, n)

    def sub(i, carry):
        rows = pl.ds(pl.multiple_of(i * (nb * CHUNK), nb * CHUNK), nb * CHUNK)
        terms = _dn_chunk_terms(*_dn_gather_inputs(*in_refs, h, nh, rows, nb))
        for r, v in zip(term_refs, terms):
            r[pl.ds(i * nb, nb)] = v
        return carry

    lax.fori_loop(0, n // nb, sub, 0)


def _dn_scan(terms_refs, o_ref, st_ref, n):
    w_ref, u_ref, qk_ref, qd_ref, kd_ref, gl_ref = terms_refs

    def step(i, state):
        if st_ref is not None:
            st_ref[i] = state
        sb = state.astype(BF16)
        v_new = u_ref[i] - jnp.dot(w_ref[i].astype(BF16), sb, preferred_element_type=F32)
        vb = v_new.astype(BF16)
        o_ref[i] = (jnp.dot(qd_ref[i].astype(BF16), sb, preferred_element_type=F32)
                    + jnp.dot(qk_ref[i].astype(BF16), vb, preferred_element_type=F32))
        return state * gl_ref[i] + lax.dot_general(kd_ref[i].astype(BF16), vb, TN, preferred_element_type=F32)

    lax.fori_loop(0, n, step, jnp.zeros((HEAD_DIM, HEAD_DIM), F32))


def _dn_specs(nh, nh_a, s_len, zc0, bac):
    head = lambda off: pl.BlockSpec((s_len, HEAD_DIM), lambda b, h: (b, off + h))
    return dict(uq=head(0), uk=head(nh), uv=head(2 * nh), z=head(zc0),
                ba=pl.BlockSpec((s_len, LANE), lambda b, h: (b, bac)),
                prm=pl.BlockSpec((8, LANE), lambda b, h: (0, 0)), dout=head(nh_a), out=head(0))


def _dn_scratch(n, with_states):
    big = pltpu.VMEM((n, CHUNK, HEAD_DIM), F32)
    sc = [big, big, pltpu.VMEM((n, CHUNK, CHUNK), F32), big, big, pltpu.VMEM((n, HEAD_DIM, HEAD_DIM), F32), big]
    if with_states:
        sc.append(pltpu.VMEM((n, HEAD_DIM, HEAD_DIM), F32))
    return sc


def _dn_fwd(u, proj, prm, nh, s_len, zc0, bac, name):
    t = u.shape[0]
    n = s_len // CHUNK
    sp = _dn_specs(nh, 0, s_len, zc0, bac)

    def body(uq_ref, uk_ref, uv_ref, z_ref, ba_ref, prm_ref, o_ref, *scr):
        h = pl.program_id(1)
        _dn_fill_terms((uq_ref, uk_ref, uv_ref, ba_ref, prm_ref), h, nh, n, scr[:6])
        _dn_scan(scr[:6], scr[6], None, n)
        o = scr[6][...].reshape(s_len, HEAD_DIM)
        z = z_ref[...]
        r = lax.rsqrt(jnp.mean(o * o, axis=-1, keepdims=True) + EPS)
        o_ref[...] = o * r * prm_ref[0:1, :] * (z * _sigmoid(z))

    return pl.pallas_call(
        body, name=name, grid=(t // s_len, nh),
        in_specs=[sp["uq"], sp["uk"], sp["uv"], sp["z"], sp["ba"], sp["prm"]], out_specs=sp["out"],
        out_shape=jax.ShapeDtypeStruct((t, nh * HEAD_DIM), F32), scratch_shapes=_dn_scratch(n, False),
        compiler_params=_params())(u, u, u, proj, proj, prm)


def _dn_bwd(u, proj, prm, dcat, nh, nh_a, s_len, zc0, bac, name):
    t = u.shape[0]
    n = s_len // CHUNK
    sp = _dn_specs(nh, nh_a, s_len, zc0, bac)

    def body(uq_ref, uk_ref, uv_ref, z_ref, ba_ref, prm_ref, do_ref,
             duq_ref, duk_ref, duv_ref, dz_ref, dba_ref, dprm_ref, *scr):
        b, h = pl.program_id(0), pl.program_id(1)
        in_refs = (uq_ref, uk_ref, uv_ref, ba_ref, prm_ref)
        w_ref, u_ref, qk_ref, qd_ref, kd_ref, gl_ref, o_scr, st_ref = scr
        _dn_fill_terms(in_refs, h, nh, n, scr[:6])
        _dn_scan(scr[:6], o_scr, st_ref, n)

        o = o_scr[...].reshape(s_len, HEAD_DIM)
        z = z_ref[...]
        dout = do_ref[...].astype(F32)
        gain = prm_ref[0:1, :]
        sg = _sigmoid(z)
        sz = z * sg
        r = lax.rsqrt(jnp.mean(o * o, axis=-1, keepdims=True) + EPS)
        oh = o * r
        dgain = jnp.sum(dout * oh * sz, axis=0, keepdims=True)
        dz_ref[...] = (dout * oh * gain * (sg * (1.0 + z * (1.0 - sg)))).astype(BF16)
        doh = dout * gain * sz
        d_o = r * (doh - oh * jnp.mean(doh * oh, axis=-1, keepdims=True))
        o_scr[...] = d_o.reshape(n, CHUNK, HEAD_DIM)

        def step(j, ds):
            i = n - 1 - j
            st = st_ref[i]
            sb = st.astype(BF16)
            wi, qki, qdi, kdi, gli = w_ref[i], qk_ref[i], qd_ref[i], kd_ref[i], gl_ref[i]
            v_new = u_ref[i] - jnp.dot(wi.astype(BF16), sb, preferred_element_type=F32)
            vb = v_new.astype(BF16)
            don = o_scr[i].astype(BF16)
            dsb = ds.astype(BF16)
            dv = (lax.dot_general(qki.astype(BF16), don, TN, preferred_element_type=F32)
                  + jnp.dot(kdi.astype(BF16), dsb, preferred_element_type=F32))
            dvb = dv.astype(BF16)
            qd_ref[i] = lax.dot_general(don, sb, NT, preferred_element_type=F32)
            qk_ref[i] = lax.dot_general(don, vb, NT, preferred_element_type=F32)
            kd_ref[i] = lax.dot_general(vb, dsb, NT, preferred_element_type=F32)
            gl_ref[i] = ds * st
            u_ref[i] = dv
            w_ref[i] = -lax.dot_general(dvb, sb, NT, preferred_element_type=F32)
            return (ds * gli + lax.dot_general(qdi.astype(BF16), don, TN, preferred_element_type=F32)
                    - lax.dot_general(wi.astype(BF16), dvb, TN, preferred_element_type=F32))

        lax.fori_loop(0, n, step, jnp.zeros((HEAD_DIM, HEAD_DIM), F32))

        @pl.when(h == 0)
        def _():
            dba_ref[...] = jnp.zeros(dba_ref.shape, F32)

        nb = min(DN_SUB, n)

        def sub(i, carry):
            rows = pl.ds(pl.multiple_of(i * (nb * CHUNK), nb * CHUNK), nb * CHUNK)
            _, pull = jax.vjp(_dn_chunk_terms, *_dn_gather_inputs(*in_refs, h, nh, rows, nb))
            duq, duk, duv, da_col, db_col, dal, ddt = pull(tuple(r[pl.ds(i * nb, nb)] for r in scr[:6]))
            duq_ref[rows, :] = duq.reshape(nb * CHUNK, HEAD_DIM)
            duk_ref[rows, :] = duk.reshape(nb * CHUNK, HEAD_DIM)
            duv_ref[rows, :] = duv.reshape(nb * CHUNK, HEAD_DIM)
            lane = lax.broadcasted_iota(jnp.int32, (nb * CHUNK, LANE), 1)
            dba_ref[rows, :] += (jnp.where(lane == h, db_col.reshape(nb * CHUNK, 1), 0.0)
                                 + jnp.where(lane == nh + h, da_col.reshape(nb * CHUNK, 1), 0.0))
            return carry[0] + dal, carry[1] + ddt

        dalog, ddtb = lax.fori_loop(0, n // nb, sub, (jnp.zeros((1, 1), F32), jnp.zeros((1, 1), F32)))
        lane1 = lax.broadcasted_iota(jnp.int32, (1, LANE), 1)
        dprm = jnp.concatenate([dgain, jnp.where(lane1 == h, dalog, 0.0), jnp.where(lane1 == h, ddtb, 0.0),
                                jnp.zeros((5, LANE), F32)], axis=0)

        @pl.when((b == 0) & (h == 0))
        def _():
            dprm_ref[...] = dprm

        @pl.when((b > 0) | (h > 0))
        def _():
            dprm_ref[...] += dprm

    osd = jax.ShapeDtypeStruct((t, nh * HEAD_DIM), F32)
    return pl.pallas_call(
        body, name=name, grid=(t // s_len, nh),
        in_specs=[sp["uq"], sp["uk"], sp["uv"], sp["z"], sp["ba"], sp["prm"], sp["dout"]],
        out_specs=[sp["out"], sp["out"], sp["out"], sp["out"], pl.BlockSpec((s_len, LANE), lambda b, h: (b, 0)),
                   pl.BlockSpec((8, LANE), lambda b, h: (0, 0))],
        out_shape=[osd, osd, osd, jax.ShapeDtypeStruct((t, nh * HEAD_DIM), BF16),
                   jax.ShapeDtypeStruct((t, LANE), F32), jax.ShapeDtypeStruct((8, LANE), F32)],
        scratch_shapes=_dn_scratch(n, True), compiler_params=_params())(u, u, u, proj, proj, prm, dcat)


def _w_in_windows(ws):
    w0 = [(ws * k) // LANE * LANE for k in range(N_CHIPS)]
    sh = [ws * k - w0[k] for k in range(N_CHIPS)]
    ww = _ceil_to(max(sh) + ws, 2 * LANE)
    n_tiles = (w0[-1] + ww) // LANE
    tpw = ww // LANE
    tph = tpw // 2
    tab = np.zeros((7, n_tiles), np.int32)
    for t in range(n_tiles):
        ks = [k for k in range(N_CHIPS) if w0[k] // LANE <= t < w0[k] // LANE + tpw]
        k1 = ks[-1]
        lt = t - w0[k1] // LANE
        tab[0, t], tab[1, t], tab[2, t] = lt // tph, k1, lt % tph
        k2 = ks[0] if len(ks) > 1 else k1
        lt2 = t - w0[k2] // LANE
        tab[3, t], tab[4, t], tab[5, t] = lt2 // tph, k2, lt2 % tph
        tab[6, t] = 1 if len(ks) > 1 else 0
        assert len(ks) <= 2
    return w0, sh, ww, n_tiles, tab


def kernel(x, ffn1_norm, ffn1_w_gate, ffn1_w_up, ffn1_w_down, mix_norm, w_in, conv_w, a_log, dt_bias, dn_norm, w_out, ffn2_norm, ffn2_w_gate, ffn2_w_up, ffn2_w_down, final_norm, loss_target, m_ffn1_norm, m_ffn1_w_gate, m_ffn1_w_up, m_ffn1_w_down, m_mix_norm, m_w_in, m_conv_w, m_a_log, m_dt_bias, m_dn_norm, m_w_out, m_ffn2_norm, m_ffn2_w_gate, m_ffn2_w_up, m_ffn2_w_down, m_final_norm, v_ffn1_norm, v_ffn1_w_gate, v_ffn1_w_up, v_ffn1_w_down, v_mix_norm, v_w_in, v_conv_w, v_a_log, v_dt_bias, v_dn_norm, v_w_out, v_ffn2_norm, v_ffn2_w_gate, v_ffn2_w_up, v_ffn2_w_down, v_final_norm):
    bl, s_, d = x.shape
    t = bl * s_
    fs = ffn1_w_gate.shape[1]
    hp = _ceil_to(-(-fs // 2), LANE)
    ws = w_in.shape[1]
    d_mix = w_out.shape[0] * N_CHIPS
    d_attn = d_dn = d_mix // 2
    nh_d = d_dn // HEAD_DIM
    d_in = 3 * d_attn + 4 * d_dn + 2 * nh_d
    cs = conv_w.shape[1]
    assert ws * N_CHIPS == d_in and cs * N_CHIPS == 3 * d_dn

    xi, yi, ci = lax.axis_index("x"), lax.axis_index("y"), lax.axis_index("c")
    me = 2 * xi + yi
    cidx = jnp.reshape(ci, (1,)).astype(jnp.int32)
    meidx = jnp.reshape(me, (1,)).astype(jnp.int32)
    mcidx = jnp.stack([me, ci]).astype(jnp.int32)

    w0, sh, ww, n_tiles, tab = _w_in_windows(ws)
    shift = (ws * me) % LANE
    w_in_win = lax.dynamic_update_slice(jnp.zeros((d, ww), F32), w_in, (jnp.int32(0), shift))
    rows_tr = math.gcd(hp, fs)
    conv_piece = jnp.pad(conv_w, ((0, 8 - CONV_WIDTH), (0, 0))).reshape(8, 2, cs // 2).transpose(1, 0, 2)
    z0 = jnp.int32(0)
    pieces = [
        _cast_split_cols(ffn1_w_gate, hp, meidx, "cast_g1"),
        _cast_split_cols(ffn1_w_up, hp, meidx, "cast_u1"),
        _cast_split_rows(ffn1_w_down, hp, rows_tr, meidx, "cast_d1"),
        _cast_split_cols(w_in_win, ww // 2, meidx, "cast_in"),
        _cast_split_rows(w_out, w_out.shape[0] // 2, w_out.shape[0] // 2, meidx, "cast_out"),
        _cast_split_cols(ffn2_w_gate, hp, meidx, "cast_g2"),
        _cast_split_cols(ffn2_w_up, hp, meidx, "cast_u2"),
        _cast_split_rows(ffn2_w_down, hp, rows_tr, meidx, "cast_d2"),
        lax.dynamic_update_slice(jnp.zeros((2, N_CHIPS, 8, cs // 2), F32), conv_piece[:, None], (z0, me, z0, z0)),
    ]
    gathered = _all_gather(pieces, "all_gather_weights")
    wg1, wu1, wd1, win_all, wout, wg2, wu2, wd2, conv_all = gathered
    wg1, wu1, wg2, wu2 = (a.reshape(8, d, hp) for a in (wg1, wu1, wg2, wu2))
    wd1, wd2 = (a.reshape(8, hp, d) for a in (wd1, wd2))
    wout = wout.reshape(8, w_out.shape[0] // 2, d)
    win_full = _combine_windows(win_all, jnp.asarray(tab), n_tiles, "combine_w_in")
    conv8 = conv_all.transpose(2, 1, 0, 3).reshape(8, 3 * d_dn)
    npc = 8
    ident = lambda p: p
    cat_map = lambda p: 2 * (p % N_CHIPS) + p // N_CHIPS

    h0 = x.reshape(t, d)
    n1 = _rms_fwd(h0, ffn1_norm, "rms1")
    a1, b1, s1 = _ffn_up(n1, wg1, wu1, "ffn1_up")
    h1 = _mm_pieces_resid(s1, wd1, h0, 0.5, ident, "ffn1_down")
    n2 = _rms_fwd(h1, mix_norm, "rms2")
    proj = _mm2d(n2, win_full, NN, F32, "in_proj")
    nh_a = d_attn // HEAD_DIM
    attn = _attn_fwd(proj, nh_a, s_, "attn_fwd")
    zc0 = (3 * d_attn + 3 * d_dn) // HEAD_DIM
    bac = (3 * d_attn + 4 * d_dn) // LANE
    row128 = lambda v: jnp.pad(v, (0, LANE - v.shape[0])).reshape(1, LANE)
    prm = jnp.concatenate([row128(dn_norm), row128(a_log), row128(dt_bias), jnp.zeros((5, LANE), F32)], axis=0)
    u_dn = _conv_fwd(proj, conv8, 3 * d_attn, 3 * d_dn, s_, "dn_conv")
    dn_out = _dn_fwd(u_dn, proj, prm, nh_d, s_, zc0, bac, "dn_fwd")
    cat_b = jnp.concatenate([attn, dn_out], axis=1).astype(BF16)
    h2 = _mm_pieces_resid(cat_b, wout, h1, 1.0, cat_map, "out_proj")
    n3 = _rms_fwd(h2, ffn2_norm, "rms3")
    a3, b3, s3 = _ffn_up(n3, wg2, wu2, "ffn2_up")
    h3 = _mm_pieces_resid(s3, wd2, h2, 0.5, ident, "ffn2_down")

    dh3, dh3b, dwf_p, lsq_p = _final_loss(h3, final_norm, loss_target.reshape(t, d), "final_loss")
    da3, db3 = _ffn_bwd_hidden(dh3b, wd2, a3, b3, "ffn2_bwd_hidden")
    g_wd2 = _grad_rows_pieces(s3, dh3b, 0.5, ident, npc, "ffn2_grad_down")
    g_wg2, g_wu2 = _grad_cols_pieces(n3, da3, db3, npc, "ffn2_grad_up")
    dn3 = _ffn_bwd_input(da3, db3, wg2, wu2, "ffn2_bwd_input")
    dh2, dh2b, dw3_p = _rms_bwd(dn3, h2, ffn2_norm, dh3, "rms3_bwd")

    dcat = _mm_nt_pieces_out(dh2b, wout, cat_map, "out_proj_bwd")
    g_wout = _grad_rows_pieces(cat_b, dh2b, 1.0, cat_map, npc, "out_proj_grad")
    dq_a, dk_a, dv_a = _attn_bwd(proj, attn, dcat, nh_a, s_, "attn_bwd")
    duq, duk, duv, dz, dba, dprm = _dn_bwd(u_dn, proj, prm, dcat, nh_d, nh_a, s_, zc0, bac, "dn_bwd")
    dx_conv, dconv8 = _conv_bwd(proj, conv8, jnp.concatenate([duq, duk, duv], axis=1), 3 * d_attn, 3 * d_dn, s_,
                                "dn_conv_bwd")
    used = 3 * d_attn + 4 * d_dn + LANE
    dproj_b = jnp.concatenate([dq_a, dk_a, dv_a, dx_conv, dz, dba.astype(BF16),
                               jnp.zeros((t, proj.shape[1] - used), BF16)], axis=1)
    dconv, ddnn, dalog, ddtb = dconv8[:CONV_WIDTH], dprm[0, :dn_norm.shape[0]], dprm[1, :nh_d], dprm[2, :nh_d]
    g_win_full = _mm2d(n2, dproj_b, TN, BF16, "in_proj_grad")
    dn2 = _mm2d(dproj_b, win_full, NT, F32, "in_proj_bwd")
    dh1, dh1b, dwm_p = _rms_bwd(dn2, h1, mix_norm, dh2, "rms2_bwd")

    da1, db1 = _ffn_bwd_hidden(dh1b, wd1, a1, b1, "ffn1_bwd_hidden")
    g_wd1 = _grad_rows_pieces(s1, dh1b, 0.5, ident, npc, "ffn1_grad_down")
    g_wg1, g_wu1 = _grad_cols_pieces(n1, da1, db1, npc, "ffn1_grad_up")
    dn1 = _ffn_bwd_input(da1, db1, wg1, wu1, "ffn1_bwd_input")
    dh0, _, dw1_p = _rms_bwd(dn1, h0, ffn1_norm, dh1, "rms1_bwd")
    grad_x = dh0.reshape(bl, s_, d)

    wh = ww // 2
    g_win = jnp.stack([jnp.stack([g_win_full[:, w0[k] + wh * h: w0[k] + wh * (h + 1)] for k in range(N_CHIPS)])
                       for h in range(2)])
    big = [g_wg1, g_wu1, g_wd1, g_win, g_wout, g_wg2, g_wu2, g_wd2]
    big = [g.reshape((2, N_CHIPS * g.shape[-2], g.shape[-1])) for g in big]
    from_sib = _sibling_take(big, "rs_sibling_take")
    pair = [_pair_sum(g, l, cidx, f"rs_pair_sum_{i}") for i, (g, l) in enumerate(zip(big, from_sib))]
    pair = [p.reshape(N_CHIPS, p.shape[0] // N_CHIPS, p.shape[1]) for p in pair]
    from_chips = _chip_all_to_all(pair, "rs_chip_all_to_all")
    halves = [_chip_sum(p, l, mcidx, f"rs_chip_sum_{i}") for i, (p, l) in enumerate(zip(pair, from_chips))]
    full = _sibling_join(halves, "rs_sibling_join")
    f_wg1, f_wu1, f_wd1, f_win, f_wout, f_wg2, f_wu2, f_wd2 = full

    unpad_cols = lambda f: jnp.concatenate([f[0], f[1][:, :fs - hp]], axis=1)
    unpad_rows = lambda f: f.reshape(2 * f.shape[1], f.shape[2])[:fs]
    gw = {
        "ffn1_w_gate": unpad_cols(f_wg1), "ffn1_w_up": unpad_cols(f_wu1), "ffn1_w_down": unpad_rows(f_wd1),
        "w_in": lax.dynamic_slice(jnp.concatenate([f_win[0], f_win[1]], axis=1), (jnp.int32(0), shift), (d, ws)),
        "w_out": f_wout.reshape(w_out.shape),
        "ffn2_w_gate": unpad_cols(f_wg2), "ffn2_w_up": unpad_cols(f_wu2), "ffn2_w_down": unpad_rows(f_wd2),
    }

    def lanes(v):
        v = v.reshape(-1)
        return jnp.pad(v, (0, _ceil_to(v.shape[0], LANE) - v.shape[0])).reshape(-1, LANE)

    small = [dw1_p.sum(0), dwm_p.sum(0), dw3_p.sum(0), dwf_p.sum(0), ddnn, dalog, ddtb,
             (0.5 / d) * jnp.sum(lsq_p).reshape(1), dconv]
    rows = [lanes(v) for v in small]
    offs = np.cumsum([0] + [r.shape[0] for r in rows])
    packed = jnp.concatenate(rows, axis=0)
    packed = jnp.pad(packed, ((0, _ceil_to(packed.shape[0], 8) - packed.shape[0]), (0, 0)))
    red = _allreduce_small(packed, "allreduce_small")
    take = lambda i, shape: red[offs[i]:offs[i + 1]].reshape(-1)[:int(np.prod(shape))].reshape(shape)
    gw["ffn1_norm"] = take(0, (d,))
    gw["mix_norm"] = take(1, (d,))
    gw["ffn2_norm"] = take(2, (d,))
    gw["final_norm"] = take(3, (d,))
    gw["dn_norm"] = take(4, dn_norm.shape)
    gw["a_log"] = take(5, a_log.shape)
    gw["dt_bias"] = take(6, dt_bias.shape)
    loss = take(7, (1,)).reshape(())
    gw["conv_w"] = lax.dynamic_slice(take(8, (CONV_WIDTH, 3 * d_dn)), (jnp.int32(0), me * cs), (CONV_WIDTH, cs))

    names = ['ffn1_norm', 'ffn1_w_gate', 'ffn1_w_up', 'ffn1_w_down', 'mix_norm', 'w_in', 'conv_w', 'a_log', 'dt_bias',
             'dn_norm', 'w_out', 'ffn2_norm', 'ffn2_w_gate', 'ffn2_w_up', 'ffn2_w_down', 'final_norm']
    wv = dict(zip(names, (ffn1_norm, ffn1_w_gate, ffn1_w_up, ffn1_w_down, mix_norm, w_in, conv_w, a_log, dt_bias,
                          dn_norm, w_out, ffn2_norm, ffn2_w_gate, ffn2_w_up, ffn2_w_down, final_norm)))
    mv = dict(zip(names, (m_ffn1_norm, m_ffn1_w_gate, m_ffn1_w_up, m_ffn1_w_down, m_mix_norm, m_w_in, m_conv_w, m_a_log,
                          m_dt_bias, m_dn_norm, m_w_out, m_ffn2_norm, m_ffn2_w_gate, m_ffn2_w_up, m_ffn2_w_down,
                          m_final_norm)))
    vv = dict(zip(names, (v_ffn1_norm, v_ffn1_w_gate, v_ffn1_w_up, v_ffn1_w_down, v_mix_norm, v_w_in, v_conv_w, v_a_log,
                          v_dt_bias, v_dn_norm, v_w_out, v_ffn2_norm, v_ffn2_w_gate, v_ffn2_w_up, v_ffn2_w_down,
                          v_final_norm)))
    delta, new_m, new_v = {}, {}, {}
    small_names = [n for n in names if wv[n].ndim == 1 or n == "conv_w"]
    for n in names:
        if n in small_names:
            continue
        delta[n], new_m[n], new_v[n] = _adamw(gw[n], wv[n], mv[n], vv[n], f"adamw_{n}")
    srows = {n: lanes(gw[n]).shape[0] for n in small_names}
    soffs = np.cumsum([0] + [srows[n] for n in small_names])
    stot = _ceil_to(int(soffs[-1]), 8)

    def pack(dct):
        p = jnp.concatenate([lanes(dct[n]) for n in small_names], axis=0)
        return jnp.pad(p, ((0, stot - p.shape[0]), (0, 0)))

    sd, sm, sv = _adamw(pack(gw), pack(wv), pack(mv), pack(vv), "adamw_small")
    for i, n in enumerate(small_names):
        cut = lambda p: p[soffs[i]:soffs[i + 1]].reshape(-1)[:wv[n].size].reshape(wv[n].shape)
        delta[n], new_m[n], new_v[n] = cut(sd), cut(sm), cut(sv)

    return (loss, grad_x, *[gw[n] for n in names], *[delta[n] for n in names], *[new_m[n] for n in names],
            *[new_v[n] for n in names])
```

```python
import functools
import math

import jax
import jax.numpy as jnp
import numpy as np
from jax import lax
from jax.experimental import pallas as pl
from jax.experimental.pallas import tpu as pltpu

F32 = jnp.float32
BF16 = jnp.bfloat16
MESH = pl.DeviceIdType.MESH
ANY = pl.BlockSpec(memory_space=pl.ANY)

LANE = 128
N_CHIPS = 4
N_DEV = 8
EPS = 1e-6
HEAD_DIM = 128
CONV_WIDTH = 4
CHUNK = 64
ATTN_BLOCK = 128
DILATED_CONFIGS = ((128, 1), (512, 4), (2048, 16))
VMEM_LIMIT = 52 * 1024 * 1024

ADAM_LR = 0.001
ADAM_B1 = 0.9
ADAM_B2 = 0.999
ADAM_EPS = 1e-08
ADAM_WD = 0.01
ADAM_STEP = 10

NN = (((1,), (0,)), ((), ()))
NT = (((1,), (1,)), ((), ()))
TN = (((0,), (0,)), ((), ()))


def _ceil_to(v, m):
    return -(-v // m) * m


def _params(vmem=VMEM_LIMIT):
    return pltpu.CompilerParams(vmem_limit_bytes=vmem)


class _Comm:
    def __init__(self, ins, out_shape, aliases, sems, start, mid, finish):
        self.ins, self.out_shape, self.aliases, self.sems = ins, out_shape, aliases, sems
        self.start, self.mid, self.finish = start, mid, finish


def _gemm(name, grid, pairs, dn, acc_shape, n_acc, extras, outs, epilogue, comm=None):
    n_pairs, n_ex, n_out = len(pairs), len(extras), len(outs)
    n_ci = len(comm.ins) if comm else 0
    n_co = len(comm.out_shape) if comm else 0
    n_sem = len(comm.sems) if comm else 0
    kax = len(grid) - 1
    nk = grid[kax]
    n_in = 2 * n_pairs + n_ex

    def body(*refs):
        ins = refs[: 2 * n_pairs]
        ex = refs[2 * n_pairs: n_in]
        c_in = refs[n_in: n_in + n_ci]
        out_refs = refs[n_in + n_ci: n_in + n_ci + n_out]
        c_out = refs[n_in + n_ci + n_out: n_in + n_ci + n_out + n_co]
        accs = refs[n_in + n_ci + n_out + n_co: n_in + n_ci + n_out + n_co + n_acc]
        sems = refs[n_in + n_ci + n_out + n_co + n_acc:]
        k = pl.program_id(kax)
        pids = [pl.program_id(a) for a in range(len(grid))]

        def at(point):
            cond = pids[0] == point[0]
            for pid, v in zip(pids[1:], point[1:]):
                cond = cond & (pid == v)
            return cond

        if comm:
            @pl.when(at([0] * len(grid)))
            def _():
                comm.start(c_in, c_out, sems)

            if comm.mid:
                @pl.when(at([grid[0] * 3 // 4] + [0] * (len(grid) - 1)))
                def _():
                    comm.mid(c_in, c_out, sems)

        @pl.when(k == 0)
        def _():
            for acc in accs:
                acc[...] = jnp.zeros(acc.shape, F32)

        for q in range(n_pairs):
            a = ins[2 * q][...]
            b = ins[2 * q + 1][...]
            if a.dtype != BF16:
                a = a.astype(BF16)
            if b.dtype != BF16:
                b = b.astype(BF16)
            accs[pairs[q][4]][...] += lax.dot_general(a, b, dn, preferred_element_type=F32)

        @pl.when(k == nk - 1)
        def _():
            res = epilogue([acc[...] for acc in accs], [e[...] for e in ex])
            for o, r in zip(out_refs, res):
                o[...] = r.astype(o.dtype)

        if comm:
            @pl.when(at([g - 1 for g in grid]))
            def _():
                comm.finish(c_in, c_out, sems)

    in_specs = []
    args = []
    for a, a_spec, b, b_spec, _ in pairs:
        in_specs += [a_spec, b_spec]
        args += [a, b]
    for e, e_spec in extras:
        in_specs.append(e_spec)
        args.append(e)
    out_shape = [o for o, _ in outs]
    out_specs = [s for _, s in outs]
    scratch = [pltpu.VMEM(acc_shape, F32) for _ in range(n_acc)]
    kwargs = {}
    if comm:
        in_specs += [ANY] * n_ci
        args += list(comm.ins)
        out_shape += list(comm.out_shape)
        out_specs += [ANY] * n_co
        scratch += list(comm.sems)
        kwargs["input_output_aliases"] = {n_in + i: n_out + o for i, o in comm.aliases.items()}
    res = pl.pallas_call(body, name=name, grid=grid, in_specs=in_specs, out_specs=out_specs,
                         out_shape=out_shape, scratch_shapes=scratch, compiler_params=_params(), **kwargs)(*args)
    if comm:
        return list(res[:n_out]), list(res[n_out:])
    return res


def _unpack(res, comm, single):
    if comm:
        outs, couts = res
        return (outs[0] if single else outs), couts
    return res[0] if single else res


def _pick(n, prefs):
    for p in prefs:
        if n % p == 0:
            return p
    return n


def _sigmoid(v):
    return 1.0 / (1.0 + jnp.exp(-v))


def _ffn_up(n, wg, wu, name, comm=None):
    t, d = n.shape
    npieces, _, hp = wg.shape
    tm = _pick(t, (512, 256, 128, 64, 32, 16))
    tk = _pick(d, (1024, 512, 256, 128))
    grid = (t // tm, npieces, d // tk)
    a_spec = pl.BlockSpec((tm, tk), lambda i, p, k: (i, k))
    w_spec = pl.BlockSpec((None, tk, hp), lambda i, p, k: (p, k, 0))
    o_spec = pl.BlockSpec((tm, hp), lambda i, p, k: (i, p))
    osd = jax.ShapeDtypeStruct((t, npieces * hp), BF16)

    def epi(accs, ex):
        a, b = accs
        return a, b, a * _sigmoid(a) * b

    return _unpack(_gemm(name, grid, [(n, a_spec, wg, w_spec, 0), (n, a_spec, wu, w_spec, 1)], NN, (tm, hp), 2, [],
                         [(osd, o_spec)] * 3, epi, comm), comm, False)


def _mm_pieces_resid(a, w, resid, scale, amap, name, comm=None):
    t = a.shape[0]
    npieces, kp, n = w.shape
    tm = _pick(t, (1024, 512, 256, 128, 64, 32, 16))
    tn = _pick(n, (1024, 512, 256, 128))
    grid = (t // tm, n // tn, npieces)
    a_spec = pl.BlockSpec((tm, kp), lambda i, j, p: (i, amap(p)))
    w_spec = pl.BlockSpec((None, kp, tn), lambda i, j, p: (p, 0, j))
    r_spec = pl.BlockSpec((tm, tn), lambda i, j, p: (i, j))

    def epi(accs, ex):
        return (ex[0] + scale * accs[0],)

    return _unpack(_gemm(name, grid, [(a, a_spec, w, w_spec, 0)], NN, (tm, tn), 1, [(resid, r_spec)],
                         [(jax.ShapeDtypeStruct((t, n), F32), r_spec)], epi, comm), comm, True)


def _ffn_bwd_hidden(dh, wd, a, b, name):
    t, d = dh.shape
    npieces, hp, _ = wd.shape
    tm = _pick(t, (512, 256, 128, 64, 32, 16))
    tk = _pick(d, (1024, 512, 256, 128))
    grid = (t // tm, npieces, d // tk)
    a_spec = pl.BlockSpec((tm, tk), lambda i, p, k: (i, k))
    w_spec = pl.BlockSpec((None, hp, tk), lambda i, p, k: (p, 0, k))
    o_spec = pl.BlockSpec((tm, hp), lambda i, p, k: (i, p))
    osd = jax.ShapeDtypeStruct((t, npieces * hp), BF16)

    def epi(accs, ex):
        ds = 0.5 * accs[0]
        av = ex[0].astype(F32)
        bv = ex[1].astype(F32)
        sg = _sigmoid(av)
        da = ds * bv * (sg * (1.0 + av * (1.0 - sg)))
        db = ds * (av * sg)
        return da, db

    return _gemm(name, grid, [(dh, a_spec, wd, w_spec, 0)], NT, (tm, hp), 1, [(a, o_spec), (b, o_spec)],
                 [(osd, o_spec)] * 2, epi)


def _mm_nt_pieces_out(dh, w, omap, name):
    t, d = dh.shape
    npieces, npp, _ = w.shape
    tm = _pick(t, (1024, 512, 256, 128, 64, 32, 16))
    tk = _pick(d, (1024, 512, 256, 128))
    grid = (t // tm, npieces, d // tk)
    a_spec = pl.BlockSpec((tm, tk), lambda i, p, k: (i, k))
    w_spec = pl.BlockSpec((None, npp, tk), lambda i, p, k: (p, 0, k))
    o_spec = pl.BlockSpec((tm, npp), lambda i, p, k: (i, omap(p)))
    return _gemm(name, grid, [(dh, a_spec, w, w_spec, 0)], NT, (tm, npp), 1, [],
                 [(jax.ShapeDtypeStruct((t, npieces * npp), BF16), o_spec)], lambda accs, ex: (accs[0],))[0]


def _grad_rows_pieces(x, dy, scale, amap, npieces, name):
    t, n = dy.shape
    mp = x.shape[1] // npieces
    tn = _pick(n, (1024, 512, 256, 128))
    tk = _pick(t, (1024, 512, 256, 128, 64, 32, 16))
    grid = (npieces, n // tn, t // tk)
    x_spec = pl.BlockSpec((tk, mp), lambda p, j, k: (k, amap(p)))
    y_spec = pl.BlockSpec((tk, tn), lambda p, j, k: (k, j))
    o_spec = pl.BlockSpec((None, mp, tn), lambda p, j, k: (p, 0, j))
    return _gemm(name, grid, [(x, x_spec, dy, y_spec, 0)], TN, (mp, tn), 1, [],
                 [(jax.ShapeDtypeStruct((npieces, mp, n), BF16), o_spec)], lambda accs, ex: (scale * accs[0],))[0]


def _grad_cols_pieces(n, da, db, npieces, name, comm=None):
    t, d = n.shape
    hp = da.shape[1] // npieces
    tm = _pick(d, (1024, 512, 256, 128))
    tk = _pick(t, (512, 256, 128, 64, 32, 16))
    grid = (npieces, d // tm, t // tk)
    n_spec = pl.BlockSpec((tk, tm), lambda p, i, k: (k, i))
    g_spec = pl.BlockSpec((tk, hp), lambda p, i, k: (k, p))
    o_spec = pl.BlockSpec((None, tm, hp), lambda p, i, k: (p, i, 0))
    osd = jax.ShapeDtypeStruct((npieces, d, hp), BF16)
    return _unpack(_gemm(name, grid, [(n, n_spec, da, g_spec, 0), (n, n_spec, db, g_spec, 1)], TN, (tm, hp), 2, [],
                         [(osd, o_spec)] * 2, lambda accs, ex: (accs[0], accs[1]), comm), comm, False)


def _ffn_bwd_input(da, db, wg, wu, name, comm=None):
    t = da.shape[0]
    npieces, d, hp = wg.shape
    tm = _pick(t, (1024, 512, 256, 128, 64, 32, 16))
    tn = _pick(d, (1024, 512, 256, 128))
    grid = (t // tm, d // tn, npieces)
    g_spec = pl.BlockSpec((tm, hp), lambda i, j, p: (i, p))
    w_spec = pl.BlockSpec((None, tn, hp), lambda i, j, p: (p, j, 0))
    o_spec = pl.BlockSpec((tm, tn), lambda i, j, p: (i, j))
    return _unpack(_gemm(name, grid, [(da, g_spec, wg, w_spec, 0), (db, g_spec, wu, w_spec, 0)], NT, (tm, tn), 1, [],
                         [(jax.ShapeDtypeStruct((t, d), F32), o_spec)], lambda accs, ex: (accs[0],), comm), comm, True)


def _mm2d(a, b, dn, out_dtype, name, tiles, comm=None):
    if dn == NN:
        m, kk = a.shape
        n = b.shape[1]
    elif dn == NT:
        m, kk = a.shape
        n = b.shape[0]
    else:
        kk, m = a.shape
        n = b.shape[1]
    tm = _pick(m, (tiles[0],) + (1024, 512, 256, 128, 64, 32, 16))
    tn = _pick(n, (tiles[1], 768, 1024, 512, 256, 128))
    tk = _pick(kk, (tiles[2], 768, 1024, 512, 256, 128, 64, 32, 16))
    grid = (m // tm, n // tn, kk // tk)
    if dn == TN:
        a_spec = pl.BlockSpec((tk, tm), lambda i, j, k: (k, i))
    else:
        a_spec = pl.BlockSpec((tm, tk), lambda i, j, k: (i, k))
    if dn == NT:
        b_spec = pl.BlockSpec((tn, tk), lambda i, j, k: (j, k))
    else:
        b_spec = pl.BlockSpec((tk, tn), lambda i, j, k: (k, j))
    o_spec = pl.BlockSpec((tm, tn), lambda i, j, k: (i, j))
    return _unpack(_gemm(name, grid, [(a, a_spec, b, b_spec, 0)], dn, (tm, tn), 1, [],
                         [(jax.ShapeDtypeStruct((m, n), out_dtype), o_spec)], lambda accs, ex: (accs[0],), comm), comm, True)


def _row_tile(t):
    return _pick(t, (256, 128, 64, 32, 16, 8))


def _rms_fwd(x, w, name):
    t, d = x.shape
    tm = _row_tile(t)

    def body(x_ref, w_ref, o_ref):
        xv = x_ref[...]
        r = lax.rsqrt(jnp.mean(xv * xv, axis=-1, keepdims=True) + EPS)
        o_ref[...] = (xv * r * w_ref[...]).astype(BF16)

    return pl.pallas_call(
        body, name=name, grid=(t // tm,),
        in_specs=[pl.BlockSpec((tm, d), lambda i: (i, 0)), pl.BlockSpec((1, d), lambda i: (0, 0))],
        out_specs=pl.BlockSpec((tm, d), lambda i: (i, 0)),
        out_shape=jax.ShapeDtypeStruct((t, d), BF16), compiler_params=_params())(x, w.reshape(1, d))


def _rms_bwd(dn, x, w, dres, name):
    t, d = x.shape
    tm = _row_tile(t)

    def body(dn_ref, x_ref, w_ref, r_ref, o_ref, ob_ref, dw_ref):
        i = pl.program_id(0)
        xv = x_ref[...]
        r = lax.rsqrt(jnp.mean(xv * xv, axis=-1, keepdims=True) + EPS)
        xh = xv * r
        dy = dn_ref[...].astype(F32)
        g = dy * w_ref[...]
        dx = r * (g - xh * jnp.mean(g * xh, axis=-1, keepdims=True))
        tot = r_ref[...] + dx
        o_ref[...] = tot
        ob_ref[...] = tot.astype(BF16)
        part = (dy * xh).reshape(tm // 8, 8, d).sum(axis=0)

        @pl.when(i == 0)
        def _():
            dw_ref[...] = part

        @pl.when(i > 0)
        def _():
            dw_ref[...] += part

    row = pl.BlockSpec((tm, d), lambda i: (i, 0))
    return pl.pallas_call(
        body, name=name, grid=(t // tm,),
        in_specs=[row, row, pl.BlockSpec((1, d), lambda i: (0, 0)), row],
        out_specs=[row, row, pl.BlockSpec((8, d), lambda i: (0, 0))],
        out_shape=[jax.ShapeDtypeStruct((t, d), F32), jax.ShapeDtypeStruct((t, d), BF16),
                   jax.ShapeDtypeStruct((8, d), F32)],
        compiler_params=_params())(dn, x, w.reshape(1, d), dres)


def _final_loss(h, w, target, name):
    t, d = h.shape
    tm = _row_tile(t)

    def body(h_ref, w_ref, t_ref, o_ref, ob_ref, dw_ref, ls_ref):
        i = pl.program_id(0)
        xv = h_ref[...]
        r = lax.rsqrt(jnp.mean(xv * xv, axis=-1, keepdims=True) + EPS)
        xh = xv * r
        err = xh * w_ref[...] - t_ref[...]
        dy = err * (1.0 / d)
        g = dy * w_ref[...]
        dx = r * (g - xh * jnp.mean(g * xh, axis=-1, keepdims=True))
        o_ref[...] = dx
        ob_ref[...] = dx.astype(BF16)
        part = (dy * xh).reshape(tm // 8, 8, d).sum(axis=0)
        lpart = (err * err).reshape(tm // 8, 8, d).sum(axis=0)

        @pl.when(i == 0)
        def _():
            dw_ref[...] = part
            ls_ref[...] = lpart

        @pl.when(i > 0)
        def _():
            dw_ref[...] += part
            ls_ref[...] += lpart

    row = pl.BlockSpec((tm, d), lambda i: (i, 0))
    acc = pl.BlockSpec((8, d), lambda i: (0, 0))
    return pl.pallas_call(
        body, name=name, grid=(t // tm,),
        in_specs=[row, pl.BlockSpec((1, d), lambda i: (0, 0)), row],
        out_specs=[row, row, acc, acc],
        out_shape=[jax.ShapeDtypeStruct((t, d), F32), jax.ShapeDtypeStruct((t, d), BF16),
                   jax.ShapeDtypeStruct((8, d), F32), jax.ShapeDtypeStruct((8, d), F32)],
        compiler_params=_params())(h, w.reshape(1, d), target)


def _cast_split_cols(w, hp, me, name):
    r, fs = w.shape
    v1 = fs - hp
    tm = _pick(r, (256, 128, 64, 32, 16))

    def body(me_ref, w_ref, o_ref):
        o_ref[0] = w_ref[:, :hp].astype(BF16)
        if v1 < hp:
            o_ref[1] = jnp.zeros((tm, hp), BF16)
        o_ref[1, :, :v1] = w_ref[:, hp:].astype(BF16)

    gs = pltpu.PrefetchScalarGridSpec(
        num_scalar_prefetch=1, grid=(r // tm,),
        in_specs=[pl.BlockSpec((tm, fs), lambda i, mr: (i, 0))],
        out_specs=pl.BlockSpec((2, None, tm, hp), lambda i, mr: (0, mr[0], i, 0)))
    return pl.pallas_call(body, name=name, grid_spec=gs, out_shape=jax.ShapeDtypeStruct((2, N_CHIPS, r, hp), BF16),
                          compiler_params=_params())(me, w)


def _cast_split_rows(w, hp, tr, me, name):
    fs, c = w.shape
    nvalid = fs // tr
    per = hp // tr

    def body(me_ref, w_ref, o_ref):
        i = pl.program_id(0)

        @pl.when(i < nvalid)
        def _():
            o_ref[...] = w_ref[...].astype(BF16)

        @pl.when(i >= nvalid)
        def _():
            o_ref[...] = jnp.zeros(o_ref.shape, BF16)

    gs = pltpu.PrefetchScalarGridSpec(
        num_scalar_prefetch=1, grid=(2 * per,),
        in_specs=[pl.BlockSpec((tr, c), lambda i, mr: (jnp.minimum(i, nvalid - 1), 0))],
        out_specs=pl.BlockSpec((None, None, tr, c), lambda i, mr: (i // per, mr[0], i % per, 0)))
    return pl.pallas_call(body, name=name, grid_spec=gs, out_shape=jax.ShapeDtypeStruct((2, N_CHIPS, hp, c), BF16),
                          compiler_params=_params())(me, w)


def _combine_windows(wall, tables, n_tiles, name):
    _, _, d, wh = wall.shape
    tpw = wh // LANE

    def body(tab_ref, a_ref, b_ref, o_ref):
        t = pl.program_id(0)
        both = tab_ref[6, t] == 1
        av = a_ref[...]
        bv = b_ref[...]
        o_ref[...] = jnp.where(both, av + bv, av)

    def amap(t, tab):
        return (tab[0, t], tab[1, t], 0, tab[2, t])

    def bmap(t, tab):
        return (tab[3, t], tab[4, t], 0, tab[5, t])

    gs = pltpu.PrefetchScalarGridSpec(
        num_scalar_prefetch=1, grid=(n_tiles,),
        in_specs=[pl.BlockSpec((None, None, d, LANE), amap), pl.BlockSpec((None, None, d, LANE), bmap)],
        out_specs=pl.BlockSpec((d, LANE), lambda t, tab: (0, t)))
    del tpw
    return pl.pallas_call(body, name=name, grid_spec=gs, out_shape=jax.ShapeDtypeStruct((d, n_tiles * LANE), BF16),
                          compiler_params=_params())(tables, wall, wall)


def _coords():
    return lax.axis_index("x"), lax.axis_index("y"), lax.axis_index("c")


def _remote(src, dst, ssem, rsem, dev):
    return pltpu.make_async_remote_copy(src_ref=src, dst_ref=dst, send_sem=ssem, recv_sem=rsem, device_id=dev,
                                        device_id_type=MESH)


def _mesh_places():
    x, y, c = _coords()
    return c, 2 * x + y, (x, y, 1 - c), [(1 - x, y), (x, 1 - y), (1 - x, 1 - y)]


def _all_gather_comm(bufs):
    n = len(bufs)

    def start(ins, outs, sems):
        c, me, _, chips = _mesh_places()
        for i in range(n):
            for j, (px, py) in enumerate(chips):
                mine = outs[i].at[c, me]
                _remote(mine, mine, sems[0].at[i, j], sems[1].at[i, j], (px, py, c)).start()

    def mid(ins, outs, sems):
        c, _, sib, chips = _mesh_places()
        for i in range(n):
            for j, (px, py) in enumerate(chips):
                slot = outs[i].at[c, 2 * px + py]
                _remote(slot, slot, sems[0].at[i, j], sems[1].at[i, j], (px, py, c)).wait_recv()
                _remote(slot, slot, sems[2].at[i, j], sems[3].at[i, j], sib).start()

    def finish(ins, outs, sems):
        c, me, sib, chips = _mesh_places()
        for i in range(n):
            for j, (px, py) in enumerate(chips):
                slot = outs[i].at[1 - c, 2 * px + py]
                _remote(slot, slot, sems[2].at[i, j], sems[3].at[i, j], sib).wait_recv()
        for i in range(n):
            for j, (px, py) in enumerate(chips):
                mine = outs[i].at[c, me]
                _remote(mine, mine, sems[0].at[i, j], sems[1].at[i, j], (px, py, c)).wait_send()
                slot = outs[i].at[c, 2 * px + py]
                _remote(slot, slot, sems[2].at[i, j], sems[3].at[i, j], sib).wait_send()

    return _Comm(list(bufs), [jax.ShapeDtypeStruct(b.shape, b.dtype) for b in bufs], {i: i for i in range(n)},
                 [pltpu.SemaphoreType.DMA((n, 3))] * 4, start, mid, finish)


def _chip_all_to_all_comm(ps):
    n = len(ps)

    def start(ins, outs, sems):
        c, me, _, chips = _mesh_places()
        for i in range(n):
            for j, (px, py) in enumerate(chips):
                _remote(ins[i].at[2 * px + py], outs[i].at[me], sems[0].at[i, j], sems[1].at[i, j], (px, py, c)).start()

    def finish(ins, outs, sems):
        c, me, _, chips = _mesh_places()
        for i in range(n):
            for j, (px, py) in enumerate(chips):
                slot = outs[i].at[2 * px + py]
                _remote(slot, slot, sems[0].at[i, j], sems[1].at[i, j], (px, py, c)).wait_recv()
        for i in range(n):
            for j, (px, py) in enumerate(chips):
                _remote(ins[i].at[2 * px + py], outs[i].at[me], sems[0].at[i, j], sems[1].at[i, j],
                        (px, py, c)).wait_send()

    return _Comm(list(ps), [jax.ShapeDtypeStruct(p.shape, p.dtype) for p in ps], {},
                 [pltpu.SemaphoreType.DMA((n, 3))] * 2, start, None, finish)


def _comm_call(comm, name):
    n_in, n_out = len(comm.ins), len(comm.out_shape)

    def body(*refs):
        ins, outs, sems = refs[:n_in], refs[n_in:n_in + n_out], refs[n_in + n_out:]
        comm.start(ins, outs, sems)
        if comm.mid:
            comm.mid(ins, outs, sems)
        comm.finish(ins, outs, sems)

    return pl.pallas_call(body, name=name, in_specs=[ANY] * n_in, out_specs=[ANY] * n_out, out_shape=comm.out_shape,
                          input_output_aliases=dict(comm.aliases), scratch_shapes=list(comm.sems))(*comm.ins)


def _sibling_take(gs, name):
    n = len(gs)

    def body(*refs):
        g, out = refs[:n], refs[n:2 * n]
        ssem, rsem = refs[2 * n:]
        x, y, c = _coords()
        sib = (x, y, 1 - c)
        cps = []
        for i in range(n):
            cp = _remote(g[i].at[1 - c], out[i], ssem.at[i], rsem.at[i], sib)
            cp.start()
            cps.append(cp)
        for cp in cps:
            cp.wait()

    out_shape = [jax.ShapeDtypeStruct(s.shape[1:], s.dtype) for s in gs]
    return pl.pallas_call(
        body, name=name, in_specs=[ANY] * n, out_specs=[ANY] * n, out_shape=out_shape,
        scratch_shapes=[pltpu.SemaphoreType.DMA((n,)), pltpu.SemaphoreType.DMA((n,))])(*gs)


def _sibling_join(bufs, name):
    n = len(bufs)

    def body(*refs):
        out = refs[n:2 * n]
        ssem, rsem = refs[2 * n:]
        x, y, c = _coords()
        sib = (x, y, 1 - c)
        cps = []
        for i in range(n):
            mine = out[i].at[c]
            cp = _remote(mine, mine, ssem.at[i], rsem.at[i], sib)
            cp.start()
            cps.append(cp)
        for i in range(n):
            slot = out[i].at[1 - c]
            _remote(slot, slot, ssem.at[i], rsem.at[i], sib).wait_recv()
        for cp in cps:
            cp.wait_send()

    out_shape = [jax.ShapeDtypeStruct(b.shape, b.dtype) for b in bufs]
    return pl.pallas_call(
        body, name=name, in_specs=[ANY] * n, out_specs=[ANY] * n, out_shape=out_shape,
        input_output_aliases={i: i for i in range(n)},
        scratch_shapes=[pltpu.SemaphoreType.DMA((n,)), pltpu.SemaphoreType.DMA((n,))])(*bufs)


def _allreduce_small(vec, name):
    r = vec.shape[0]

    def body(v_ref, o_ref, buf, ssem, rsem):
        x, y, c = _coords()
        my = 4 * x + 2 * y + c
        buf[my] = v_ref[...]
        cps = []
        for dd in range(1, N_DEV):
            px = 1 - x if (dd >> 2) & 1 else x
            py = 1 - y if (dd >> 1) & 1 else y
            pc = 1 - c if dd & 1 else c
            cp = _remote(v_ref, buf.at[my], ssem.at[dd - 1], rsem.at[dd - 1], (px, py, pc))
            cp.start()
            cps.append(cp)
        for dd in range(1, N_DEV):
            px = 1 - x if (dd >> 2) & 1 else x
            py = 1 - y if (dd >> 1) & 1 else y
            pc = 1 - c if dd & 1 else c
            slot = buf.at[4 * px + 2 * py + pc]
            _remote(slot, slot, ssem.at[dd - 1], rsem.at[dd - 1], (px, py, pc)).wait_recv()
        tot = buf[0]
        for k in range(1, N_DEV):
            tot = tot + buf[k]
        o_ref[...] = tot
        for cp in cps:
            cp.wait_send()

    vm = pl.BlockSpec(memory_space=pltpu.VMEM)
    return pl.pallas_call(
        body, name=name, in_specs=[vm], out_specs=vm, out_shape=jax.ShapeDtypeStruct((r, LANE), F32),
        scratch_shapes=[pltpu.VMEM((N_DEV, r, LANE), F32), pltpu.SemaphoreType.DMA((N_DEV - 1,)),
                        pltpu.SemaphoreType.DMA((N_DEV - 1,))])(vec)


def _pair_sum(g, l1, cidx, name):
    _, r, c = g.shape
    tr = _pick(r, (512, 256, 128, 64, 32, 16))

    def body(c_ref, g_ref, l_ref, o_ref):
        o_ref[...] = (g_ref[...].astype(F32) + l_ref[...].astype(F32)).astype(BF16)

    gs = pltpu.PrefetchScalarGridSpec(
        num_scalar_prefetch=1, grid=(r // tr,),
        in_specs=[pl.BlockSpec((None, tr, c), lambda i, cr: (cr[0], i, 0)), pl.BlockSpec((tr, c), lambda i, cr: (i, 0))],
        out_specs=pl.BlockSpec((tr, c), lambda i, cr: (i, 0)))
    return pl.pallas_call(body, name=name, grid_spec=gs, out_shape=jax.ShapeDtypeStruct((r, c), BF16),
                          compiler_params=_params())(cidx, g, l1)


def _chip_sum(p, l2, mc, name):
    _, r, c = l2.shape
    tr = _pick(r, (256, 128, 64, 32, 16))

    def body(mc_ref, p_ref, l0, l1, l2_, l3, o_ref):
        me = mc_ref[0]
        pv = p_ref[...].astype(F32)
        tot = None
        for k, lr in enumerate((l0, l1, l2_, l3)):
            term = jnp.where(me == k, pv, lr[...].astype(F32))
            tot = term if tot is None else tot + term
        o_ref[...] = tot

    def other(k):
        return lambda i, mr: (jnp.where(mr[0] == k, (k + 1) % N_CHIPS, k), i, 0)

    gs = pltpu.PrefetchScalarGridSpec(
        num_scalar_prefetch=1, grid=(r // tr,),
        in_specs=[pl.BlockSpec((None, tr, c), lambda i, mr: (mr[0], i, 0))]
        + [pl.BlockSpec((None, tr, c), other(k)) for k in range(N_CHIPS)],
        out_specs=pl.BlockSpec((None, tr, c), lambda i, mr: (mr[1], i, 0)))
    return pl.pallas_call(body, name=name, grid_spec=gs, out_shape=jax.ShapeDtypeStruct((2, r, c), F32),
                          compiler_params=_params())(mc, p, l2, l2, l2, l2)


def _adamw(g, w, m, v, name):
    r, c = w.shape
    tr = r
    if r * c * 4 > (2 << 20):
        tr = next(p for p in (256, 128, 64, 32, 16, 8) if r % p == 0 and (p * c * 4 <= (2 << 20) or p == 8))

    def body(g_ref, w_ref, m_ref, v_ref, d_ref, nm_ref, nv_ref):
        gv = g_ref[...]
        mn = ADAM_B1 * m_ref[...] + (1.0 - ADAM_B1) * gv
        vn = ADAM_B2 * v_ref[...] + (1.0 - ADAM_B2) * (gv * gv)
        m_hat = mn / (1.0 - ADAM_B1 ** ADAM_STEP)
        v_hat = vn / (1.0 - ADAM_B2 ** ADAM_STEP)
        d_ref[...] = -ADAM_LR * (m_hat / (jnp.sqrt(v_hat) + ADAM_EPS) + ADAM_WD * w_ref[...])
        nm_ref[...] = mn
        nv_ref[...] = vn

    blk = pl.BlockSpec((tr, c), lambda i: (i, 0))
    osd = jax.ShapeDtypeStruct((r, c), F32)
    return pl.pallas_call(body, name=name, grid=(r // tr,), in_specs=[blk] * 4, out_specs=[blk] * 3,
                          out_shape=[osd] * 3, compiler_params=_params())(g, w, m, v)


def _attn_probs(q, k, q0, s_len):
    tq = q.shape[0]
    sc = lax.dot_general(q, k, NT, preferred_element_type=F32) * (HEAD_DIM ** -0.5)
    dlt = (q0 + lax.broadcasted_iota(jnp.int32, (tq, s_len), 0)) - lax.broadcasted_iota(jnp.int32, (tq, s_len), 1)
    cnt = jnp.zeros((tq, s_len), F32)
    for window, dil in DILATED_CONFIGS:
        seen = (dlt >= 0) & (dlt <= window) & ((dlt & (dil - 1)) == 0)
        cnt = cnt + jnp.where(seen, 1.0, 0.0)
    live = cnt > 0.0
    m = jnp.max(jnp.where(live, sc, -jnp.inf), axis=-1, keepdims=True)
    p = cnt * jnp.exp(jnp.where(live, sc - m, -jnp.inf))
    return p / jnp.sum(p, axis=-1, keepdims=True)


def _attn_fwd(proj, nh, s_len, name):
    t = proj.shape[0]
    tq = min(256, s_len)
    nq = s_len // tq

    def body(q_ref, k_ref, v_ref, o_ref):
        qi = pl.program_id(2)
        p = _attn_probs(q_ref[...].astype(BF16), k_ref[...].astype(BF16), qi * tq, s_len)
        o_ref[...] = jnp.dot(p.astype(BF16), v_ref[...].astype(BF16), preferred_element_type=F32)

    q_spec = pl.BlockSpec((tq, HEAD_DIM), lambda b, h, qi: (b * nq + qi, h))
    return pl.pallas_call(
        body, name=name, grid=(t // s_len, nh, nq),
        in_specs=[q_spec, pl.BlockSpec((s_len, HEAD_DIM), lambda b, h, qi: (b, nh + h)),
                  pl.BlockSpec((s_len, HEAD_DIM), lambda b, h, qi: (b, 2 * nh + h))],
        out_specs=q_spec, out_shape=jax.ShapeDtypeStruct((t, nh * HEAD_DIM), F32),
        compiler_params=_params())(proj, proj, proj)


def _attn_bwd(proj, o, do, nh, s_len, name):
    t = proj.shape[0]
    tq = min(256, s_len)
    nq = s_len // tq
    scale = HEAD_DIM ** -0.5

    def body(q_ref, k_ref, v_ref, o_ref, do_ref, dq_ref, dk_ref, dv_ref, dk_acc, dv_acc):
        qi = pl.program_id(2)
        q = q_ref[...].astype(BF16)
        k = k_ref[...].astype(BF16)
        p = _attn_probs(q, k, qi * tq, s_len)
        dob = do_ref[...]
        dp = lax.dot_general(dob, v_ref[...].astype(BF16), NT, preferred_element_type=F32)
        delta = jnp.sum(dob.astype(F32) * o_ref[...], axis=-1, keepdims=True)
        ds = (p * (dp - delta)).astype(BF16)
        dq_ref[...] = (jnp.dot(ds, k, preferred_element_type=F32) * scale).astype(BF16)
        dk_part = lax.dot_general(ds, q, TN, preferred_element_type=F32) * scale
        dv_part = lax.dot_general(p.astype(BF16), dob, TN, preferred_element_type=F32)

        @pl.when(qi == 0)
        def _():
            dk_acc[...] = dk_part
            dv_acc[...] = dv_part

        @pl.when(qi > 0)
        def _():
            dk_acc[...] += dk_part
            dv_acc[...] += dv_part

        @pl.when(qi == nq - 1)
        def _():
            dk_ref[...] = dk_acc[...].astype(BF16)
            dv_ref[...] = dv_acc[...].astype(BF16)

    q_spec = pl.BlockSpec((tq, HEAD_DIM), lambda b, h, qi: (b * nq + qi, h))
    kv_out = pl.BlockSpec((s_len, HEAD_DIM), lambda b, h, qi: (b, h))
    osd = jax.ShapeDtypeStruct((t, nh * HEAD_DIM), BF16)
    return pl.pallas_call(
        body, name=name, grid=(t // s_len, nh, nq),
        in_specs=[q_spec, pl.BlockSpec((s_len, HEAD_DIM), lambda b, h, qi: (b, nh + h)),
                  pl.BlockSpec((s_len, HEAD_DIM), lambda b, h, qi: (b, 2 * nh + h)), q_spec, q_spec],
        out_specs=[q_spec, kv_out, kv_out], out_shape=[osd, osd, osd],
        scratch_shapes=[pltpu.VMEM((s_len, HEAD_DIM), F32), pltpu.VMEM((s_len, HEAD_DIM), F32)],
        compiler_params=_params())(proj, proj, proj, o, do)


HI = lax.Precision.HIGHEST


def _conv_taps(x, w_ref, s_len):
    row = lax.broadcasted_iota(jnp.int32, x.shape, 0)
    c = w_ref[CONV_WIDTH - 1:CONV_WIDTH, :] * x
    for j in range(1, CONV_WIDTH):
        xs = jnp.where(row >= j, pltpu.roll(x, j, 0), 0.0)
        c = c + w_ref[CONV_WIDTH - 1 - j:CONV_WIDTH - j, :] * xs
    return c


def _conv_fwd(proj, conv8, col0, width, s_len, name):
    t = proj.shape[0]
    cb = _pick(width, (512, 256, 128))
    c0 = col0 // cb

    def body(x_ref, w_ref, o_ref):
        c = _conv_taps(x_ref[...], w_ref, s_len)
        o_ref[...] = c * _sigmoid(c)

    return pl.pallas_call(
        body, name=name, grid=(t // s_len, width // cb),
        in_specs=[pl.BlockSpec((s_len, cb), lambda b, j: (b, c0 + j)), pl.BlockSpec((8, cb), lambda b, j: (0, j))],
        out_specs=pl.BlockSpec((s_len, cb), lambda b, j: (b, j)),
        out_shape=jax.ShapeDtypeStruct((t, width), F32), compiler_params=_params())(proj, conv8)


def _conv_bwd(proj, conv8, du, col0, width, s_len, name):
    t = proj.shape[0]
    cb = _pick(width, (512, 256, 128))
    c0 = col0 // cb

    def body(x_ref, w_ref, du_ref, dx_ref, dw_ref):
        b = pl.program_id(1)
        x = x_ref[...]
        c = _conv_taps(x, w_ref, s_len)
        sg = _sigmoid(c)
        dc = du_ref[...] * (sg * (1.0 + c * (1.0 - sg)))
        row = lax.broadcasted_iota(jnp.int32, x.shape, 0)
        dx = w_ref[CONV_WIDTH - 1:CONV_WIDTH, :] * dc
        rows = [jnp.sum(dc * x, axis=0, keepdims=True)]
        for j in range(1, CONV_WIDTH):
            up = jnp.where(row < s_len - j, pltpu.roll(dc, s_len - j, 0), 0.0)
            dx = dx + w_ref[CONV_WIDTH - 1 - j:CONV_WIDTH - j, :] * up
            xs = jnp.where(row >= j, pltpu.roll(x, j, 0), 0.0)
            rows.append(jnp.sum(dc * xs, axis=0, keepdims=True))
        dx_ref[...] = dx.astype(BF16)
        part = jnp.concatenate(rows[::-1] + [jnp.zeros((8 - CONV_WIDTH, cb), F32)], axis=0)

        @pl.when(b == 0)
        def _():
            dw_ref[...] = part

        @pl.when(b > 0)
        def _():
            dw_ref[...] += part

    return pl.pallas_call(
        body, name=name, grid=(width // cb, t // s_len),
        in_specs=[pl.BlockSpec((s_len, cb), lambda j, b: (b, c0 + j)), pl.BlockSpec((8, cb), lambda j, b: (0, j)),
                  pl.BlockSpec((s_len, cb), lambda j, b: (b, j))],
        out_specs=[pl.BlockSpec((s_len, cb), lambda j, b: (b, j)), pl.BlockSpec((8, cb), lambda j, b: (0, j))],
        out_shape=[jax.ShapeDtypeStruct((t, width), BF16), jax.ShapeDtypeStruct((8, width), F32)],
        compiler_params=_params())(proj, conv8, du)


def _bmm(a, b, prec=None):
    return lax.dot_general(a, b, (((2,), (1,)), ((0,), (0,))), precision=prec, preferred_element_type=F32)


def _bmm_nt(a, b):
    return lax.dot_general(a.astype(BF16), b.astype(BF16), (((2,), (2,)), ((0,), (0,))), preferred_element_type=F32)


def _unit_lower_inverse(nm):
    c = nm.shape[-1]
    eye = (lax.broadcasted_iota(jnp.int32, (c, c), 0) == lax.broadcasted_iota(jnp.int32, (c, c), 1)).astype(F32)
    x = -nm
    inv = eye[None] + x
    p = x
    for _ in range(int(math.log2(c)) - 1):
        p = _bmm(p, p, HI)
        inv = inv + _bmm(inv, p, HI)
    return inv


def _dn_chunk_terms(uq, uk, uv, a_col, b_col, alog, dtb):
    n, c, dh = uq.shape
    q = uq * lax.rsqrt(jnp.sum(uq * uq, axis=-1, keepdims=True) + EPS) * (HEAD_DIM ** -0.5)
    k = uk * lax.rsqrt(jnp.sum(uk * uk, axis=-1, keepdims=True) + EPS)
    beta = _sigmoid(b_col)
    xa = a_col + dtb
    g = -jnp.exp(alog) * (jnp.maximum(xa, 0.0) + jnp.log(1.0 + jnp.exp(-jnp.abs(xa))))
    ri = lax.broadcasted_iota(jnp.int32, (c, c), 0)
    ci = lax.broadcasted_iota(jnp.int32, (c, c), 1)
    incl = ri >= ci
    strict = ri > ci
    l_incl = jnp.broadcast_to(incl.astype(F32)[None], (n, c, c))
    gb = jnp.broadcast_to(g, (n, c, dh))
    gc = _bmm(l_incl, gb, HI)
    gdiff = _bmm(l_incl, jnp.broadcast_to(g, (n, c, c)) * strict.astype(F32)[None], HI)
    decay = jnp.where(incl[None], jnp.exp(jnp.where(incl[None], gdiff, 0.0)), 0.0)
    gtot = _bmm(jnp.ones((n, c, c), F32), gb, HI)
    kb = k * beta
    nm = jnp.where(strict[None], _bmm_nt(kb, k) * decay, 0.0)
    tinv = _unit_lower_inverse(nm)
    w = _bmm(tinv, kb * jnp.exp(gc), HI)
    u = _bmm(tinv, uv * beta, HI)
    qk = _bmm_nt(q, k) * decay
    q_dec = q * jnp.exp(gc)
    k_dec = k * jnp.exp(gtot - gc)
    g_last = jnp.exp(_bmm(jnp.ones((n, dh, c), F32), gb, HI))
    return w, u, qk, q_dec, k_dec, g_last


DN_SUB = 8


def _dn_gather_inputs(uq_ref, uk_ref, uv_ref, ba_ref, prm_ref, h, nh, rows, nb):
    ba = ba_ref[rows, :]
    lane = lax.broadcasted_iota(jnp.int32, ba.shape, 1)
    b_col = jnp.sum(jnp.where(lane == h, ba, 0.0), axis=-1, keepdims=True).reshape(nb, CHUNK, 1)
    a_col = jnp.sum(jnp.where(lane == nh + h, ba, 0.0), axis=-1, keepdims=True).reshape(nb, CHUNK, 1)
    lane1 = lax.broadcasted_iota(jnp.int32, (1, LANE), 1)
    alog = jnp.sum(jnp.where(lane1 == h, prm_ref[1:2, :], 0.0), axis=-1, keepdims=True)
    dtb = jnp.sum(jnp.where(lane1 == h, prm_ref[2:3, :], 0.0), axis=-1, keepdims=True)
    shp = (nb, CHUNK, HEAD_DIM)
    return (uq_ref[rows, :].reshape(shp), uk_ref[rows, :].reshape(shp), uv_ref[rows, :].reshape(shp),
            a_col, b_col, alog, dtb)


def _dn_fill_terms(in_refs, h, nh, n, term_refs):
    nb = min(DN_SUB, n)

    def sub(i, carry):
        rows = pl.ds(pl.multiple_of(i * (nb * CHUNK), nb * CHUNK), nb * CHUNK)
        terms = _dn_chunk_terms(*_dn_gather_inputs(*in_refs, h, nh, rows, nb))
        for r, v in zip(term_refs, terms):
            r[pl.ds(i * nb, nb)] = v
        return carry

    lax.fori_loop(0, n // nb, sub, 0)


def _dn_scan(terms_refs, o_ref, st_ref, n):
    w_ref, u_ref, qk_ref, qd_ref, kd_ref, gl_ref = terms_refs

    def step(i, state):
        if st_ref is not None:
            st_ref[i] = state
        sb = state.astype(BF16)
        v_new = u_ref[i] - jnp.dot(w_ref[i].astype(BF16), sb, preferred_element_type=F32)
        vb = v_new.astype(BF16)
        o_ref[i] = (jnp.dot(qd_ref[i].astype(BF16), sb, preferred_element_type=F32)
                    + jnp.dot(qk_ref[i].astype(BF16), vb, preferred_element_type=F32))
        return state * gl_ref[i] + lax.dot_general(kd_ref[i].astype(BF16), vb, TN, preferred_element_type=F32)

    lax.fori_loop(0, n, step, jnp.zeros((HEAD_DIM, HEAD_DIM), F32))


def _dn_specs(nh, nh_a, s_len, zc0, bac):
    head = lambda off: pl.BlockSpec((s_len, HEAD_DIM), lambda b, h: (b, off + h))
    return dict(uq=head(0), uk=head(nh), uv=head(2 * nh), z=head(zc0),
                ba=pl.BlockSpec((s_len, LANE), lambda b, h: (b, bac)),
                prm=pl.BlockSpec((8, LANE), lambda b, h: (0, 0)), dout=head(nh_a), out=head(0))


def _dn_scratch(n, with_states):
    big = pltpu.VMEM((n, CHUNK, HEAD_DIM), F32)
    sc = [big, big, pltpu.VMEM((n, CHUNK, CHUNK), F32), big, big, pltpu.VMEM((n, HEAD_DIM, HEAD_DIM), F32), big]
    if with_states:
        sc.append(pltpu.VMEM((n, HEAD_DIM, HEAD_DIM), F32))
    return sc


def _dn_fwd(u, proj, prm, nh, s_len, zc0, bac, name):
    t = u.shape[0]
    n = s_len // CHUNK
    sp = _dn_specs(nh, 0, s_len, zc0, bac)

    def body(uq_ref, uk_ref, uv_ref, z_ref, ba_ref, prm_ref, o_ref, *scr):
        h = pl.program_id(1)
        _dn_fill_terms((uq_ref, uk_ref, uv_ref, ba_ref, prm_ref), h, nh, n, scr[:6])
        _dn_scan(scr[:6], scr[6], None, n)
        o = scr[6][...].reshape(s_len, HEAD_DIM)
        z = z_ref[...]
        r = lax.rsqrt(jnp.mean(o * o, axis=-1, keepdims=True) + EPS)
        o_ref[...] = o * r * prm_ref[0:1, :] * (z * _sigmoid(z))

    return pl.pallas_call(
        body, name=name, grid=(t // s_len, nh),
        in_specs=[sp["uq"], sp["uk"], sp["uv"], sp["z"], sp["ba"], sp["prm"]], out_specs=sp["out"],
        out_shape=jax.ShapeDtypeStruct((t, nh * HEAD_DIM), F32), scratch_shapes=_dn_scratch(n, False),
        compiler_params=_params())(u, u, u, proj, proj, prm)


def _dn_bwd(u, proj, prm, dcat, nh, nh_a, s_len, zc0, bac, name):
    t = u.shape[0]
    n = s_len // CHUNK
    sp = _dn_specs(nh, nh_a, s_len, zc0, bac)

    def body(uq_ref, uk_ref, uv_ref, z_ref, ba_ref, prm_ref, do_ref,
             duq_ref, duk_ref, duv_ref, dz_ref, dba_ref, dprm_ref, *scr):
        b, h = pl.program_id(0), pl.program_id(1)
        in_refs = (uq_ref, uk_ref, uv_ref, ba_ref, prm_ref)
        w_ref, u_ref, qk_ref, qd_ref, kd_ref, gl_ref, o_scr, st_ref = scr
        _dn_fill_terms(in_refs, h, nh, n, scr[:6])
        _dn_scan(scr[:6], o_scr, st_ref, n)

        o = o_scr[...].reshape(s_len, HEAD_DIM)
        z = z_ref[...]
        dout = do_ref[...].astype(F32)
        gain = prm_ref[0:1, :]
        sg = _sigmoid(z)
        sz = z * sg
        r = lax.rsqrt(jnp.mean(o * o, axis=-1, keepdims=True) + EPS)
        oh = o * r
        dgain = jnp.sum(dout * oh * sz, axis=0, keepdims=True)
        dz_ref[...] = (dout * oh * gain * (sg * (1.0 + z * (1.0 - sg)))).astype(BF16)
        doh = dout * gain * sz
        d_o = r * (doh - oh * jnp.mean(doh * oh, axis=-1, keepdims=True))
        o_scr[...] = d_o.reshape(n, CHUNK, HEAD_DIM)

        def step(j, ds):
            i = n - 1 - j
            st = st_ref[i]
            sb = st.astype(BF16)
            wi, qki, qdi, kdi, gli = w_ref[i], qk_ref[i], qd_ref[i], kd_ref[i], gl_ref[i]
            v_new = u_ref[i] - jnp.dot(wi.astype(BF16), sb, preferred_element_type=F32)
            vb = v_new.astype(BF16)
            don = o_scr[i].astype(BF16)
            dsb = ds.astype(BF16)
            dv = (lax.dot_general(qki.astype(BF16), don, TN, preferred_element_type=F32)
                  + jnp.dot(kdi.astype(BF16), dsb, preferred_element_type=F32))
            dvb = dv.astype(BF16)
            qd_ref[i] = lax.dot_general(don, sb, NT, preferred_element_type=F32)
            qk_ref[i] = lax.dot_general(don, vb, NT, preferred_element_type=F32)
            kd_ref[i] = lax.dot_general(vb, dsb, NT, preferred_element_type=F32)
            gl_ref[i] = ds * st
            u_ref[i] = dv
            w_ref[i] = -lax.dot_general(dvb, sb, NT, preferred_element_type=F32)
            return (ds * gli + lax.dot_general(qdi.astype(BF16), don, TN, preferred_element_type=F32)
                    - lax.dot_general(wi.astype(BF16), dvb, TN, preferred_element_type=F32))

        lax.fori_loop(0, n, step, jnp.zeros((HEAD_DIM, HEAD_DIM), F32))

        @pl.when(h == 0)
        def _():
            dba_ref[...] = jnp.zeros(dba_ref.shape, F32)

        nb = min(DN_SUB, n)

        def sub(i, carry):
            rows = pl.ds(pl.multiple_of(i * (nb * CHUNK), nb * CHUNK), nb * CHUNK)
            _, pull = jax.vjp(_dn_chunk_terms, *_dn_gather_inputs(*in_refs, h, nh, rows, nb))
            duq, duk, duv, da_col, db_col, dal, ddt = pull(tuple(r[pl.ds(i * nb, nb)] for r in scr[:6]))
            duq_ref[rows, :] = duq.reshape(nb * CHUNK, HEAD_DIM)
            duk_ref[rows, :] = duk.reshape(nb * CHUNK, HEAD_DIM)
            duv_ref[rows, :] = duv.reshape(nb * CHUNK, HEAD_DIM)
            lane = lax.broadcasted_iota(jnp.int32, (nb * CHUNK, LANE), 1)
            dba_ref[rows, :] += (jnp.where(lane == h, db_col.reshape(nb * CHUNK, 1), 0.0)
                                 + jnp.where(lane == nh + h, da_col.reshape(nb * CHUNK, 1), 0.0))
            return carry[0] + dal, carry[1] + ddt

        dalog, ddtb = lax.fori_loop(0, n // nb, sub, (jnp.zeros((1, 1), F32), jnp.zeros((1, 1), F32)))
        lane1 = lax.broadcasted_iota(jnp.int32, (1, LANE), 1)
        dprm = jnp.concatenate([dgain, jnp.where(lane1 == h, dalog, 0.0), jnp.where(lane1 == h, ddtb, 0.0),
                                jnp.zeros((5, LANE), F32)], axis=0)

        @pl.when((b == 0) & (h == 0))
        def _():
            dprm_ref[...] = dprm

        @pl.when((b > 0) | (h > 0))
        def _():
            dprm_ref[...] += dprm

    osd = jax.ShapeDtypeStruct((t, nh * HEAD_DIM), F32)
    return pl.pallas_call(
        body, name=name, grid=(t // s_len, nh),
        in_specs=[sp["uq"], sp["uk"], sp["uv"], sp["z"], sp["ba"], sp["prm"], sp["dout"]],
        out_specs=[sp["out"], sp["out"], sp["out"], sp["out"], pl.BlockSpec((s_len, LANE), lambda b, h: (b, 0)),
                   pl.BlockSpec((8, LANE), lambda b, h: (0, 0))],
        out_shape=[osd, osd, osd, jax.ShapeDtypeStruct((t, nh * HEAD_DIM), BF16),
                   jax.ShapeDtypeStruct((t, LANE), F32), jax.ShapeDtypeStruct((8, LANE), F32)],
        scratch_shapes=_dn_scratch(n, True), compiler_params=_params())(u, u, u, proj, proj, prm, dcat)


def _w_in_windows(ws):
    w0 = [(ws * k) // LANE * LANE for k in range(N_CHIPS)]
    sh = [ws * k - w0[k] for k in range(N_CHIPS)]
    ww = _ceil_to(max(sh) + ws, 2 * LANE)
    n_tiles = (w0[-1] + ww) // LANE
    tpw = ww // LANE
    tph = tpw // 2
    tab = np.zeros((7, n_tiles), np.int32)
    for t in range(n_tiles):
        ks = [k for k in range(N_CHIPS) if w0[k] // LANE <= t < w0[k] // LANE + tpw]
        k1 = ks[-1]
        lt = t - w0[k1] // LANE
        tab[0, t], tab[1, t], tab[2, t] = lt // tph, k1, lt % tph
        k2 = ks[0] if len(ks) > 1 else k1
        lt2 = t - w0[k2] // LANE
        tab[3, t], tab[4, t], tab[5, t] = lt2 // tph, k2, lt2 % tph
        tab[6, t] = 1 if len(ks) > 1 else 0
        assert len(ks) <= 2
    return w0, sh, ww, n_tiles, tab


def kernel(x, ffn1_norm, ffn1_w_gate, ffn1_w_up, ffn1_w_down, mix_norm, w_in, conv_w, a_log, dt_bias, dn_norm, w_out, ffn2_norm, ffn2_w_gate, ffn2_w_up, ffn2_w_down, final_norm, loss_target, m_ffn1_norm, m_ffn1_w_gate, m_ffn1_w_up, m_ffn1_w_down, m_mix_norm, m_w_in, m_conv_w, m_a_log, m_dt_bias, m_dn_norm, m_w_out, m_ffn2_norm, m_ffn2_w_gate, m_ffn2_w_up, m_ffn2_w_down, m_final_norm, v_ffn1_norm, v_ffn1_w_gate, v_ffn1_w_up, v_ffn1_w_down, v_mix_norm, v_w_in, v_conv_w, v_a_log, v_dt_bias, v_dn_norm, v_w_out, v_ffn2_norm, v_ffn2_w_gate, v_ffn2_w_up, v_ffn2_w_down, v_final_norm):
    bl, s_, d = x.shape
    t = bl * s_
    fs = ffn1_w_gate.shape[1]
    hp = _ceil_to(-(-fs // 2), LANE)
    ws = w_in.shape[1]
    d_mix = w_out.shape[0] * N_CHIPS
    d_attn = d_dn = d_mix // 2
    nh_d = d_dn // HEAD_DIM
    d_in = 3 * d_attn + 4 * d_dn + 2 * nh_d
    cs = conv_w.shape[1]
    assert ws * N_CHIPS == d_in and cs * N_CHIPS == 3 * d_dn

    xi, yi, ci = lax.axis_index("x"), lax.axis_index("y"), lax.axis_index("c")
    me = 2 * xi + yi
    cidx = jnp.reshape(ci, (1,)).astype(jnp.int32)
    meidx = jnp.reshape(me, (1,)).astype(jnp.int32)
    mcidx = jnp.stack([me, ci]).astype(jnp.int32)

    w0, sh, ww, n_tiles, tab = _w_in_windows(ws)
    shift = (ws * me) % LANE
    w_in_win = lax.dynamic_update_slice(jnp.zeros((d, ww), F32), w_in, (jnp.int32(0), shift))
    rows_tr = math.gcd(hp, fs)
    conv_piece = jnp.pad(conv_w, ((0, 8 - CONV_WIDTH), (0, 0))).reshape(8, 2, cs // 2).transpose(1, 0, 2)
    z0 = jnp.int32(0)
    pieces = [
        _cast_split_cols(ffn1_w_gate, hp, meidx, "cast_g1"),
        _cast_split_cols(ffn1_w_up, hp, meidx, "cast_u1"),
        _cast_split_rows(ffn1_w_down, hp, rows_tr, meidx, "cast_d1"),
        _cast_split_cols(w_in_win, ww // 2, meidx, "cast_in"),
        _cast_split_rows(w_out, w_out.shape[0] // 2, w_out.shape[0] // 2, meidx, "cast_out"),
        _cast_split_cols(ffn2_w_gate, hp, meidx, "cast_g2"),
        _cast_split_cols(ffn2_w_up, hp, meidx, "cast_u2"),
        _cast_split_rows(ffn2_w_down, hp, rows_tr, meidx, "cast_d2"),
        lax.dynamic_update_slice(jnp.zeros((2, N_CHIPS, 8, cs // 2), F32), conv_piece[:, None], (z0, me, z0, z0)),
    ]
    p_g1, p_u1, p_d1, p_in, p_out, p_g2, p_u2, p_d2, p_conv = pieces
    npc = 8
    ident = lambda p: p
    cat_map = lambda p: 2 * (p % N_CHIPS) + p // N_CHIPS
    as_cols = lambda a: a.reshape(npc, d, hp)
    as_rows = lambda a: a.reshape(npc, hp, d)

    wg1, wu1 = _comm_call(_all_gather_comm([p_g1, p_u1]), "all_gather_first")
    wg1, wu1 = as_cols(wg1), as_cols(wu1)
    h0 = x.reshape(t, d)
    n1 = _rms_fwd(h0, ffn1_norm, "rms1")
    (a1, b1, s1), (wd1, win_all) = _ffn_up(n1, wg1, wu1, "ffn1_up", comm=_all_gather_comm([p_d1, p_in]))
    wd1 = as_rows(wd1)
    h1, (wout, wg2) = _mm_pieces_resid(s1, wd1, h0, 0.5, ident, "ffn1_down", comm=_all_gather_comm([p_out, p_g2]))
    wout, wg2 = wout.reshape(npc, w_out.shape[0] // 2, d), as_cols(wg2)
    win_full = _combine_windows(win_all, jnp.asarray(tab), n_tiles, "combine_w_in")
    n2 = _rms_fwd(h1, mix_norm, "rms2")
    proj, (wu2, wd2, conv_all) = _mm2d(n2, win_full, NN, F32, "in_proj", (1024, 768, 4096),
                                       comm=_all_gather_comm([p_u2, p_d2, p_conv]))
    wu2, wd2 = as_cols(wu2), as_rows(wd2)
    conv8 = conv_all.transpose(2, 1, 0, 3).reshape(8, 3 * d_dn)
    nh_a = d_attn // HEAD_DIM
    attn = _attn_fwd(proj, nh_a, s_, "attn_fwd")
    zc0 = (3 * d_attn + 3 * d_dn) // HEAD_DIM
    bac = (3 * d_attn + 4 * d_dn) // LANE
    row128 = lambda v: jnp.pad(v, (0, LANE - v.shape[0])).reshape(1, LANE)
    prm = jnp.concatenate([row128(dn_norm), row128(a_log), row128(dt_bias), jnp.zeros((5, LANE), F32)], axis=0)
    u_dn = _conv_fwd(proj, conv8, 3 * d_attn, 3 * d_dn, s_, "dn_conv")
    dn_out = _dn_fwd(u_dn, proj, prm, nh_d, s_, zc0, bac, "dn_fwd")
    cat_b = jnp.concatenate([attn, dn_out], axis=1).astype(BF16)
    h2 = _mm_pieces_resid(cat_b, wout, h1, 1.0, cat_map, "out_proj")
    n3 = _rms_fwd(h2, ffn2_norm, "rms3")
    a3, b3, s3 = _ffn_up(n3, wg2, wu2, "ffn2_up")
    h3 = _mm_pieces_resid(s3, wd2, h2, 0.5, ident, "ffn2_down")

    def rs_front(gs, tag):
        gs = [g.reshape((2, N_CHIPS * g.shape[-2], g.shape[-1])) for g in gs]
        from_sib = _sibling_take(gs, f"rs_sibling_take_{tag}")
        ps = [_pair_sum(g, l, cidx, f"rs_pair_sum_{tag}_{i}") for i, (g, l) in enumerate(zip(gs, from_sib))]
        return [p.reshape(N_CHIPS, p.shape[0] // N_CHIPS, p.shape[1]) for p in ps]

    dh3, dh3b, dwf_p, lsq_p = _final_loss(h3, final_norm, loss_target.reshape(t, d), "final_loss")
    da3, db3 = _ffn_bwd_hidden(dh3b, wd2, a3, b3, "ffn2_bwd_hidden")
    g_wd2 = _grad_rows_pieces(s3, dh3b, 0.5, ident, npc, "ffn2_grad_down")
    p_d2s = rs_front([g_wd2], "d2")
    (g_wg2, g_wu2), l_d2 = _grad_cols_pieces(n3, da3, db3, npc, "ffn2_grad_up", comm=_chip_all_to_all_comm(p_d2s))
    p_gu2 = rs_front([g_wg2, g_wu2], "gu2")
    dn3, l_gu2 = _ffn_bwd_input(da3, db3, wg2, wu2, "ffn2_bwd_input", comm=_chip_all_to_all_comm(p_gu2))
    dh2, dh2b, dw3_p = _rms_bwd(dn3, h2, ffn2_norm, dh3, "rms3_bwd")

    dcat = _mm_nt_pieces_out(dh2b, wout, cat_map, "out_proj_bwd")
    g_wout = _grad_rows_pieces(cat_b, dh2b, 1.0, cat_map, npc, "out_proj_grad")
    dq_a, dk_a, dv_a = _attn_bwd(proj, attn, dcat, nh_a, s_, "attn_bwd")
    duq, duk, duv, dz, dba, dprm = _dn_bwd(u_dn, proj, prm, dcat, nh_d, nh_a, s_, zc0, bac, "dn_bwd")
    dx_conv, dconv8 = _conv_bwd(proj, conv8, jnp.concatenate([duq, duk, duv], axis=1), 3 * d_attn, 3 * d_dn, s_,
                                "dn_conv_bwd")
    used = 3 * d_attn + 4 * d_dn + LANE
    dproj_b = jnp.concatenate([dq_a, dk_a, dv_a, dx_conv, dz, dba.astype(BF16),
                               jnp.zeros((t, proj.shape[1] - used), BF16)], axis=1)
    dconv, ddnn, dalog, ddtb = dconv8[:CONV_WIDTH], dprm[0, :dn_norm.shape[0]], dprm[1, :nh_d], dprm[2, :nh_d]
    g_win_full = _mm2d(n2, dproj_b, TN, BF16, "in_proj_grad", (1024, 2432, 512))
    wh = ww // 2
    g_win = jnp.stack([jnp.stack([g_win_full[:, w0[k] + wh * h: w0[k] + wh * (h + 1)] for k in range(N_CHIPS)])
                       for h in range(2)])
    p_ow = rs_front([g_wout, g_win], "ow")
    dn2, l_ow = _mm2d(dproj_b, win_full, NT, F32, "in_proj_bwd", (1024, 2048, 768), comm=_chip_all_to_all_comm(p_ow))
    dh1, dh1b, dwm_p = _rms_bwd(dn2, h1, mix_norm, dh2, "rms2_bwd")

    da1, db1 = _ffn_bwd_hidden(dh1b, wd1, a1, b1, "ffn1_bwd_hidden")
    g_wd1 = _grad_rows_pieces(s1, dh1b, 0.5, ident, npc, "ffn1_grad_down")
    p_d1s = rs_front([g_wd1], "d1")
    (g_wg1, g_wu1), l_d1 = _grad_cols_pieces(n1, da1, db1, npc, "ffn1_grad_up", comm=_chip_all_to_all_comm(p_d1s))
    p_gu1 = rs_front([g_wg1, g_wu1], "gu1")
    dn1, l_gu1 = _ffn_bwd_input(da1, db1, wg1, wu1, "ffn1_bwd_input", comm=_chip_all_to_all_comm(p_gu1))
    dh0, _, dw1_p = _rms_bwd(dn1, h0, ffn1_norm, dh1, "rms1_bwd")
    grad_x = dh0.reshape(bl, s_, d)

    pair = [p_gu1[0], p_gu1[1], p_d1s[0], p_ow[1], p_ow[0], p_gu2[0], p_gu2[1], p_d2s[0]]
    from_chips = [l_gu1[0], l_gu1[1], l_d1[0], l_ow[1], l_ow[0], l_gu2[0], l_gu2[1], l_d2[0]]
    halves = [_chip_sum(p, l, mcidx, f"rs_chip_sum_{i}") for i, (p, l) in enumerate(zip(pair, from_chips))]
    full = _sibling_join(halves, "rs_sibling_join")
    f_wg1, f_wu1, f_wd1, f_win, f_wout, f_wg2, f_wu2, f_wd2 = full

    unpad_cols = lambda f: jnp.concatenate([f[0], f[1][:, :fs - hp]], axis=1)
    unpad_rows = lambda f: f.reshape(2 * f.shape[1], f.shape[2])[:fs]
    gw = {
        "ffn1_w_gate": unpad_cols(f_wg1), "ffn1_w_up": unpad_cols(f_wu1), "ffn1_w_down": unpad_rows(f_wd1),
        "w_in": lax.dynamic_slice(jnp.concatenate([f_win[0], f_win[1]], axis=1), (jnp.int32(0), shift), (d, ws)),
        "w_out": f_wout.reshape(w_out.shape),
        "ffn2_w_gate": unpad_cols(f_wg2), "ffn2_w_up": unpad_cols(f_wu2), "ffn2_w_down": unpad_rows(f_wd2),
    }

    def lanes(v):
        v = v.reshape(-1)
        return jnp.pad(v, (0, _ceil_to(v.shape[0], LANE) - v.shape[0])).reshape(-1, LANE)

    small = [dw1_p.sum(0), dwm_p.sum(0), dw3_p.sum(0), dwf_p.sum(0), ddnn, dalog, ddtb,
             (0.5 / d) * jnp.sum(lsq_p).reshape(1), dconv]
    rows = [lanes(v) for v in small]
    offs = np.cumsum([0] + [r.shape[0] for r in rows])
    packed = jnp.concatenate(rows, axis=0)
    packed = jnp.pad(packed, ((0, _ceil_to(packed.shape[0], 8) - packed.shape[0]), (0, 0)))
    red = _allreduce_small(packed, "allreduce_small")
    take = lambda i, shape: red[offs[i]:offs[i + 1]].reshape(-1)[:int(np.prod(shape))].reshape(shape)
    gw["ffn1_norm"] = take(0, (d,))
    gw["mix_norm"] = take(1, (d,))
    gw["ffn2_norm"] = take(2, (d,))
    gw["final_norm"] = take(3, (d,))
    gw["dn_norm"] = take(4, dn_norm.shape)
    gw["a_log"] = take(5, a_log.shape)
    gw["dt_bias"] = take(6, dt_bias.shape)
    loss = take(7, (1,)).reshape(())
    gw["conv_w"] = lax.dynamic_slice(take(8, (CONV_WIDTH, 3 * d_dn)), (jnp.int32(0), me * cs), (CONV_WIDTH, cs))

    names = ['ffn1_norm', 'ffn1_w_gate', 'ffn1_w_up', 'ffn1_w_down', 'mix_norm', 'w_in', 'conv_w', 'a_log', 'dt_bias',
             'dn_norm', 'w_out', 'ffn2_norm', 'ffn2_w_gate', 'ffn2_w_up', 'ffn2_w_down', 'final_norm']
    wv = dict(zip(names, (ffn1_norm, ffn1_w_gate, ffn1_w_up, ffn1_w_down, mix_norm, w_in, conv_w, a_log, dt_bias,
                          dn_norm, w_out, ffn2_norm, ffn2_w_gate, ffn2_w_up, ffn2_w_down, final_norm)))
    mv = dict(zip(names, (m_ffn1_norm, m_ffn1_w_gate, m_ffn1_w_up, m_ffn1_w_down, m_mix_norm, m_w_in, m_conv_w, m_a_log,
                          m_dt_bias, m_dn_norm, m_w_out, m_ffn2_norm, m_ffn2_w_gate, m_ffn2_w_up, m_ffn2_w_down,
                          m_final_norm)))
    vv = dict(zip(names, (v_ffn1_norm, v_ffn1_w_gate, v_ffn1_w_up, v_ffn1_w_down, v_mix_norm, v_w_in, v_conv_w, v_a_log,
                          v_dt_bias, v_dn_norm, v_w_out, v_ffn2_norm, v_ffn2_w_gate, v_ffn2_w_up, v_ffn2_w_down,
                          v_final_norm)))
    delta, new_m, new_v = {}, {}, {}
    small_names = [n for n in names if wv[n].ndim == 1 or n == "conv_w"]
    for n in names:
        if n in small_names:
            continue
        delta[n], new_m[n], new_v[n] = _adamw(gw[n], wv[n], mv[n], vv[n], f"adamw_{n}")
    srows = {n: lanes(gw[n]).shape[0] for n in small_names}
    soffs = np.cumsum([0] + [srows[n] for n in small_names])
    stot = _ceil_to(int(soffs[-1]), 8)

    def pack(dct):
        p = jnp.concatenate([lanes(dct[n]) for n in small_names], axis=0)
        return jnp.pad(p, ((0, stot - p.shape[0]), (0, 0)))

    sd, sm, sv = _adamw(pack(gw), pack(wv), pack(mv), pack(vv), "adamw_small")
    for i, n in enumerate(small_names):
        cut = lambda p: p[soffs[i]:soffs[i + 1]].reshape(-1)[:wv[n].size].reshape(wv[n].shape)
        delta[n], new_m[n], new_v[n] = cut(sd), cut(sm), cut(sv)

    return (loss, grad_x, *[gw[n] for n in names], *[delta[n] for n in names], *[new_m[n] for n in names],
            *[new_v[n] for n in names])
```

```python
import functools
import math

import jax
import jax.numpy as jnp
import numpy as np
from jax import lax
from jax.experimental import pallas as pl
from jax.experimental.pallas import tpu as pltpu

F32 = jnp.float32
BF16 = jnp.bfloat16
MESH = pl.DeviceIdType.MESH
ANY = pl.BlockSpec(memory_space=pl.ANY)

LANE = 128
N_CHIPS = 4
N_DEV = 8
EPS = 1e-6
HEAD_DIM = 128
CONV_WIDTH = 4
CHUNK = 64
ATTN_BLOCK = 128
DILATED_CONFIGS = ((128, 1), (512, 4), (2048, 16))
VMEM_LIMIT = 52 * 1024 * 1024

ADAM_LR = 0.001
ADAM_B1 = 0.9
ADAM_B2 = 0.999
ADAM_EPS = 1e-08
ADAM_WD = 0.01
ADAM_STEP = 10

NN = (((1,), (0,)), ((), ()))
NT = (((1,), (1,)), ((), ()))
TN = (((0,), (0,)), ((), ()))


def _ceil_to(v, m):
    return -(-v // m) * m


def _params(vmem=VMEM_LIMIT):
    return pltpu.CompilerParams(vmem_limit_bytes=vmem)


class _Comm:
    def __init__(self, ins, out_shape, aliases, sems, start, mid, finish):
        self.ins, self.out_shape, self.aliases, self.sems = ins, out_shape, aliases, sems
        self.start, self.mid, self.finish = start, mid, finish


def _gemm(name, grid, pairs, dn, acc_shape, n_acc, extras, outs, epilogue, comm=None):
    n_pairs, n_ex, n_out = len(pairs), len(extras), len(outs)
    n_ci = len(comm.ins) if comm else 0
    n_co = len(comm.out_shape) if comm else 0
    n_sem = len(comm.sems) if comm else 0
    kax = len(grid) - 1
    nk = grid[kax]
    n_in = 2 * n_pairs + n_ex

    def body(*refs):
        ins = refs[: 2 * n_pairs]
        ex = refs[2 * n_pairs: n_in]
        c_in = refs[n_in: n_in + n_ci]
        out_refs = refs[n_in + n_ci: n_in + n_ci + n_out]
        c_out = refs[n_in + n_ci + n_out: n_in + n_ci + n_out + n_co]
        accs = refs[n_in + n_ci + n_out + n_co: n_in + n_ci + n_out + n_co + n_acc]
        sems = refs[n_in + n_ci + n_out + n_co + n_acc:]
        k = pl.program_id(kax)
        pids = [pl.program_id(a) for a in range(len(grid))]

        def at(point):
            cond = pids[0] == point[0]
            for pid, v in zip(pids[1:], point[1:]):
                cond = cond & (pid == v)
            return cond

        if comm:
            @pl.when(at([0] * len(grid)))
            def _():
                comm.start(c_in, c_out, sems)

            if comm.mid:
                @pl.when(at([grid[0] * 3 // 4] + [0] * (len(grid) - 1)))
                def _():
                    comm.mid(c_in, c_out, sems)

        @pl.when(k == 0)
        def _():
            for acc in accs:
                acc[...] = jnp.zeros(acc.shape, F32)

        for q in range(n_pairs):
            a = ins[2 * q][...]
            b = ins[2 * q + 1][...]
            if a.dtype != BF16:
                a = a.astype(BF16)
            if b.dtype != BF16:
                b = b.astype(BF16)
            accs[pairs[q][4]][...] += lax.dot_general(a, b, dn, preferred_element_type=F32)

        @pl.when(k == nk - 1)
        def _():
            res = epilogue([acc[...] for acc in accs], [e[...] for e in ex])
            for o, r in zip(out_refs, res):
                o[...] = r.astype(o.dtype)

        if comm:
            @pl.when(at([g - 1 for g in grid]))
            def _():
                comm.finish(c_in, c_out, sems)

    in_specs = []
    args = []
    for a, a_spec, b, b_spec, _ in pairs:
        in_specs += [a_spec, b_spec]
        args += [a, b]
    for e, e_spec in extras:
        in_specs.append(e_spec)
        args.append(e)
    out_shape = [o for o, _ in outs]
    out_specs = [s for _, s in outs]
    scratch = [pltpu.VMEM(acc_shape, F32) for _ in range(n_acc)]
    kwargs = {}
    if comm:
        in_specs += [ANY] * n_ci
        args += list(comm.ins)
        out_shape += list(comm.out_shape)
        out_specs += [ANY] * n_co
        scratch += list(comm.sems)
        kwargs["input_output_aliases"] = {n_in + i: n_out + o for i, o in comm.aliases.items()}
    res = pl.pallas_call(body, name=name, grid=grid, in_specs=in_specs, out_specs=out_specs,
                         out_shape=out_shape, scratch_shapes=scratch, compiler_params=_params(), **kwargs)(*args)
    if comm:
        return list(res[:n_out]), list(res[n_out:])
    return res


def _unpack(res, comm, single):
    if comm:
        outs, couts = res
        return (outs[0] if single else outs), couts
    return res[0] if single else res


def _call_carrying(body, name, grid, in_specs, out_specs, out_shape, scratch, args, comm):
    n_in, n_out, n_scr = len(in_specs), len(out_shape), len(scratch)
    n_ci, n_co = len(comm.ins), len(comm.out_shape)

    def wrapped(*refs):
        ins, c_in = refs[:n_in], refs[n_in:n_in + n_ci]
        outs = refs[n_in + n_ci:n_in + n_ci + n_out]
        c_out = refs[n_in + n_ci + n_out:n_in + n_ci + n_out + n_co]
        scr = refs[n_in + n_ci + n_out + n_co:n_in + n_ci + n_out + n_co + n_scr]
        sems = refs[n_in + n_ci + n_out + n_co + n_scr:]
        pids = [pl.program_id(a) for a in range(len(grid))]

        def at(point):
            cond = pids[0] == point[0]
            for pid, v in zip(pids[1:], point[1:]):
                cond = cond & (pid == v)
            return cond

        @pl.when(at([0] * len(grid)))
        def _():
            comm.start(c_in, c_out, sems)

        if comm.mid:
            @pl.when(at([grid[0] * 3 // 4] + [0] * (len(grid) - 1)))
            def _():
                comm.mid(c_in, c_out, sems)

        body(*ins, *outs, *scr)

        @pl.when(at([g - 1 for g in grid]))
        def _():
            comm.finish(c_in, c_out, sems)

    res = pl.pallas_call(
        wrapped, name=name, grid=grid, in_specs=list(in_specs) + [ANY] * n_ci, out_specs=list(out_specs) + [ANY] * n_co,
        out_shape=list(out_shape) + list(comm.out_shape), scratch_shapes=list(scratch) + list(comm.sems),
        input_output_aliases={n_in + i: n_out + o for i, o in comm.aliases.items()},
        compiler_params=_params())(*args, *comm.ins)
    return list(res[:n_out]), list(res[n_out:])


def _pick(n, prefs):
    for p in prefs:
        if n % p == 0:
            return p
    return n


def _sigmoid(v):
    return 1.0 / (1.0 + jnp.exp(-v))


def _ffn_up(n, wg, wu, name, comm=None):
    t, d = n.shape
    npieces, _, hp = wg.shape
    tm = _pick(t, (512, 256, 128, 64, 32, 16))
    tk = _pick(d, (1024, 512, 256, 128))
    grid = (t // tm, npieces, d // tk)
    a_spec = pl.BlockSpec((tm, tk), lambda i, p, k: (i, k))
    w_spec = pl.BlockSpec((None, tk, hp), lambda i, p, k: (p, k, 0))
    o_spec = pl.BlockSpec((tm, hp), lambda i, p, k: (i, p))
    osd = jax.ShapeDtypeStruct((t, npieces * hp), BF16)

    def epi(accs, ex):
        a, b = accs
        return a, b, a * _sigmoid(a) * b

    return _unpack(_gemm(name, grid, [(n, a_spec, wg, w_spec, 0), (n, a_spec, wu, w_spec, 1)], NN, (tm, hp), 2, [],
                         [(osd, o_spec)] * 3, epi, comm), comm, False)


def _mm_pieces_resid(a, w, resid, scale, amap, name, comm=None):
    t = a.shape[0]
    npieces, kp, n = w.shape
    tm = _pick(t, (1024, 512, 256, 128, 64, 32, 16))
    tn = _pick(n, (1024, 512, 256, 128))
    grid = (t // tm, n // tn, npieces)
    a_spec = pl.BlockSpec((tm, kp), lambda i, j, p: (i, amap(p)))
    w_spec = pl.BlockSpec((None, kp, tn), lambda i, j, p: (p, 0, j))
    r_spec = pl.BlockSpec((tm, tn), lambda i, j, p: (i, j))

    def epi(accs, ex):
        return (ex[0] + scale * accs[0],)

    return _unpack(_gemm(name, grid, [(a, a_spec, w, w_spec, 0)], NN, (tm, tn), 1, [(resid, r_spec)],
                         [(jax.ShapeDtypeStruct((t, n), F32), r_spec)], epi, comm), comm, True)


def _ffn_bwd_hidden(dh, wd, a, b, name):
    t, d = dh.shape
    npieces, hp, _ = wd.shape
    tm = _pick(t, (512, 256, 128, 64, 32, 16))
    tk = _pick(d, (1024, 512, 256, 128))
    grid = (t // tm, npieces, d // tk)
    a_spec = pl.BlockSpec((tm, tk), lambda i, p, k: (i, k))
    w_spec = pl.BlockSpec((None, hp, tk), lambda i, p, k: (p, 0, k))
    o_spec = pl.BlockSpec((tm, hp), lambda i, p, k: (i, p))
    osd = jax.ShapeDtypeStruct((t, npieces * hp), BF16)

    def epi(accs, ex):
        ds = 0.5 * accs[0]
        av = ex[0].astype(F32)
        bv = ex[1].astype(F32)
        sg = _sigmoid(av)
        da = ds * bv * (sg * (1.0 + av * (1.0 - sg)))
        db = ds * (av * sg)
        return da, db

    return _gemm(name, grid, [(dh, a_spec, wd, w_spec, 0)], NT, (tm, hp), 1, [(a, o_spec), (b, o_spec)],
                 [(osd, o_spec)] * 2, epi)


def _mm_nt_pieces_out(dh, w, omap, name):
    t, d = dh.shape
    npieces, npp, _ = w.shape
    tm = _pick(t, (1024, 512, 256, 128, 64, 32, 16))
    tk = _pick(d, (1024, 512, 256, 128))
    grid = (t // tm, npieces, d // tk)
    a_spec = pl.BlockSpec((tm, tk), lambda i, p, k: (i, k))
    w_spec = pl.BlockSpec((None, npp, tk), lambda i, p, k: (p, 0, k))
    o_spec = pl.BlockSpec((tm, npp), lambda i, p, k: (i, omap(p)))
    return _gemm(name, grid, [(dh, a_spec, w, w_spec, 0)], NT, (tm, npp), 1, [],
                 [(jax.ShapeDtypeStruct((t, npieces * npp), BF16), o_spec)], lambda accs, ex: (accs[0],))[0]


def _grad_rows_pieces(x, dy, scale, amap, npieces, name):
    t, n = dy.shape
    mp = x.shape[1] // npieces
    tn = _pick(n, (1024, 512, 256, 128))
    tk = _pick(t, (1024, 512, 256, 128, 64, 32, 16))
    grid = (npieces, n // tn, t // tk)
    x_spec = pl.BlockSpec((tk, mp), lambda p, j, k: (k, amap(p)))
    y_spec = pl.BlockSpec((tk, tn), lambda p, j, k: (k, j))
    o_spec = pl.BlockSpec((None, mp, tn), lambda p, j, k: (p, 0, j))
    return _gemm(name, grid, [(x, x_spec, dy, y_spec, 0)], TN, (mp, tn), 1, [],
                 [(jax.ShapeDtypeStruct((npieces, mp, n), BF16), o_spec)], lambda accs, ex: (scale * accs[0],))[0]


def _grad_cols_pieces(n, da, db, npieces, name, comm=None):
    t, d = n.shape
    hp = da.shape[1] // npieces
    tm = _pick(d, (1024, 512, 256, 128))
    tk = _pick(t, (512, 256, 128, 64, 32, 16))
    grid = (npieces, d // tm, t // tk)
    n_spec = pl.BlockSpec((tk, tm), lambda p, i, k: (k, i))
    g_spec = pl.BlockSpec((tk, hp), lambda p, i, k: (k, p))
    o_spec = pl.BlockSpec((None, tm, hp), lambda p, i, k: (p, i, 0))
    osd = jax.ShapeDtypeStruct((npieces, d, hp), BF16)
    return _unpack(_gemm(name, grid, [(n, n_spec, da, g_spec, 0), (n, n_spec, db, g_spec, 1)], TN, (tm, hp), 2, [],
                         [(osd, o_spec)] * 2, lambda accs, ex: (accs[0], accs[1]), comm), comm, False)


def _ffn_bwd_input(da, db, wg, wu, name, comm=None):
    t = da.shape[0]
    npieces, d, hp = wg.shape
    tm = _pick(t, (1024, 512, 256, 128, 64, 32, 16))
    tn = _pick(d, (1024, 512, 256, 128))
    grid = (t // tm, d // tn, npieces)
    g_spec = pl.BlockSpec((tm, hp), lambda i, j, p: (i, p))
    w_spec = pl.BlockSpec((None, tn, hp), lambda i, j, p: (p, j, 0))
    o_spec = pl.BlockSpec((tm, tn), lambda i, j, p: (i, j))
    return _unpack(_gemm(name, grid, [(da, g_spec, wg, w_spec, 0), (db, g_spec, wu, w_spec, 0)], NT, (tm, tn), 1, [],
                         [(jax.ShapeDtypeStruct((t, d), F32), o_spec)], lambda accs, ex: (accs[0],), comm), comm, True)


def _mm2d(a, b, dn, out_dtype, name, tiles, comm=None):
    if dn == NN:
        m, kk = a.shape
        n = b.shape[1]
    elif dn == NT:
        m, kk = a.shape
        n = b.shape[0]
    else:
        kk, m = a.shape
        n = b.shape[1]
    tm = _pick(m, (tiles[0],) + (1024, 512, 256, 128, 64, 32, 16))
    tn = _pick(n, (tiles[1], 768, 1024, 512, 256, 128))
    tk = _pick(kk, (tiles[2], 768, 1024, 512, 256, 128, 64, 32, 16))
    grid = (m // tm, n // tn, kk // tk)
    if dn == TN:
        a_spec = pl.BlockSpec((tk, tm), lambda i, j, k: (k, i))
    else:
        a_spec = pl.BlockSpec((tm, tk), lambda i, j, k: (i, k))
    if dn == NT:
        b_spec = pl.BlockSpec((tn, tk), lambda i, j, k: (j, k))
    else:
        b_spec = pl.BlockSpec((tk, tn), lambda i, j, k: (k, j))
    o_spec = pl.BlockSpec((tm, tn), lambda i, j, k: (i, j))
    return _unpack(_gemm(name, grid, [(a, a_spec, b, b_spec, 0)], dn, (tm, tn), 1, [],
                         [(jax.ShapeDtypeStruct((m, n), out_dtype), o_spec)], lambda accs, ex: (accs[0],), comm), comm, True)


def _row_tile(t):
    return _pick(t, (256, 128, 64, 32, 16, 8))


def _rms_fwd(x, w, name):
    t, d = x.shape
    tm = _row_tile(t)

    def body(x_ref, w_ref, o_ref):
        xv = x_ref[...]
        r = lax.rsqrt(jnp.mean(xv * xv, axis=-1, keepdims=True) + EPS)
        o_ref[...] = (xv * r * w_ref[...]).astype(BF16)

    return pl.pallas_call(
        body, name=name, grid=(t // tm,),
        in_specs=[pl.BlockSpec((tm, d), lambda i: (i, 0)), pl.BlockSpec((1, d), lambda i: (0, 0))],
        out_specs=pl.BlockSpec((tm, d), lambda i: (i, 0)),
        out_shape=jax.ShapeDtypeStruct((t, d), BF16), compiler_params=_params())(x, w.reshape(1, d))


def _rms_bwd(dn, x, w, dres, name):
    t, d = x.shape
    tm = _row_tile(t)

    def body(dn_ref, x_ref, w_ref, r_ref, o_ref, ob_ref, dw_ref):
        i = pl.program_id(0)
        xv = x_ref[...]
        r = lax.rsqrt(jnp.mean(xv * xv, axis=-1, keepdims=True) + EPS)
        xh = xv * r
        dy = dn_ref[...].astype(F32)
        g = dy * w_ref[...]
        dx = r * (g - xh * jnp.mean(g * xh, axis=-1, keepdims=True))
        tot = r_ref[...] + dx
        o_ref[...] = tot
        ob_ref[...] = tot.astype(BF16)
        part = (dy * xh).reshape(tm // 8, 8, d).sum(axis=0)

        @pl.when(i == 0)
        def _():
            dw_ref[...] = part

        @pl.when(i > 0)
        def _():
            dw_ref[...] += part

    row = pl.BlockSpec((tm, d), lambda i: (i, 0))
    return pl.pallas_call(
        body, name=name, grid=(t // tm,),
        in_specs=[row, row, pl.BlockSpec((1, d), lambda i: (0, 0)), row],
        out_specs=[row, row, pl.BlockSpec((8, d), lambda i: (0, 0))],
        out_shape=[jax.ShapeDtypeStruct((t, d), F32), jax.ShapeDtypeStruct((t, d), BF16),
                   jax.ShapeDtypeStruct((8, d), F32)],
        compiler_params=_params())(dn, x, w.reshape(1, d), dres)


def _final_loss(h, w, target, name):
    t, d = h.shape
    tm = _row_tile(t)

    def body(h_ref, w_ref, t_ref, o_ref, ob_ref, dw_ref, ls_ref):
        i = pl.program_id(0)
        xv = h_ref[...]
        r = lax.rsqrt(jnp.mean(xv * xv, axis=-1, keepdims=True) + EPS)
        xh = xv * r
        err = xh * w_ref[...] - t_ref[...]
        dy = err * (1.0 / d)
        g = dy * w_ref[...]
        dx = r * (g - xh * jnp.mean(g * xh, axis=-1, keepdims=True))
        o_ref[...] = dx
        ob_ref[...] = dx.astype(BF16)
        part = (dy * xh).reshape(tm // 8, 8, d).sum(axis=0)
        lpart = (err * err).reshape(tm // 8, 8, d).sum(axis=0)

        @pl.when(i == 0)
        def _():
            dw_ref[...] = part
            ls_ref[...] = lpart

        @pl.when(i > 0)
        def _():
            dw_ref[...] += part
            ls_ref[...] += lpart

    row = pl.BlockSpec((tm, d), lambda i: (i, 0))
    acc = pl.BlockSpec((8, d), lambda i: (0, 0))
    return pl.pallas_call(
        body, name=name, grid=(t // tm,),
        in_specs=[row, pl.BlockSpec((1, d), lambda i: (0, 0)), row],
        out_specs=[row, row, acc, acc],
        out_shape=[jax.ShapeDtypeStruct((t, d), F32), jax.ShapeDtypeStruct((t, d), BF16),
                   jax.ShapeDtypeStruct((8, d), F32), jax.ShapeDtypeStruct((8, d), F32)],
        compiler_params=_params())(h, w.reshape(1, d), target)


def _cast_split_cols(w, hp, me, name):
    r, fs = w.shape
    v1 = fs - hp
    tm = _pick(r, (256, 128, 64, 32, 16))

    def body(me_ref, w_ref, o_ref):
        o_ref[0] = w_ref[:, :hp].astype(BF16)
        if v1 < hp:
            o_ref[1] = jnp.zeros((tm, hp), BF16)
        o_ref[1, :, :v1] = w_ref[:, hp:].astype(BF16)

    gs = pltpu.PrefetchScalarGridSpec(
        num_scalar_prefetch=1, grid=(r // tm,),
        in_specs=[pl.BlockSpec((tm, fs), lambda i, mr: (i, 0))],
        out_specs=pl.BlockSpec((2, None, tm, hp), lambda i, mr: (0, mr[0], i, 0)))
    return pl.pallas_call(body, name=name, grid_spec=gs, out_shape=jax.ShapeDtypeStruct((2, N_CHIPS, r, hp), BF16),
                          compiler_params=_params())(me, w)


def _cast_split_rows(w, hp, tr, me, name):
    fs, c = w.shape
    nvalid = fs // tr
    per = hp // tr

    def body(me_ref, w_ref, o_ref):
        i = pl.program_id(0)

        @pl.when(i < nvalid)
        def _():
            o_ref[...] = w_ref[...].astype(BF16)

        @pl.when(i >= nvalid)
        def _():
            o_ref[...] = jnp.zeros(o_ref.shape, BF16)

    gs = pltpu.PrefetchScalarGridSpec(
        num_scalar_prefetch=1, grid=(2 * per,),
        in_specs=[pl.BlockSpec((tr, c), lambda i, mr: (jnp.minimum(i, nvalid - 1), 0))],
        out_specs=pl.BlockSpec((None, None, tr, c), lambda i, mr: (i // per, mr[0], i % per, 0)))
    return pl.pallas_call(body, name=name, grid_spec=gs, out_shape=jax.ShapeDtypeStruct((2, N_CHIPS, hp, c), BF16),
                          compiler_params=_params())(me, w)


def _combine_windows(wall, tables, n_tiles, name):
    _, _, d, wh = wall.shape
    tpw = wh // LANE

    def body(tab_ref, a_ref, b_ref, o_ref):
        t = pl.program_id(0)
        both = tab_ref[6, t] == 1
        av = a_ref[...]
        bv = b_ref[...]
        o_ref[...] = jnp.where(both, av + bv, av)

    def amap(t, tab):
        return (tab[0, t], tab[1, t], 0, tab[2, t])

    def bmap(t, tab):
        return (tab[3, t], tab[4, t], 0, tab[5, t])

    gs = pltpu.PrefetchScalarGridSpec(
        num_scalar_prefetch=1, grid=(n_tiles,),
        in_specs=[pl.BlockSpec((None, None, d, LANE), amap), pl.BlockSpec((None, None, d, LANE), bmap)],
        out_specs=pl.BlockSpec((d, LANE), lambda t, tab: (0, t)))
    del tpw
    return pl.pallas_call(body, name=name, grid_spec=gs, out_shape=jax.ShapeDtypeStruct((d, n_tiles * LANE), BF16),
                          compiler_params=_params())(tables, wall, wall)


def _coords():
    return lax.axis_index("x"), lax.axis_index("y"), lax.axis_index("c")


def _remote(src, dst, ssem, rsem, dev):
    return pltpu.make_async_remote_copy(src_ref=src, dst_ref=dst, send_sem=ssem, recv_sem=rsem, device_id=dev,
                                        device_id_type=MESH)


def _mesh_places():
    x, y, c = _coords()
    return c, 2 * x + y, (x, y, 1 - c), [(1 - x, y), (x, 1 - y), (1 - x, 1 - y)]


def _all_gather_comm(bufs):
    n = len(bufs)

    def start(ins, outs, sems):
        c, me, _, chips = _mesh_places()
        for i in range(n):
            for j, (px, py) in enumerate(chips):
                mine = outs[i].at[c, me]
                _remote(mine, mine, sems[0].at[i, j], sems[1].at[i, j], (px, py, c)).start()

    def mid(ins, outs, sems):
        c, _, sib, chips = _mesh_places()
        for i in range(n):
            for j, (px, py) in enumerate(chips):
                slot = outs[i].at[c, 2 * px + py]
                _remote(slot, slot, sems[0].at[i, j], sems[1].at[i, j], (px, py, c)).wait_recv()
                _remote(slot, slot, sems[2].at[i, j], sems[3].at[i, j], sib).start()

    def finish(ins, outs, sems):
        c, me, sib, chips = _mesh_places()
        for i in range(n):
            for j, (px, py) in enumerate(chips):
                slot = outs[i].at[1 - c, 2 * px + py]
                _remote(slot, slot, sems[2].at[i, j], sems[3].at[i, j], sib).wait_recv()
        for i in range(n):
            for j, (px, py) in enumerate(chips):
                mine = outs[i].at[c, me]
                _remote(mine, mine, sems[0].at[i, j], sems[1].at[i, j], (px, py, c)).wait_send()
                slot = outs[i].at[c, 2 * px + py]
                _remote(slot, slot, sems[2].at[i, j], sems[3].at[i, j], sib).wait_send()

    return _Comm(list(bufs), [jax.ShapeDtypeStruct(b.shape, b.dtype) for b in bufs], {i: i for i in range(n)},
                 [pltpu.SemaphoreType.DMA((n, 3))] * 4, start, mid, finish)


def _chip_all_to_all_comm(ps):
    n = len(ps)

    def start(ins, outs, sems):
        c, me, _, chips = _mesh_places()
        for i in range(n):
            for j, (px, py) in enumerate(chips):
                _remote(ins[i].at[2 * px + py], outs[i].at[me], sems[0].at[i, j], sems[1].at[i, j], (px, py, c)).start()

    def finish(ins, outs, sems):
        c, me, _, chips = _mesh_places()
        for i in range(n):
            for j, (px, py) in enumerate(chips):
                slot = outs[i].at[2 * px + py]
                _remote(slot, slot, sems[0].at[i, j], sems[1].at[i, j], (px, py, c)).wait_recv()
        for i in range(n):
            for j, (px, py) in enumerate(chips):
                _remote(ins[i].at[2 * px + py], outs[i].at[me], sems[0].at[i, j], sems[1].at[i, j],
                        (px, py, c)).wait_send()

    return _Comm(list(ps), [jax.ShapeDtypeStruct(p.shape, p.dtype) for p in ps], {},
                 [pltpu.SemaphoreType.DMA((n, 3))] * 2, start, None, finish)


def _comm_call(comm, name):
    n_in, n_out = len(comm.ins), len(comm.out_shape)

    def body(*refs):
        ins, outs, sems = refs[:n_in], refs[n_in:n_in + n_out], refs[n_in + n_out:]
        comm.start(ins, outs, sems)
        if comm.mid:
            comm.mid(ins, outs, sems)
        comm.finish(ins, outs, sems)

    return pl.pallas_call(body, name=name, in_specs=[ANY] * n_in, out_specs=[ANY] * n_out, out_shape=comm.out_shape,
                          input_output_aliases=dict(comm.aliases), scratch_shapes=list(comm.sems))(*comm.ins)


def _sibling_take(gs, name):
    n = len(gs)

    def body(*refs):
        g, out = refs[:n], refs[n:2 * n]
        ssem, rsem = refs[2 * n:]
        x, y, c = _coords()
        sib = (x, y, 1 - c)
        cps = []
        for i in range(n):
            cp = _remote(g[i].at[1 - c], out[i], ssem.at[i], rsem.at[i], sib)
            cp.start()
            cps.append(cp)
        for cp in cps:
            cp.wait()

    out_shape = [jax.ShapeDtypeStruct(s.shape[1:], s.dtype) for s in gs]
    return pl.pallas_call(
        body, name=name, in_specs=[ANY] * n, out_specs=[ANY] * n, out_shape=out_shape,
        scratch_shapes=[pltpu.SemaphoreType.DMA((n,)), pltpu.SemaphoreType.DMA((n,))])(*gs)


def _sibling_join(bufs, name):
    n = len(bufs)

    def body(*refs):
        out = refs[n:2 * n]
        ssem, rsem = refs[2 * n:]
        x, y, c = _coords()
        sib = (x, y, 1 - c)
        cps = []
        for i in range(n):
            mine = out[i].at[c]
            cp = _remote(mine, mine, ssem.at[i], rsem.at[i], sib)
            cp.start()
            cps.append(cp)
        for i in range(n):
            slot = out[i].at[1 - c]
            _remote(slot, slot, ssem.at[i], rsem.at[i], sib).wait_recv()
        for cp in cps:
            cp.wait_send()

    out_shape = [jax.ShapeDtypeStruct(b.shape, b.dtype) for b in bufs]
    return pl.pallas_call(
        body, name=name, in_specs=[ANY] * n, out_specs=[ANY] * n, out_shape=out_shape,
        input_output_aliases={i: i for i in range(n)},
        scratch_shapes=[pltpu.SemaphoreType.DMA((n,)), pltpu.SemaphoreType.DMA((n,))])(*bufs)


def _allreduce_small(vec, name):
    r = vec.shape[0]

    def body(v_ref, o_ref, buf, ssem, rsem):
        x, y, c = _coords()
        my = 4 * x + 2 * y + c
        buf[my] = v_ref[...]
        cps = []
        for dd in range(1, N_DEV):
            px = 1 - x if (dd >> 2) & 1 else x
            py = 1 - y if (dd >> 1) & 1 else y
            pc = 1 - c if dd & 1 else c
            cp = _remote(v_ref, buf.at[my], ssem.at[dd - 1], rsem.at[dd - 1], (px, py, pc))
            cp.start()
            cps.append(cp)
        for dd in range(1, N_DEV):
            px = 1 - x if (dd >> 2) & 1 else x
            py = 1 - y if (dd >> 1) & 1 else y
            pc = 1 - c if dd & 1 else c
            slot = buf.at[4 * px + 2 * py + pc]
            _remote(slot, slot, ssem.at[dd - 1], rsem.at[dd - 1], (px, py, pc)).wait_recv()
        tot = buf[0]
        for k in range(1, N_DEV):
            tot = tot + buf[k]
        o_ref[...] = tot
        for cp in cps:
            cp.wait_send()

    vm = pl.BlockSpec(memory_space=pltpu.VMEM)
    return pl.pallas_call(
        body, name=name, in_specs=[vm], out_specs=vm, out_shape=jax.ShapeDtypeStruct((r, LANE), F32),
        scratch_shapes=[pltpu.VMEM((N_DEV, r, LANE), F32), pltpu.SemaphoreType.DMA((N_DEV - 1,)),
                        pltpu.SemaphoreType.DMA((N_DEV - 1,))])(vec)


def _pair_sum(g, l1, cidx, name):
    _, r, c = g.shape
    tr = _pick(r, (512, 256, 128, 64, 32, 16))

    def body(c_ref, g_ref, l_ref, o_ref):
        o_ref[...] = (g_ref[...].astype(F32) + l_ref[...].astype(F32)).astype(BF16)

    gs = pltpu.PrefetchScalarGridSpec(
        num_scalar_prefetch=1, grid=(r // tr,),
        in_specs=[pl.BlockSpec((None, tr, c), lambda i, cr: (cr[0], i, 0)), pl.BlockSpec((tr, c), lambda i, cr: (i, 0))],
        out_specs=pl.BlockSpec((tr, c), lambda i, cr: (i, 0)))
    return pl.pallas_call(body, name=name, grid_spec=gs, out_shape=jax.ShapeDtypeStruct((r, c), BF16),
                          compiler_params=_params())(cidx, g, l1)


def _chip_sum(p, l2, mc, name):
    _, r, c = l2.shape
    tr = _pick(r, (256, 128, 64, 32, 16))

    def body(mc_ref, p_ref, l0, l1, l2_, l3, o_ref):
        me = mc_ref[0]
        pv = p_ref[...].astype(F32)
        tot = None
        for k, lr in enumerate((l0, l1, l2_, l3)):
            term = jnp.where(me == k, pv, lr[...].astype(F32))
            tot = term if tot is None else tot + term
        o_ref[...] = tot

    def other(k):
        return lambda i, mr: (jnp.where(mr[0] == k, (k + 1) % N_CHIPS, k), i, 0)

    gs = pltpu.PrefetchScalarGridSpec(
        num_scalar_prefetch=1, grid=(r // tr,),
        in_specs=[pl.BlockSpec((None, tr, c), lambda i, mr: (mr[0], i, 0))]
        + [pl.BlockSpec((None, tr, c), other(k)) for k in range(N_CHIPS)],
        out_specs=pl.BlockSpec((None, tr, c), lambda i, mr: (mr[1], i, 0)))
    return pl.pallas_call(body, name=name, grid_spec=gs, out_shape=jax.ShapeDtypeStruct((2, r, c), F32),
                          compiler_params=_params())(mc, p, l2, l2, l2, l2)


def _adamw(g, w, m, v, name):
    r, c = w.shape
    tr = r
    if r * c * 4 > (2 << 20):
        tr = next(p for p in (256, 128, 64, 32, 16, 8) if r % p == 0 and (p * c * 4 <= (2 << 20) or p == 8))

    def body(g_ref, w_ref, m_ref, v_ref, d_ref, nm_ref, nv_ref):
        gv = g_ref[...]
        mn = ADAM_B1 * m_ref[...] + (1.0 - ADAM_B1) * gv
        vn = ADAM_B2 * v_ref[...] + (1.0 - ADAM_B2) * (gv * gv)
        m_hat = mn / (1.0 - ADAM_B1 ** ADAM_STEP)
        v_hat = vn / (1.0 - ADAM_B2 ** ADAM_STEP)
        d_ref[...] = -ADAM_LR * (m_hat / (jnp.sqrt(v_hat) + ADAM_EPS) + ADAM_WD * w_ref[...])
        nm_ref[...] = mn
        nv_ref[...] = vn

    blk = pl.BlockSpec((tr, c), lambda i: (i, 0))
    osd = jax.ShapeDtypeStruct((r, c), F32)
    return pl.pallas_call(body, name=name, grid=(r // tr,), in_specs=[blk] * 4, out_specs=[blk] * 3,
                          out_shape=[osd] * 3, compiler_params=_params())(g, w, m, v)


def _attn_probs(q, k, q0, s_len):
    tq = q.shape[0]
    sc = lax.dot_general(q, k, NT, preferred_element_type=F32) * (HEAD_DIM ** -0.5)
    dlt = (q0 + lax.broadcasted_iota(jnp.int32, (tq, s_len), 0)) - lax.broadcasted_iota(jnp.int32, (tq, s_len), 1)
    cnt = jnp.zeros((tq, s_len), F32)
    for window, dil in DILATED_CONFIGS:
        seen = (dlt >= 0) & (dlt <= window) & ((dlt & (dil - 1)) == 0)
        cnt = cnt + jnp.where(seen, 1.0, 0.0)
    live = cnt > 0.0
    m = jnp.max(jnp.where(live, sc, -jnp.inf), axis=-1, keepdims=True)
    p = cnt * jnp.exp(jnp.where(live, sc - m, -jnp.inf))
    return p / jnp.sum(p, axis=-1, keepdims=True)


def _attn_key_groups(nq):
    return next(g for g in (4, 2, 1) if nq % g == 0)


def _attn_fwd(proj, nh, s_len, name, comm):
    t = proj.shape[0]
    tq = min(256, s_len)
    nq = s_len // tq
    ng = _attn_key_groups(nq)
    per = nq // ng

    def body(q_ref, k_ref, v_ref, o_ref):
        qi = pl.program_id(2)
        for j in range(ng):
            klen = (j + 1) * per * tq

            @pl.when(qi // per == j)
            def _(klen=klen):
                p = _attn_probs(q_ref[...].astype(BF16), k_ref[:klen, :].astype(BF16), qi * tq, klen)
                o_ref[...] = jnp.dot(p.astype(BF16), v_ref[:klen, :].astype(BF16), preferred_element_type=F32)

    q_spec = pl.BlockSpec((tq, HEAD_DIM), lambda b, h, qi: (b * nq + qi, h))
    outs, couts = _call_carrying(
        body, name, (t // s_len, nh, nq),
        [q_spec, pl.BlockSpec((s_len, HEAD_DIM), lambda b, h, qi: (b, nh + h)),
         pl.BlockSpec((s_len, HEAD_DIM), lambda b, h, qi: (b, 2 * nh + h))],
        [q_spec], [jax.ShapeDtypeStruct((t, nh * HEAD_DIM), F32)], [], (proj, proj, proj), comm)
    return outs[0], couts


def _attn_bwd(proj, o, do, nh, s_len, name):
    t = proj.shape[0]
    tq = min(256, s_len)
    nq = s_len // tq
    ng = _attn_key_groups(nq)
    per = nq // ng
    scale = HEAD_DIM ** -0.5

    def body(q_ref, k_ref, v_ref, o_ref, do_ref, dq_ref, dk_ref, dv_ref, dk_acc, dv_acc):
        qi = pl.program_id(2)

        @pl.when(qi == 0)
        def _():
            dk_acc[...] = jnp.zeros(dk_acc.shape, F32)
            dv_acc[...] = jnp.zeros(dv_acc.shape, F32)

        for j in range(ng):
            klen = (j + 1) * per * tq

            @pl.when(qi // per == j)
            def _(klen=klen):
                q = q_ref[...].astype(BF16)
                k = k_ref[:klen, :].astype(BF16)
                p = _attn_probs(q, k, qi * tq, klen)
                dob = do_ref[...]
                dp = lax.dot_general(dob, v_ref[:klen, :].astype(BF16), NT, preferred_element_type=F32)
                delta = jnp.sum(dob.astype(F32) * o_ref[...], axis=-1, keepdims=True)
                ds = (p * (dp - delta)).astype(BF16)
                dq_ref[...] = (jnp.dot(ds, k, preferred_element_type=F32) * scale).astype(BF16)
                dk_acc[:klen, :] += lax.dot_general(ds, q, TN, preferred_element_type=F32) * scale
                dv_acc[:klen, :] += lax.dot_general(p.astype(BF16), dob, TN, preferred_element_type=F32)

        @pl.when(qi == nq - 1)
        def _():
            dk_ref[...] = dk_acc[...].astype(BF16)
            dv_ref[...] = dv_acc[...].astype(BF16)

    q_spec = pl.BlockSpec((tq, HEAD_DIM), lambda b, h, qi: (b * nq + qi, h))
    kv_out = pl.BlockSpec((s_len, HEAD_DIM), lambda b, h, qi: (b, h))
    osd = jax.ShapeDtypeStruct((t, nh * HEAD_DIM), BF16)
    return pl.pallas_call(
        body, name=name, grid=(t // s_len, nh, nq),
        in_specs=[q_spec, pl.BlockSpec((s_len, HEAD_DIM), lambda b, h, qi: (b, nh + h)),
                  pl.BlockSpec((s_len, HEAD_DIM), lambda b, h, qi: (b, 2 * nh + h)), q_spec, q_spec],
        out_specs=[q_spec, kv_out, kv_out], out_shape=[osd, osd, osd],
        scratch_shapes=[pltpu.VMEM((s_len, HEAD_DIM), F32), pltpu.VMEM((s_len, HEAD_DIM), F32)],
        compiler_params=_params())(proj, proj, proj, o, do)


def _conv_taps(x, w_ref, s_len):
    row = lax.broadcasted_iota(jnp.int32, x.shape, 0)
    c = w_ref[CONV_WIDTH - 1:CONV_WIDTH, :] * x
    for j in range(1, CONV_WIDTH):
        xs = jnp.where(row >= j, pltpu.roll(x, j, 0), 0.0)
        c = c + w_ref[CONV_WIDTH - 1 - j:CONV_WIDTH - j, :] * xs
    return c


def _conv_fwd(proj, conv8, col0, width, s_len, name):
    t = proj.shape[0]
    cb = _pick(width, (512, 256, 128))
    c0 = col0 // cb

    def body(x_ref, w_ref, o_ref):
        c = _conv_taps(x_ref[...], w_ref, s_len)
        o_ref[...] = c * _sigmoid(c)

    return pl.pallas_call(
        body, name=name, grid=(t // s_len, width // cb),
        in_specs=[pl.BlockSpec((s_len, cb), lambda b, j: (b, c0 + j)), pl.BlockSpec((8, cb), lambda b, j: (0, j))],
        out_specs=pl.BlockSpec((s_len, cb), lambda b, j: (b, j)),
        out_shape=jax.ShapeDtypeStruct((t, width), F32), compiler_params=_params())(proj, conv8)


def _conv_bwd(proj, conv8, du, col0, width, s_len, name):
    t = proj.shape[0]
    cb = _pick(width, (512, 256, 128))
    c0 = col0 // cb

    def body(x_ref, w_ref, du_ref, dx_ref, dw_ref):
        b = pl.program_id(1)
        x = x_ref[...]
        c = _conv_taps(x, w_ref, s_len)
        sg = _sigmoid(c)
        dc = du_ref[...] * (sg * (1.0 + c * (1.0 - sg)))
        row = lax.broadcasted_iota(jnp.int32, x.shape, 0)
        dx = w_ref[CONV_WIDTH - 1:CONV_WIDTH, :] * dc
        rows = [jnp.sum(dc * x, axis=0, keepdims=True)]
        for j in range(1, CONV_WIDTH):
            up = jnp.where(row < s_len - j, pltpu.roll(dc, s_len - j, 0), 0.0)
            dx = dx + w_ref[CONV_WIDTH - 1 - j:CONV_WIDTH - j, :] * up
            xs = jnp.where(row >= j, pltpu.roll(x, j, 0), 0.0)
            rows.append(jnp.sum(dc * xs, axis=0, keepdims=True))
        dx_ref[...] = dx.astype(BF16)
        part = jnp.concatenate(rows[::-1] + [jnp.zeros((8 - CONV_WIDTH, cb), F32)], axis=0)

        @pl.when(b == 0)
        def _():
            dw_ref[...] = part

        @pl.when(b > 0)
        def _():
            dw_ref[...] += part

    return pl.pallas_call(
        body, name=name, grid=(width // cb, t // s_len),
        in_specs=[pl.BlockSpec((s_len, cb), lambda j, b: (b, c0 + j)), pl.BlockSpec((8, cb), lambda j, b: (0, j)),
                  pl.BlockSpec((s_len, cb), lambda j, b: (b, j))],
        out_specs=[pl.BlockSpec((s_len, cb), lambda j, b: (b, j)), pl.BlockSpec((8, cb), lambda j, b: (0, j))],
        out_shape=[jax.ShapeDtypeStruct((t, width), BF16), jax.ShapeDtypeStruct((8, width), F32)],
        compiler_params=_params())(proj, conv8, du)


def _split_bf16(v):
    hi = v.astype(BF16)
    return hi, (v - hi.astype(F32)).astype(BF16)


def _bdot(a, b, dims):
    return lax.dot_general(a, b, (dims, ((0,), (0,))), preferred_element_type=F32)


def _dot3(a, b, dims, exact_a=False):
    ah, al = _split_bf16(a)
    bh, bl = _split_bf16(b)
    out = _bdot(ah, bh, dims) + _bdot(ah, bl, dims)
    return out if exact_a else out + _bdot(al, bh, dims)


@functools.partial(jax.custom_vjp, nondiff_argnums=(2,))
def _bmm(a, b, exact_a=False):
    return _dot3(a, b, ((2,), (1,)), exact_a)


def _bmm_fwd(a, b, exact_a):
    return _dot3(a, b, ((2,), (1,)), exact_a), (a, b)


def _bmm_bwd(exact_a, res, ct):
    a, b = res
    da = jnp.zeros_like(a) if exact_a else _dot3(ct, b, ((2,), (2,)))
    db = _dot3(a, ct, ((1,), (1,)), exact_a)
    return da, db


_bmm.defvjp(_bmm_fwd, _bmm_bwd)


@jax.custom_vjp
def _bmm_nt(a, b):
    return _bdot(a.astype(BF16), b.astype(BF16), ((2,), (2,)))


def _bmm_nt_fwd(a, b):
    return _bmm_nt(a, b), (a, b)


def _bmm_nt_bwd(res, ct):
    a, b = res
    ctb = ct.astype(BF16)
    return _bdot(ctb, b.astype(BF16), ((2,), (1,))), _bdot(ctb, a.astype(BF16), ((1,), (1,)))


_bmm_nt.defvjp(_bmm_nt_fwd, _bmm_nt_bwd)


def _unit_lower_inverse(nm):
    c = nm.shape[-1]
    eye = (lax.broadcasted_iota(jnp.int32, (c, c), 0) == lax.broadcasted_iota(jnp.int32, (c, c), 1)).astype(F32)
    x = -nm
    inv = eye[None] + x
    p = x
    for _ in range(int(math.log2(c)) - 1):
        p = _bmm(p, p)
        inv = inv + _bmm(inv, p)
    return inv


def _dn_chunk_terms(uq, uk, uv, a_col, b_col, alog, dtb):
    n, c, dh = uq.shape
    q = uq * lax.rsqrt(jnp.sum(uq * uq, axis=-1, keepdims=True) + EPS) * (HEAD_DIM ** -0.5)
    k = uk * lax.rsqrt(jnp.sum(uk * uk, axis=-1, keepdims=True) + EPS)
    beta = _sigmoid(b_col)
    xa = a_col + dtb
    g = -jnp.exp(alog) * (jnp.maximum(xa, 0.0) + jnp.log(1.0 + jnp.exp(-jnp.abs(xa))))
    ri = lax.broadcasted_iota(jnp.int32, (c, c), 0)
    ci = lax.broadcasted_iota(jnp.int32, (c, c), 1)
    incl = ri >= ci
    strict = ri > ci
    l_incl = jnp.broadcast_to(incl.astype(F32)[None], (n, c, c))
    gb = jnp.broadcast_to(g, (n, c, dh))
    l_sums = jnp.broadcast_to(jnp.concatenate([incl.astype(F32), jnp.ones((dh - c, c), F32)], axis=0)[None], (n, dh, c))
    sums = _bmm(l_sums, gb, True)
    gc, gtot = sums[:, :c], sums[:, c:2 * c]
    gdiff = _bmm(l_incl, jnp.broadcast_to(g, (n, c, c)) * strict.astype(F32)[None], True)
    decay = jnp.where(incl[None], jnp.exp(jnp.where(incl[None], gdiff, 0.0)), 0.0)
    kb = k * beta
    nm = jnp.where(strict[None], _bmm_nt(kb, k) * decay, 0.0)
    tinv = _unit_lower_inverse(nm)
    w = _bmm(tinv, kb * jnp.exp(gc))
    u = _bmm(tinv, uv * beta)
    qk = _bmm_nt(q, k) * decay
    q_dec = q * jnp.exp(gc)
    k_dec = k * jnp.exp(gtot - gc)
    g_last = jnp.exp(jnp.concatenate([gtot] * (dh // c), axis=1))
    return w, u, qk, q_dec, k_dec, g_last


DN_SUB = 8


def _dn_gather_inputs(uq_ref, uk_ref, uv_ref, ba_ref, prm_ref, h, nh, rows, nb):
    ba = ba_ref[rows, :]
    lane = lax.broadcasted_iota(jnp.int32, ba.shape, 1)
    b_col = jnp.sum(jnp.where(lane == h, ba, 0.0), axis=-1, keepdims=True).reshape(nb, CHUNK, 1)
    a_col = jnp.sum(jnp.where(lane == nh + h, ba, 0.0), axis=-1, keepdims=True).reshape(nb, CHUNK, 1)
    lane1 = lax.broadcasted_iota(jnp.int32, (1, LANE), 1)
    alog = jnp.sum(jnp.where(lane1 == h, prm_ref[1:2, :], 0.0), axis=-1, keepdims=True)
    dtb = jnp.sum(jnp.where(lane1 == h, prm_ref[2:3, :], 0.0), axis=-1, keepdims=True)
    shp = (nb, CHUNK, HEAD_DIM)
    return (uq_ref[rows, :].reshape(shp), uk_ref[rows, :].reshape(shp), uv_ref[rows, :].reshape(shp),
            a_col, b_col, alog, dtb)


def _dn_fill_terms(in_refs, h, nh, n, term_refs):
    nb = min(DN_SUB, n)

    def sub(i, carry):
        rows = pl.ds(pl.multiple_of(i * (nb * CHUNK), nb * CHUNK), nb * CHUNK)
        terms = _dn_chunk_terms(*_dn_gather_inputs(*in_refs, h, nh, rows, nb))
        for r, v in zip(term_refs, terms):
            r[pl.ds(i * nb, nb)] = v
        return carry

    lax.fori_loop(0, n // nb, sub, 0)


def _dn_scan(terms_refs, o_ref, st_ref, n):
    w_ref, u_ref, qk_ref, qd_ref, kd_ref, gl_ref = terms_refs

    def step(i, state):
        if st_ref is not None:
            st_ref[i] = state
        sb = state.astype(BF16)
        v_new = u_ref[i] - jnp.dot(w_ref[i].astype(BF16), sb, preferred_element_type=F32)
        vb = v_new.astype(BF16)
        o_ref[i] = (jnp.dot(qd_ref[i].astype(BF16), sb, preferred_element_type=F32)
                    + jnp.dot(qk_ref[i].astype(BF16), vb, preferred_element_type=F32))
        return state * gl_ref[i] + lax.dot_general(kd_ref[i].astype(BF16), vb, TN, preferred_element_type=F32)

    lax.fori_loop(0, n, step, jnp.zeros((HEAD_DIM, HEAD_DIM), F32))


def _dn_specs(nh, nh_a, s_len, zc0, bac):
    head = lambda off: pl.BlockSpec((s_len, HEAD_DIM), lambda b, h: (b, off + h))
    return dict(uq=head(0), uk=head(nh), uv=head(2 * nh), z=head(zc0),
                ba=pl.BlockSpec((s_len, LANE), lambda b, h: (b, bac)),
                prm=pl.BlockSpec((8, LANE), lambda b, h: (0, 0)), dout=head(nh_a), out=head(0))


def _dn_scratch(n, with_states):
    big = pltpu.VMEM((n, CHUNK, HEAD_DIM), F32)
    sc = [big, big, pltpu.VMEM((n, CHUNK, CHUNK), F32), big, big, pltpu.VMEM((n, HEAD_DIM, HEAD_DIM), F32), big]
    if with_states:
        sc.append(pltpu.VMEM((n, HEAD_DIM, HEAD_DIM), F32))
    return sc


def _dn_fwd(u, proj, prm, nh, s_len, zc0, bac, name, comm):
    t = u.shape[0]
    n = s_len // CHUNK
    sp = _dn_specs(nh, 0, s_len, zc0, bac)

    def body(uq_ref, uk_ref, uv_ref, z_ref, ba_ref, prm_ref, o_ref, *scr):
        h = pl.program_id(1)
        _dn_fill_terms((uq_ref, uk_ref, uv_ref, ba_ref, prm_ref), h, nh, n, scr[:6])
        _dn_scan(scr[:6], scr[6], None, n)
        o = scr[6][...].reshape(s_len, HEAD_DIM)
        z = z_ref[...]
        r = lax.rsqrt(jnp.mean(o * o, axis=-1, keepdims=True) + EPS)
        o_ref[...] = o * r * prm_ref[0:1, :] * (z * _sigmoid(z))

    outs, couts = _call_carrying(
        body, name, (t // s_len, nh), [sp["uq"], sp["uk"], sp["uv"], sp["z"], sp["ba"], sp["prm"]], [sp["out"]],
        [jax.ShapeDtypeStruct((t, nh * HEAD_DIM), F32)], _dn_scratch(n, False), (u, u, u, proj, proj, prm), comm)
    return outs[0], couts


def _dn_bwd(u, proj, prm, dcat, nh, nh_a, s_len, zc0, bac, name):
    t = u.shape[0]
    n = s_len // CHUNK
    sp = _dn_specs(nh, nh_a, s_len, zc0, bac)

    def body(uq_ref, uk_ref, uv_ref, z_ref, ba_ref, prm_ref, do_ref,
             duq_ref, duk_ref, duv_ref, dz_ref, dba_ref, dprm_ref, *scr):
        b, h = pl.program_id(0), pl.program_id(1)
        in_refs = (uq_ref, uk_ref, uv_ref, ba_ref, prm_ref)
        w_ref, u_ref, qk_ref, qd_ref, kd_ref, gl_ref, o_scr, st_ref = scr
        _dn_fill_terms(in_refs, h, nh, n, scr[:6])
        _dn_scan(scr[:6], o_scr, st_ref, n)

        o = o_scr[...].reshape(s_len, HEAD_DIM)
        z = z_ref[...]
        dout = do_ref[...].astype(F32)
        gain = prm_ref[0:1, :]
        sg = _sigmoid(z)
        sz = z * sg
        r = lax.rsqrt(jnp.mean(o * o, axis=-1, keepdims=True) + EPS)
        oh = o * r
        dgain = jnp.sum(dout * oh * sz, axis=0, keepdims=True)
        dz_ref[...] = (dout * oh * gain * (sg * (1.0 + z * (1.0 - sg)))).astype(BF16)
        doh = dout * gain * sz
        d_o = r * (doh - oh * jnp.mean(doh * oh, axis=-1, keepdims=True))
        o_scr[...] = d_o.reshape(n, CHUNK, HEAD_DIM)

        def step(j, ds):
            i = n - 1 - j
            st = st_ref[i]
            sb = st.astype(BF16)
            wi, qki, qdi, kdi, gli = w_ref[i], qk_ref[i], qd_ref[i], kd_ref[i], gl_ref[i]
            v_new = u_ref[i] - jnp.dot(wi.astype(BF16), sb, preferred_element_type=F32)
            vb = v_new.astype(BF16)
            don = o_scr[i].astype(BF16)
            dsb = ds.astype(BF16)
            dv = (lax.dot_general(qki.astype(BF16), don, TN, preferred_element_type=F32)
                  + jnp.dot(kdi.astype(BF16), dsb, preferred_element_type=F32))
            dvb = dv.astype(BF16)
            qd_ref[i] = lax.dot_general(don, sb, NT, preferred_element_type=F32)
            qk_ref[i] = lax.dot_general(don, vb, NT, preferred_element_type=F32)
            kd_ref[i] = lax.dot_general(vb, dsb, NT, preferred_element_type=F32)
            gl_ref[i] = ds * st
            u_ref[i] = dv
            w_ref[i] = -lax.dot_general(dvb, sb, NT, preferred_element_type=F32)
            return (ds * gli + lax.dot_general(qdi.astype(BF16), don, TN, preferred_element_type=F32)
                    - lax.dot_general(wi.astype(BF16), dvb, TN, preferred_element_type=F32))

        lax.fori_loop(0, n, step, jnp.zeros((HEAD_DIM, HEAD_DIM), F32))

        @pl.when(h == 0)
        def _():
            dba_ref[...] = jnp.zeros(dba_ref.shape, F32)

        nb = min(DN_SUB, n)

        def sub(i, carry):
            rows = pl.ds(pl.multiple_of(i * (nb * CHUNK), nb * CHUNK), nb * CHUNK)
            _, pull = jax.vjp(_dn_chunk_terms, *_dn_gather_inputs(*in_refs, h, nh, rows, nb))
            duq, duk, duv, da_col, db_col, dal, ddt = pull(tuple(r[pl.ds(i * nb, nb)] for r in scr[:6]))
            duq_ref[rows, :] = duq.reshape(nb * CHUNK, HEAD_DIM)
            duk_ref[rows, :] = duk.reshape(nb * CHUNK, HEAD_DIM)
            duv_ref[rows, :] = duv.reshape(nb * CHUNK, HEAD_DIM)
            lane = lax.broadcasted_iota(jnp.int32, (nb * CHUNK, LANE), 1)
            dba_ref[rows, :] += (jnp.where(lane == h, db_col.reshape(nb * CHUNK, 1), 0.0)
                                 + jnp.where(lane == nh + h, da_col.reshape(nb * CHUNK, 1), 0.0))
            return carry[0] + dal, carry[1] + ddt

        dalog, ddtb = lax.fori_loop(0, n // nb, sub, (jnp.zeros((1, 1), F32), jnp.zeros((1, 1), F32)))
        lane1 = lax.broadcasted_iota(jnp.int32, (1, LANE), 1)
        dprm = jnp.concatenate([dgain, jnp.where(lane1 == h, dalog, 0.0), jnp.where(lane1 == h, ddtb, 0.0),
                                jnp.zeros((5, LANE), F32)], axis=0)

        @pl.when((b == 0) & (h == 0))
        def _():
            dprm_ref[...] = dprm

        @pl.when((b > 0) | (h > 0))
        def _():
            dprm_ref[...] += dprm

    osd = jax.ShapeDtypeStruct((t, nh * HEAD_DIM), F32)
    return pl.pallas_call(
        body, name=name, grid=(t // s_len, nh),
        in_specs=[sp["uq"], sp["uk"], sp["uv"], sp["z"], sp["ba"], sp["prm"], sp["dout"]],
        out_specs=[sp["out"], sp["out"], sp["out"], sp["out"], pl.BlockSpec((s_len, LANE), lambda b, h: (b, 0)),
                   pl.BlockSpec((8, LANE), lambda b, h: (0, 0))],
        out_shape=[osd, osd, osd, jax.ShapeDtypeStruct((t, nh * HEAD_DIM), BF16),
                   jax.ShapeDtypeStruct((t, LANE), F32), jax.ShapeDtypeStruct((8, LANE), F32)],
        scratch_shapes=_dn_scratch(n, True), compiler_params=_params())(u, u, u, proj, proj, prm, dcat)


def _w_in_windows(ws):
    w0 = [(ws * k) // LANE * LANE for k in range(N_CHIPS)]
    sh = [ws * k - w0[k] for k in range(N_CHIPS)]
    ww = _ceil_to(max(sh) + ws, 2 * LANE)
    n_tiles = (w0[-1] + ww) // LANE
    tpw = ww // LANE
    tph = tpw // 2
    tab = np.zeros((7, n_tiles), np.int32)
    for t in range(n_tiles):
        ks = [k for k in range(N_CHIPS) if w0[k] // LANE <= t < w0[k] // LANE + tpw]
        k1 = ks[-1]
        lt = t - w0[k1] // LANE
        tab[0, t], tab[1, t], tab[2, t] = lt // tph, k1, lt % tph
        k2 = ks[0] if len(ks) > 1 else k1
        lt2 = t - w0[k2] // LANE
        tab[3, t], tab[4, t], tab[5, t] = lt2 // tph, k2, lt2 % tph
        tab[6, t] = 1 if len(ks) > 1 else 0
        assert len(ks) <= 2
    return w0, sh, ww, n_tiles, tab


def kernel(x, ffn1_norm, ffn1_w_gate, ffn1_w_up, ffn1_w_down, mix_norm, w_in, conv_w, a_log, dt_bias, dn_norm, w_out, ffn2_norm, ffn2_w_gate, ffn2_w_up, ffn2_w_down, final_norm, loss_target, m_ffn1_norm, m_ffn1_w_gate, m_ffn1_w_up, m_ffn1_w_down, m_mix_norm, m_w_in, m_conv_w, m_a_log, m_dt_bias, m_dn_norm, m_w_out, m_ffn2_norm, m_ffn2_w_gate, m_ffn2_w_up, m_ffn2_w_down, m_final_norm, v_ffn1_norm, v_ffn1_w_gate, v_ffn1_w_up, v_ffn1_w_down, v_mix_norm, v_w_in, v_conv_w, v_a_log, v_dt_bias, v_dn_norm, v_w_out, v_ffn2_norm, v_ffn2_w_gate, v_ffn2_w_up, v_ffn2_w_down, v_final_norm):
    bl, s_, d = x.shape
    t = bl * s_
    fs = ffn1_w_gate.shape[1]
    hp = _ceil_to(-(-fs // 2), LANE)
    ws = w_in.shape[1]
    d_mix = w_out.shape[0] * N_CHIPS
    d_attn = d_dn = d_mix // 2
    nh_d = d_dn // HEAD_DIM
    d_in = 3 * d_attn + 4 * d_dn + 2 * nh_d
    cs = conv_w.shape[1]
    assert ws * N_CHIPS == d_in and cs * N_CHIPS == 3 * d_dn

    xi, yi, ci = lax.axis_index("x"), lax.axis_index("y"), lax.axis_index("c")
    me = 2 * xi + yi
    cidx = jnp.reshape(ci, (1,)).astype(jnp.int32)
    meidx = jnp.reshape(me, (1,)).astype(jnp.int32)
    mcidx = jnp.stack([me, ci]).astype(jnp.int32)

    w0, sh, ww, n_tiles, tab = _w_in_windows(ws)
    shift = (ws * me) % LANE
    w_in_win = lax.dynamic_update_slice(jnp.zeros((d, ww), F32), w_in, (jnp.int32(0), shift))
    rows_tr = math.gcd(hp, fs)
    conv_piece = jnp.pad(conv_w, ((0, 8 - CONV_WIDTH), (0, 0))).reshape(8, 2, cs // 2).transpose(1, 0, 2)
    z0 = jnp.int32(0)
    pieces = [
        _cast_split_cols(ffn1_w_gate, hp, meidx, "cast_g1"),
        _cast_split_cols(ffn1_w_up, hp, meidx, "cast_u1"),
        _cast_split_rows(ffn1_w_down, hp, rows_tr, meidx, "cast_d1"),
        _cast_split_cols(w_in_win, ww // 2, meidx, "cast_in"),
        _cast_split_rows(w_out, w_out.shape[0] // 2, w_out.shape[0] // 2, meidx, "cast_out"),
        _cast_split_cols(ffn2_w_gate, hp, meidx, "cast_g2"),
        _cast_split_cols(ffn2_w_up, hp, meidx, "cast_u2"),
        _cast_split_rows(ffn2_w_down, hp, rows_tr, meidx, "cast_d2"),
        lax.dynamic_update_slice(jnp.zeros((2, N_CHIPS, 8, cs // 2), F32), conv_piece[:, None], (z0, me, z0, z0)),
    ]
    p_g1, p_u1, p_d1, p_in, p_out, p_g2, p_u2, p_d2, p_conv = pieces
    npc = 8
    ident = lambda p: p
    cat_map = lambda p: 2 * (p % N_CHIPS) + p // N_CHIPS
    as_cols = lambda a: a.reshape(npc, d, hp)
    as_rows = lambda a: a.reshape(npc, hp, d)

    wg1, wu1 = _comm_call(_all_gather_comm([p_g1, p_u1]), "all_gather_first")
    wg1, wu1 = as_cols(wg1), as_cols(wu1)
    h0 = x.reshape(t, d)
    n1 = _rms_fwd(h0, ffn1_norm, "rms1")
    (a1, b1, s1), (wd1, win_all) = _ffn_up(n1, wg1, wu1, "ffn1_up", comm=_all_gather_comm([p_d1, p_in]))
    wd1 = as_rows(wd1)
    h1, (wout, conv_all) = _mm_pieces_resid(s1, wd1, h0, 0.5, ident, "ffn1_down",
                                            comm=_all_gather_comm([p_out, p_conv]))
    wout = wout.reshape(npc, w_out.shape[0] // 2, d)
    conv8 = conv_all.transpose(2, 1, 0, 3).reshape(8, 3 * d_dn)
    win_full = _combine_windows(win_all, jnp.asarray(tab), n_tiles, "combine_w_in")
    n2 = _rms_fwd(h1, mix_norm, "rms2")
    proj, (wg2,) = _mm2d(n2, win_full, NN, F32, "in_proj", (1024, 768, 4096), comm=_all_gather_comm([p_g2]))
    nh_a = d_attn // HEAD_DIM
    attn, (wu2,) = _attn_fwd(proj, nh_a, s_, "attn_fwd", _all_gather_comm([p_u2]))
    zc0 = (3 * d_attn + 3 * d_dn) // HEAD_DIM
    bac = (3 * d_attn + 4 * d_dn) // LANE
    row128 = lambda v: jnp.pad(v, (0, LANE - v.shape[0])).reshape(1, LANE)
    prm = jnp.concatenate([row128(dn_norm), row128(a_log), row128(dt_bias), jnp.zeros((5, LANE), F32)], axis=0)
    u_dn = _conv_fwd(proj, conv8, 3 * d_attn, 3 * d_dn, s_, "dn_conv")
    dn_out, (wd2,) = _dn_fwd(u_dn, proj, prm, nh_d, s_, zc0, bac, "dn_fwd", _all_gather_comm([p_d2]))
    wg2, wu2, wd2 = as_cols(wg2), as_cols(wu2), as_rows(wd2)
    cat_b = jnp.concatenate([attn, dn_out], axis=1).astype(BF16)
    h2 = _mm_pieces_resid(cat_b, wout, h1, 1.0, cat_map, "out_proj")
    n3 = _rms_fwd(h2, ffn2_norm, "rms3")
    a3, b3, s3 = _ffn_up(n3, wg2, wu2, "ffn2_up")
    h3 = _mm_pieces_resid(s3, wd2, h2, 0.5, ident, "ffn2_down")

    def rs_front(gs, tag):
        gs = [g.reshape((2, N_CHIPS * g.shape[-2], g.shape[-1])) for g in gs]
        from_sib = _sibling_take(gs, f"rs_sibling_take_{tag}")
        ps = [_pair_sum(g, l, cidx, f"rs_pair_sum_{tag}_{i}") for i, (g, l) in enumerate(zip(gs, from_sib))]
        return [p.reshape(N_CHIPS, p.shape[0] // N_CHIPS, p.shape[1]) for p in ps]

    dh3, dh3b, dwf_p, lsq_p = _final_loss(h3, final_norm, loss_target.reshape(t, d), "final_loss")
    da3, db3 = _ffn_bwd_hidden(dh3b, wd2, a3, b3, "ffn2_bwd_hidden")
    g_wd2 = _grad_rows_pieces(s3, dh3b, 0.5, ident, npc, "ffn2_grad_down")
    p_d2s = rs_front([g_wd2], "d2")
    (g_wg2, g_wu2), l_d2 = _grad_cols_pieces(n3, da3, db3, npc, "ffn2_grad_up", comm=_chip_all_to_all_comm(p_d2s))
    p_gu2 = rs_front([g_wg2, g_wu2], "gu2")
    dn3, l_gu2 = _ffn_bwd_input(da3, db3, wg2, wu2, "ffn2_bwd_input", comm=_chip_all_to_all_comm(p_gu2))
    dh2, dh2b, dw3_p = _rms_bwd(dn3, h2, ffn2_norm, dh3, "rms3_bwd")

    dcat = _mm_nt_pieces_out(dh2b, wout, cat_map, "out_proj_bwd")
    g_wout = _grad_rows_pieces(cat_b, dh2b, 1.0, cat_map, npc, "out_proj_grad")
    dq_a, dk_a, dv_a = _attn_bwd(proj, attn, dcat, nh_a, s_, "attn_bwd")
    duq, duk, duv, dz, dba, dprm = _dn_bwd(u_dn, proj, prm, dcat, nh_d, nh_a, s_, zc0, bac, "dn_bwd")
    dx_conv, dconv8 = _conv_bwd(proj, conv8, jnp.concatenate([duq, duk, duv], axis=1), 3 * d_attn, 3 * d_dn, s_,
                                "dn_conv_bwd")
    used = 3 * d_attn + 4 * d_dn + LANE
    dproj_b = jnp.concatenate([dq_a, dk_a, dv_a, dx_conv, dz, dba.astype(BF16),
                               jnp.zeros((t, proj.shape[1] - used), BF16)], axis=1)
    dconv, ddnn, dalog, ddtb = dconv8[:CONV_WIDTH], dprm[0, :dn_norm.shape[0]], dprm[1, :nh_d], dprm[2, :nh_d]
    g_win_full = _mm2d(n2, dproj_b, TN, BF16, "in_proj_grad", (1024, 2432, 512))
    wh = ww // 2
    g_win = jnp.stack([jnp.stack([g_win_full[:, w0[k] + wh * h: w0[k] + wh * (h + 1)] for k in range(N_CHIPS)])
                       for h in range(2)])
    p_ow = rs_front([g_wout, g_win], "ow")
    dn2, l_ow = _mm2d(dproj_b, win_full, NT, F32, "in_proj_bwd", (1024, 2048, 768), comm=_chip_all_to_all_comm(p_ow))
    dh1, dh1b, dwm_p = _rms_bwd(dn2, h1, mix_norm, dh2, "rms2_bwd")

    da1, db1 = _ffn_bwd_hidden(dh1b, wd1, a1, b1, "ffn1_bwd_hidden")
    g_wd1 = _grad_rows_pieces(s1, dh1b, 0.5, ident, npc, "ffn1_grad_down")
    p_d1s = rs_front([g_wd1], "d1")
    (g_wg1, g_wu1), l_d1 = _grad_cols_pieces(n1, da1, db1, npc, "ffn1_grad_up", comm=_chip_all_to_all_comm(p_d1s))
    p_gu1 = rs_front([g_wg1, g_wu1], "gu1")
    dn1, l_gu1 = _ffn_bwd_input(da1, db1, wg1, wu1, "ffn1_bwd_input", comm=_chip_all_to_all_comm(p_gu1))
    dh0, _, dw1_p = _rms_bwd(dn1, h0, ffn1_norm, dh1, "rms1_bwd")
    grad_x = dh0.reshape(bl, s_, d)

    pair = [p_gu1[0], p_gu1[1], p_d1s[0], p_ow[1], p_ow[0], p_gu2[0], p_gu2[1], p_d2s[0]]
    from_chips = [l_gu1[0], l_gu1[1], l_d1[0], l_ow[1], l_ow[0], l_gu2[0], l_gu2[1], l_d2[0]]
    halves = [_chip_sum(p, l, mcidx, f"rs_chip_sum_{i}") for i, (p, l) in enumerate(zip(pair, from_chips))]
    full = _sibling_join(halves, "rs_sibling_join")
    f_wg1, f_wu1, f_wd1, f_win, f_wout, f_wg2, f_wu2, f_wd2 = full

    unpad_cols = lambda f: jnp.concatenate([f[0], f[1][:, :fs - hp]], axis=1)
    unpad_rows = lambda f: f.reshape(2 * f.shape[1], f.shape[2])[:fs]
    gw = {
        "ffn1_w_gate": unpad_cols(f_wg1), "ffn1_w_up": unpad_cols(f_wu1), "ffn1_w_down": unpad_rows(f_wd1),
        "w_in": lax.dynamic_slice(jnp.concatenate([f_win[0], f_win[1]], axis=1), (jnp.int32(0), shift), (d, ws)),
        "w_out": f_wout.reshape(w_out.shape),
        "ffn2_w_gate": unpad_cols(f_wg2), "ffn2_w_up": unpad_cols(f_wu2), "ffn2_w_down": unpad_rows(f_wd2),
    }

    def lanes(v):
        v = v.reshape(-1)
        return jnp.pad(v, (0, _ceil_to(v.shape[0], LANE) - v.shape[0])).reshape(-1, LANE)

    small = [dw1_p.sum(0), dwm_p.sum(0), dw3_p.sum(0), dwf_p.sum(0), ddnn, dalog, ddtb,
             (0.5 / d) * jnp.sum(lsq_p).reshape(1), dconv]
    rows = [lanes(v) for v in small]
    offs = np.cumsum([0] + [r.shape[0] for r in rows])
    packed = jnp.concatenate(rows, axis=0)
    packed = jnp.pad(packed, ((0, _ceil_to(packed.shape[0], 8) - packed.shape[0]), (0, 0)))
    red = _allreduce_small(packed, "allreduce_small")
    take = lambda i, shape: red[offs[i]:offs[i + 1]].reshape(-1)[:int(np.prod(shape))].reshape(shape)
    gw["ffn1_norm"] = take(0, (d,))
    gw["mix_norm"] = take(1, (d,))
    gw["ffn2_norm"] = take(2, (d,))
    gw["final_norm"] = take(3, (d,))
    gw["dn_norm"] = take(4, dn_norm.shape)
    gw["a_log"] = take(5, a_log.shape)
    gw["dt_bias"] = take(6, dt_bias.shape)
    loss = take(7, (1,)).reshape(())
    gw["conv_w"] = lax.dynamic_slice(take(8, (CONV_WIDTH, 3 * d_dn)), (jnp.int32(0), me * cs), (CONV_WIDTH, cs))

    names = ['ffn1_norm', 'ffn1_w_gate', 'ffn1_w_up', 'ffn1_w_down', 'mix_norm', 'w_in', 'conv_w', 'a_log', 'dt_bias',
             'dn_norm', 'w_out', 'ffn2_norm', 'ffn2_w_gate', 'ffn2_w_up', 'ffn2_w_down', 'final_norm']
    wv = dict(zip(names, (ffn1_norm, ffn1_w_gate, ffn1_w_up, ffn1_w_down, mix_norm, w_in, conv_w, a_log, dt_bias,
                          dn_norm, w_out, ffn2_norm, ffn2_w_gate, ffn2_w_up, ffn2_w_down, final_norm)))
    mv = dict(zip(names, (m_ffn1_norm, m_ffn1_w_gate, m_ffn1_w_up, m_ffn1_w_down, m_mix_norm, m_w_in, m_conv_w, m_a_log,
                          m_dt_bias, m_dn_norm, m_w_out, m_ffn2_norm, m_ffn2_w_gate, m_ffn2_w_up, m_ffn2_w_down,
                          m_final_norm)))
    vv = dict(zip(names, (v_ffn1_norm, v_ffn1_w_gate, v_ffn1_w_up, v_ffn1_w_down, v_mix_norm, v_w_in, v_conv_w, v_a_log,
                          v_dt_bias, v_dn_norm, v_w_out, v_ffn2_norm, v_ffn2_w_gate, v_ffn2_w_up, v_ffn2_w_down,
                          v_final_norm)))
    delta, new_m, new_v = {}, {}, {}
    small_names = [n for n in names if wv[n].ndim == 1 or n == "conv_w"]
    for n in names:
        if n in small_names:
            continue
        delta[n], new_m[n], new_v[n] = _adamw(gw[n], wv[n], mv[n], vv[n], f"adamw_{n}")
    srows = {n: lanes(gw[n]).shape[0] for n in small_names}
    soffs = np.cumsum([0] + [srows[n] for n in small_names])
    stot = _ceil_to(int(soffs[-1]), 8)

    def pack(dct):
        p = jnp.concatenate([lanes(dct[n]) for n in small_names], axis=0)
        return jnp.pad(p, ((0, stot - p.shape[0]), (0, 0)))

    sd, sm, sv = _adamw(pack(gw), pack(wv), pack(mv), pack(vv), "adamw_small")
    for i, n in enumerate(small_names):
        cut = lambda p: p[soffs[i]:soffs[i + 1]].reshape(-1)[:wv[n].size].reshape(wv[n].shape)
        delta[n], new_m[n], new_v[n] = cut(sd), cut(sm), cut(sv)

    return (loss, grad_x, *[gw[n] for n in names], *[delta[n] for n in names], *[new_m[n] for n in names],
            *[new_v[n] for n in names])
```

```python
import functools
import math

import jax
import jax.numpy as jnp
import numpy as np
from jax import lax
from jax.experimental import pallas as pl
from jax.experimental.pallas import tpu as pltpu

F32 = jnp.float32
BF16 = jnp.bfloat16
MESH = pl.DeviceIdType.MESH
ANY = pl.BlockSpec(memory_space=pl.ANY)

LANE = 128
N_CHIPS = 4
N_DEV = 8
EPS = 1e-6
HEAD_DIM = 128
CONV_WIDTH = 4
CHUNK = 64
ATTN_BLOCK = 128
DILATED_CONFIGS = ((128, 1), (512, 4), (2048, 16))
VMEM_LIMIT = 52 * 1024 * 1024

ADAM_LR = 0.001
ADAM_B1 = 0.9
ADAM_B2 = 0.999
ADAM_EPS = 1e-08
ADAM_WD = 0.01
ADAM_STEP = 10

NN = (((1,), (0,)), ((), ()))
NT = (((1,), (1,)), ((), ()))
TN = (((0,), (0,)), ((), ()))


def _ceil_to(v, m):
    return -(-v // m) * m


def _params(vmem=VMEM_LIMIT):
    return pltpu.CompilerParams(vmem_limit_bytes=vmem)


class _Comm:
    def __init__(self, ins, out_shape, aliases, sems, start, mid, finish):
        self.ins, self.out_shape, self.aliases, self.sems = ins, out_shape, aliases, sems
        self.start, self.mid, self.finish = start, mid, finish


def _gemm(name, grid, pairs, dn, acc_shape, n_acc, extras, outs, epilogue, comm=None):
    n_pairs, n_ex, n_out = len(pairs), len(extras), len(outs)
    n_ci = len(comm.ins) if comm else 0
    n_co = len(comm.out_shape) if comm else 0
    n_sem = len(comm.sems) if comm else 0
    kax = len(grid) - 1
    nk = grid[kax]
    n_in = 2 * n_pairs + n_ex

    def body(*refs):
        ins = refs[: 2 * n_pairs]
        ex = refs[2 * n_pairs: n_in]
        c_in = refs[n_in: n_in + n_ci]
        out_refs = refs[n_in + n_ci: n_in + n_ci + n_out]
        c_out = refs[n_in + n_ci + n_out: n_in + n_ci + n_out + n_co]
        accs = refs[n_in + n_ci + n_out + n_co: n_in + n_ci + n_out + n_co + n_acc]
        sems = refs[n_in + n_ci + n_out + n_co + n_acc:]
        k = pl.program_id(kax)
        pids = [pl.program_id(a) for a in range(len(grid))]

        def at(point):
            cond = pids[0] == point[0]
            for pid, v in zip(pids[1:], point[1:]):
                cond = cond & (pid == v)
            return cond

        if comm:
            @pl.when(at([0] * len(grid)))
            def _():
                comm.start(c_in, c_out, sems)

            if comm.mid:
                @pl.when(at([grid[0] * 3 // 4] + [0] * (len(grid) - 1)))
                def _():
                    comm.mid(c_in, c_out, sems)

        @pl.when(k == 0)
        def _():
            for acc in accs:
                acc[...] = jnp.zeros(acc.shape, F32)

        for q in range(n_pairs):
            a = ins[2 * q][...]
            b = ins[2 * q + 1][...]
            if a.dtype != BF16:
                a = a.astype(BF16)
            if b.dtype != BF16:
                b = b.astype(BF16)
            accs[pairs[q][4]][...] += lax.dot_general(a, b, dn, preferred_element_type=F32)

        @pl.when(k == nk - 1)
        def _():
            res = epilogue([acc[...] for acc in accs], [e[...] for e in ex])
            for o, r in zip(out_refs, res):
                o[...] = r.astype(o.dtype)

        if comm:
            @pl.when(at([g - 1 for g in grid]))
            def _():
                comm.finish(c_in, c_out, sems)

    in_specs = []
    args = []
    for a, a_spec, b, b_spec, _ in pairs:
        in_specs += [a_spec, b_spec]
        args += [a, b]
    for e, e_spec in extras:
        in_specs.append(e_spec)
        args.append(e)
    out_shape = [o for o, _ in outs]
    out_specs = [s for _, s in outs]
    scratch = [pltpu.VMEM(acc_shape, F32) for _ in range(n_acc)]
    kwargs = {}
    if comm:
        in_specs += [ANY] * n_ci
        args += list(comm.ins)
        out_shape += list(comm.out_shape)
        out_specs += [ANY] * n_co
        scratch += list(comm.sems)
        kwargs["input_output_aliases"] = {n_in + i: n_out + o for i, o in comm.aliases.items()}
    res = pl.pallas_call(body, name=name, grid=grid, in_specs=in_specs, out_specs=out_specs,
                         out_shape=out_shape, scratch_shapes=scratch, compiler_params=_params(), **kwargs)(*args)
    if comm:
        return list(res[:n_out]), list(res[n_out:])
    return res


def _unpack(res, comm, single):
    if comm:
        outs, couts = res
        return (outs[0] if single else outs), couts
    return res[0] if single else res


def _call_carrying(body, name, grid, in_specs, out_specs, out_shape, scratch, args, comm):
    n_in, n_out, n_scr = len(in_specs), len(out_shape), len(scratch)
    n_ci, n_co = len(comm.ins), len(comm.out_shape)

    def wrapped(*refs):
        ins, c_in = refs[:n_in], refs[n_in:n_in + n_ci]
        outs = refs[n_in + n_ci:n_in + n_ci + n_out]
        c_out = refs[n_in + n_ci + n_out:n_in + n_ci + n_out + n_co]
        scr = refs[n_in + n_ci + n_out + n_co:n_in + n_ci + n_out + n_co + n_scr]
        sems = refs[n_in + n_ci + n_out + n_co + n_scr:]
        pids = [pl.program_id(a) for a in range(len(grid))]

        def at(point):
            cond = pids[0] == point[0]
            for pid, v in zip(pids[1:], point[1:]):
                cond = cond & (pid == v)
            return cond

        @pl.when(at([0] * len(grid)))
        def _():
            comm.start(c_in, c_out, sems)

        if comm.mid:
            @pl.when(at([grid[0] * 3 // 4] + [0] * (len(grid) - 1)))
            def _():
                comm.mid(c_in, c_out, sems)

        body(*ins, *outs, *scr)

        @pl.when(at([g - 1 for g in grid]))
        def _():
            comm.finish(c_in, c_out, sems)

    res = pl.pallas_call(
        wrapped, name=name, grid=grid, in_specs=list(in_specs) + [ANY] * n_ci, out_specs=list(out_specs) + [ANY] * n_co,
        out_shape=list(out_shape) + list(comm.out_shape), scratch_shapes=list(scratch) + list(comm.sems),
        input_output_aliases={n_in + i: n_out + o for i, o in comm.aliases.items()},
        compiler_params=_params())(*args, *comm.ins)
    return list(res[:n_out]), list(res[n_out:])


def _pick(n, prefs):
    for p in prefs:
        if n % p == 0:
            return p
    return n


def _sigmoid(v):
    return 1.0 / (1.0 + jnp.exp(-v))


def _ffn_up(n, wg, wu, name, comm=None):
    t, d = n.shape
    npieces, _, hp = wg.shape
    tm = _pick(t, (512, 256, 128, 64, 32, 16))
    tk = _pick(d, (1024, 512, 256, 128))
    grid = (t // tm, npieces, d // tk)
    a_spec = pl.BlockSpec((tm, tk), lambda i, p, k: (i, k))
    w_spec = pl.BlockSpec((None, tk, hp), lambda i, p, k: (p, k, 0))
    o_spec = pl.BlockSpec((tm, hp), lambda i, p, k: (i, p))
    osd = jax.ShapeDtypeStruct((t, npieces * hp), BF16)

    def epi(accs, ex):
        a, b = accs
        return a, b, a * _sigmoid(a) * b

    return _unpack(_gemm(name, grid, [(n, a_spec, wg, w_spec, 0), (n, a_spec, wu, w_spec, 1)], NN, (tm, hp), 2, [],
                         [(osd, o_spec)] * 3, epi, comm), comm, False)


def _mm_pieces_resid(a, w, resid, scale, amap, name, comm=None):
    t = a.shape[0]
    npieces, kp, n = w.shape
    tm = _pick(t, (1024, 512, 256, 128, 64, 32, 16))
    tn = _pick(n, (1024, 512, 256, 128))
    grid = (t // tm, n // tn, npieces)
    a_spec = pl.BlockSpec((tm, kp), lambda i, j, p: (i, amap(p)))
    w_spec = pl.BlockSpec((None, kp, tn), lambda i, j, p: (p, 0, j))
    r_spec = pl.BlockSpec((tm, tn), lambda i, j, p: (i, j))

    def epi(accs, ex):
        return (ex[0] + scale * accs[0],)

    return _unpack(_gemm(name, grid, [(a, a_spec, w, w_spec, 0)], NN, (tm, tn), 1, [(resid, r_spec)],
                         [(jax.ShapeDtypeStruct((t, n), F32), r_spec)], epi, comm), comm, True)


def _ffn_bwd_hidden(dh, wd, a, b, name):
    t, d = dh.shape
    npieces, hp, _ = wd.shape
    tm = _pick(t, (512, 256, 128, 64, 32, 16))
    tk = _pick(d, (1024, 512, 256, 128))
    grid = (t // tm, npieces, d // tk)
    a_spec = pl.BlockSpec((tm, tk), lambda i, p, k: (i, k))
    w_spec = pl.BlockSpec((None, hp, tk), lambda i, p, k: (p, 0, k))
    o_spec = pl.BlockSpec((tm, hp), lambda i, p, k: (i, p))
    osd = jax.ShapeDtypeStruct((t, npieces * hp), BF16)

    def epi(accs, ex):
        ds = 0.5 * accs[0]
        av = ex[0].astype(F32)
        bv = ex[1].astype(F32)
        sg = _sigmoid(av)
        da = ds * bv * (sg * (1.0 + av * (1.0 - sg)))
        db = ds * (av * sg)
        return da, db

    return _gemm(name, grid, [(dh, a_spec, wd, w_spec, 0)], NT, (tm, hp), 1, [(a, o_spec), (b, o_spec)],
                 [(osd, o_spec)] * 2, epi)


def _mm_nt_pieces_out(dh, w, omap, name):
    t, d = dh.shape
    npieces, npp, _ = w.shape
    tm = _pick(t, (1024, 512, 256, 128, 64, 32, 16))
    tk = _pick(d, (1024, 512, 256, 128))
    grid = (t // tm, npieces, d // tk)
    a_spec = pl.BlockSpec((tm, tk), lambda i, p, k: (i, k))
    w_spec = pl.BlockSpec((None, npp, tk), lambda i, p, k: (p, 0, k))
    o_spec = pl.BlockSpec((tm, npp), lambda i, p, k: (i, omap(p)))
    return _gemm(name, grid, [(dh, a_spec, w, w_spec, 0)], NT, (tm, npp), 1, [],
                 [(jax.ShapeDtypeStruct((t, npieces * npp), BF16), o_spec)], lambda accs, ex: (accs[0],))[0]


def _grad_rows_pieces(x, dy, scale, amap, npieces, name):
    t, n = dy.shape
    mp = x.shape[1] // npieces
    tn = _pick(n, (1024, 512, 256, 128))
    tk = _pick(t, (1024, 512, 256, 128, 64, 32, 16))
    grid = (npieces, n // tn, t // tk)
    x_spec = pl.BlockSpec((tk, mp), lambda p, j, k: (k, amap(p)))
    y_spec = pl.BlockSpec((tk, tn), lambda p, j, k: (k, j))
    o_spec = pl.BlockSpec((None, mp, tn), lambda p, j, k: (p, 0, j))
    return _gemm(name, grid, [(x, x_spec, dy, y_spec, 0)], TN, (mp, tn), 1, [],
                 [(jax.ShapeDtypeStruct((npieces, mp, n), BF16), o_spec)], lambda accs, ex: (scale * accs[0],))[0]


def _grad_cols_pieces(n, da, db, npieces, name, comm=None):
    t, d = n.shape
    hp = da.shape[1] // npieces
    tm = _pick(d, (1024, 512, 256, 128))
    tk = _pick(t, (512, 256, 128, 64, 32, 16))
    grid = (npieces, d // tm, t // tk)
    n_spec = pl.BlockSpec((tk, tm), lambda p, i, k: (k, i))
    g_spec = pl.BlockSpec((tk, hp), lambda p, i, k: (k, p))
    o_spec = pl.BlockSpec((None, tm, hp), lambda p, i, k: (p, i, 0))
    osd = jax.ShapeDtypeStruct((npieces, d, hp), BF16)
    return _unpack(_gemm(name, grid, [(n, n_spec, da, g_spec, 0), (n, n_spec, db, g_spec, 1)], TN, (tm, hp), 2, [],
                         [(osd, o_spec)] * 2, lambda accs, ex: (accs[0], accs[1]), comm), comm, False)


def _ffn_bwd_input(da, db, wg, wu, name, comm=None):
    t = da.shape[0]
    npieces, d, hp = wg.shape
    tm = _pick(t, (1024, 512, 256, 128, 64, 32, 16))
    tn = _pick(d, (1024, 512, 256, 128))
    grid = (t // tm, d // tn, npieces)
    g_spec = pl.BlockSpec((tm, hp), lambda i, j, p: (i, p))
    w_spec = pl.BlockSpec((None, tn, hp), lambda i, j, p: (p, j, 0))
    o_spec = pl.BlockSpec((tm, tn), lambda i, j, p: (i, j))
    return _unpack(_gemm(name, grid, [(da, g_spec, wg, w_spec, 0), (db, g_spec, wu, w_spec, 0)], NT, (tm, tn), 1, [],
                         [(jax.ShapeDtypeStruct((t, d), F32), o_spec)], lambda accs, ex: (accs[0],), comm), comm, True)


def _mm2d(a, b, dn, out_dtype, name, tiles, comm=None):
    if dn == NN:
        m, kk = a.shape
        n = b.shape[1]
    elif dn == NT:
        m, kk = a.shape
        n = b.shape[0]
    else:
        kk, m = a.shape
        n = b.shape[1]
    tm = _pick(m, (tiles[0],) + (1024, 512, 256, 128, 64, 32, 16))
    tn = _pick(n, (tiles[1], 768, 1024, 512, 256, 128))
    tk = _pick(kk, (tiles[2], 768, 1024, 512, 256, 128, 64, 32, 16))
    grid = (m // tm, n // tn, kk // tk)
    if dn == TN:
        a_spec = pl.BlockSpec((tk, tm), lambda i, j, k: (k, i))
    else:
        a_spec = pl.BlockSpec((tm, tk), lambda i, j, k: (i, k))
    if dn == NT:
        b_spec = pl.BlockSpec((tn, tk), lambda i, j, k: (j, k))
    else:
        b_spec = pl.BlockSpec((tk, tn), lambda i, j, k: (k, j))
    o_spec = pl.BlockSpec((tm, tn), lambda i, j, k: (i, j))
    return _unpack(_gemm(name, grid, [(a, a_spec, b, b_spec, 0)], dn, (tm, tn), 1, [],
                         [(jax.ShapeDtypeStruct((m, n), out_dtype), o_spec)], lambda accs, ex: (accs[0],), comm), comm, True)


def _row_tile(t):
    return _pick(t, (256, 128, 64, 32, 16, 8))


def _rms_fwd(x, w, name):
    t, d = x.shape
    tm = _row_tile(t)

    def body(x_ref, w_ref, o_ref):
        xv = x_ref[...]
        r = lax.rsqrt(jnp.mean(xv * xv, axis=-1, keepdims=True) + EPS)
        o_ref[...] = (xv * r * w_ref[...]).astype(BF16)

    return pl.pallas_call(
        body, name=name, grid=(t // tm,),
        in_specs=[pl.BlockSpec((tm, d), lambda i: (i, 0)), pl.BlockSpec((1, d), lambda i: (0, 0))],
        out_specs=pl.BlockSpec((tm, d), lambda i: (i, 0)),
        out_shape=jax.ShapeDtypeStruct((t, d), BF16), compiler_params=_params())(x, w.reshape(1, d))


def _rms_bwd(dn, x, w, dres, name):
    t, d = x.shape
    tm = _row_tile(t)

    def body(dn_ref, x_ref, w_ref, r_ref, o_ref, ob_ref, dw_ref):
        i = pl.program_id(0)
        xv = x_ref[...]
        r = lax.rsqrt(jnp.mean(xv * xv, axis=-1, keepdims=True) + EPS)
        xh = xv * r
        dy = dn_ref[...].astype(F32)
        g = dy * w_ref[...]
        dx = r * (g - xh * jnp.mean(g * xh, axis=-1, keepdims=True))
        tot = r_ref[...] + dx
        o_ref[...] = tot
        ob_ref[...] = tot.astype(BF16)
        part = (dy * xh).reshape(tm // 8, 8, d).sum(axis=0)

        @pl.when(i == 0)
        def _():
            dw_ref[...] = part

        @pl.when(i > 0)
        def _():
            dw_ref[...] += part

    row = pl.BlockSpec((tm, d), lambda i: (i, 0))
    return pl.pallas_call(
        body, name=name, grid=(t // tm,),
        in_specs=[row, row, pl.BlockSpec((1, d), lambda i: (0, 0)), row],
        out_specs=[row, row, pl.BlockSpec((8, d), lambda i: (0, 0))],
        out_shape=[jax.ShapeDtypeStruct((t, d), F32), jax.ShapeDtypeStruct((t, d), BF16),
                   jax.ShapeDtypeStruct((8, d), F32)],
        compiler_params=_params())(dn, x, w.reshape(1, d), dres)


def _final_loss(h, w, target, name):
    t, d = h.shape
    tm = _row_tile(t)

    def body(h_ref, w_ref, t_ref, o_ref, ob_ref, dw_ref, ls_ref):
        i = pl.program_id(0)
        xv = h_ref[...]
        r = lax.rsqrt(jnp.mean(xv * xv, axis=-1, keepdims=True) + EPS)
        xh = xv * r
        err = xh * w_ref[...] - t_ref[...]
        dy = err * (1.0 / d)
        g = dy * w_ref[...]
        dx = r * (g - xh * jnp.mean(g * xh, axis=-1, keepdims=True))
        o_ref[...] = dx
        ob_ref[...] = dx.astype(BF16)
        part = (dy * xh).reshape(tm // 8, 8, d).sum(axis=0)
        lpart = (err * err).reshape(tm // 8, 8, d).sum(axis=0)

        @pl.when(i == 0)
        def _():
            dw_ref[...] = part
            ls_ref[...] = lpart

        @pl.when(i > 0)
        def _():
            dw_ref[...] += part
            ls_ref[...] += lpart

    row = pl.BlockSpec((tm, d), lambda i: (i, 0))
    acc = pl.BlockSpec((8, d), lambda i: (0, 0))
    return pl.pallas_call(
        body, name=name, grid=(t // tm,),
        in_specs=[row, pl.BlockSpec((1, d), lambda i: (0, 0)), row],
        out_specs=[row, row, acc, acc],
        out_shape=[jax.ShapeDtypeStruct((t, d), F32), jax.ShapeDtypeStruct((t, d), BF16),
                   jax.ShapeDtypeStruct((8, d), F32), jax.ShapeDtypeStruct((8, d), F32)],
        compiler_params=_params())(h, w.reshape(1, d), target)


def _cast_split_cols(w, hp, me, name):
    r, fs = w.shape
    v1 = fs - hp
    tm = _pick(r, (256, 128, 64, 32, 16))

    def body(me_ref, w_ref, o_ref):
        o_ref[0] = w_ref[:, :hp].astype(BF16)
        if v1 < hp:
            o_ref[1] = jnp.zeros((tm, hp), BF16)
        o_ref[1, :, :v1] = w_ref[:, hp:].astype(BF16)

    gs = pltpu.PrefetchScalarGridSpec(
        num_scalar_prefetch=1, grid=(r // tm,),
        in_specs=[pl.BlockSpec((tm, fs), lambda i, mr: (i, 0))],
        out_specs=pl.BlockSpec((2, None, tm, hp), lambda i, mr: (0, mr[0], i, 0)))
    return pl.pallas_call(body, name=name, grid_spec=gs, out_shape=jax.ShapeDtypeStruct((2, N_CHIPS, r, hp), BF16),
                          compiler_params=_params())(me, w)


def _cast_split_cols_t(wt, hp, me, name):
    fs, r = wt.shape
    v1 = fs - hp
    tm = _pick(r, (256, 128))

    def body(me_ref, w_ref, o_ref):
        o_ref[0] = w_ref[:hp, :].T.astype(BF16)
        if v1 < hp:
            o_ref[1] = jnp.zeros((tm, hp), BF16)
        o_ref[1, :, :v1] = w_ref[hp:, :].T.astype(BF16)

    gs = pltpu.PrefetchScalarGridSpec(
        num_scalar_prefetch=1, grid=(r // tm,),
        in_specs=[pl.BlockSpec((fs, tm), lambda i, mr: (0, i))],
        out_specs=pl.BlockSpec((2, None, tm, hp), lambda i, mr: (0, mr[0], i, 0)))
    return pl.pallas_call(body, name=name, grid_spec=gs, out_shape=jax.ShapeDtypeStruct((2, N_CHIPS, r, hp), BF16),
                          compiler_params=_params())(me, wt)


def _cast_split_rows(w, hp, tr, me, name):
    fs, c = w.shape
    nvalid = fs // tr
    per = hp // tr

    def body(me_ref, w_ref, o_ref):
        i = pl.program_id(0)

        @pl.when(i < nvalid)
        def _():
            o_ref[...] = w_ref[...].astype(BF16)

        @pl.when(i >= nvalid)
        def _():
            o_ref[...] = jnp.zeros(o_ref.shape, BF16)

    gs = pltpu.PrefetchScalarGridSpec(
        num_scalar_prefetch=1, grid=(2 * per,),
        in_specs=[pl.BlockSpec((tr, c), lambda i, mr: (jnp.minimum(i, nvalid - 1), 0))],
        out_specs=pl.BlockSpec((None, None, tr, c), lambda i, mr: (i // per, mr[0], i % per, 0)))
    return pl.pallas_call(body, name=name, grid_spec=gs, out_shape=jax.ShapeDtypeStruct((2, N_CHIPS, hp, c), BF16),
                          compiler_params=_params())(me, w)


def _combine_windows(wall, tables, n_tiles, name):
    _, _, d, wh = wall.shape
    tpw = wh // LANE

    def body(tab_ref, a_ref, b_ref, o_ref):
        t = pl.program_id(0)
        both = tab_ref[6, t] == 1
        av = a_ref[...]
        bv = b_ref[...]
        o_ref[...] = jnp.where(both, av + bv, av)

    def amap(t, tab):
        return (tab[0, t], tab[1, t], 0, tab[2, t])

    def bmap(t, tab):
        return (tab[3, t], tab[4, t], 0, tab[5, t])

    gs = pltpu.PrefetchScalarGridSpec(
        num_scalar_prefetch=1, grid=(n_tiles,),
        in_specs=[pl.BlockSpec((None, None, d, LANE), amap), pl.BlockSpec((None, None, d, LANE), bmap)],
        out_specs=pl.BlockSpec((d, LANE), lambda t, tab: (0, t)))
    del tpw
    return pl.pallas_call(body, name=name, grid_spec=gs, out_shape=jax.ShapeDtypeStruct((d, n_tiles * LANE), BF16),
                          compiler_params=_params())(tables, wall, wall)


def _coords():
    return lax.axis_index("x"), lax.axis_index("y"), lax.axis_index("c")


def _remote(src, dst, ssem, rsem, dev):
    return pltpu.make_async_remote_copy(src_ref=src, dst_ref=dst, send_sem=ssem, recv_sem=rsem, device_id=dev,
                                        device_id_type=MESH)


def _mesh_places():
    x, y, c = _coords()
    return c, 2 * x + y, (x, y, 1 - c), [(1 - x, y), (x, 1 - y), (1 - x, 1 - y)]


def _all_gather_comm(bufs):
    n = len(bufs)

    def start(ins, outs, sems):
        c, me, _, chips = _mesh_places()
        for i in range(n):
            for j, (px, py) in enumerate(chips):
                mine = outs[i].at[c, me]
                _remote(mine, mine, sems[0].at[i, j], sems[1].at[i, j], (px, py, c)).start()

    def mid(ins, outs, sems):
        c, _, sib, chips = _mesh_places()
        for i in range(n):
            for j, (px, py) in enumerate(chips):
                slot = outs[i].at[c, 2 * px + py]
                _remote(slot, slot, sems[0].at[i, j], sems[1].at[i, j], (px, py, c)).wait_recv()
                _remote(slot, slot, sems[2].at[i, j], sems[3].at[i, j], sib).start()

    def finish(ins, outs, sems):
        c, me, sib, chips = _mesh_places()
        for i in range(n):
            for j, (px, py) in enumerate(chips):
                slot = outs[i].at[1 - c, 2 * px + py]
                _remote(slot, slot, sems[2].at[i, j], sems[3].at[i, j], sib).wait_recv()
        for i in range(n):
            for j, (px, py) in enumerate(chips):
                mine = outs[i].at[c, me]
                _remote(mine, mine, sems[0].at[i, j], sems[1].at[i, j], (px, py, c)).wait_send()
                slot = outs[i].at[c, 2 * px + py]
                _remote(slot, slot, sems[2].at[i, j], sems[3].at[i, j], sib).wait_send()

    return _Comm(list(bufs), [jax.ShapeDtypeStruct(b.shape, b.dtype) for b in bufs], {i: i for i in range(n)},
                 [pltpu.SemaphoreType.DMA((n, 3))] * 4, start, mid, finish)


def _chip_all_to_all_comm(ps):
    n = len(ps)

    def start(ins, outs, sems):
        c, me, _, chips = _mesh_places()
        for i in range(n):
            for j, (px, py) in enumerate(chips):
                _remote(ins[i].at[2 * px + py], outs[i].at[me], sems[0].at[i, j], sems[1].at[i, j], (px, py, c)).start()

    def finish(ins, outs, sems):
        c, me, _, chips = _mesh_places()
        for i in range(n):
            for j, (px, py) in enumerate(chips):
                slot = outs[i].at[2 * px + py]
                _remote(slot, slot, sems[0].at[i, j], sems[1].at[i, j], (px, py, c)).wait_recv()
        for i in range(n):
            for j, (px, py) in enumerate(chips):
                _remote(ins[i].at[2 * px + py], outs[i].at[me], sems[0].at[i, j], sems[1].at[i, j],
                        (px, py, c)).wait_send()

    return _Comm(list(ps), [jax.ShapeDtypeStruct(p.shape, p.dtype) for p in ps], {},
                 [pltpu.SemaphoreType.DMA((n, 3))] * 2, start, None, finish)


def _comm_call(comm, name):
    n_in, n_out = len(comm.ins), len(comm.out_shape)

    def body(*refs):
        ins, outs, sems = refs[:n_in], refs[n_in:n_in + n_out], refs[n_in + n_out:]
        comm.start(ins, outs, sems)
        if comm.mid:
            comm.mid(ins, outs, sems)
        comm.finish(ins, outs, sems)

    return pl.pallas_call(body, name=name, in_specs=[ANY] * n_in, out_specs=[ANY] * n_out, out_shape=comm.out_shape,
                          input_output_aliases=dict(comm.aliases), scratch_shapes=list(comm.sems))(*comm.ins)


def _sibling_take(gs, name):
    n = len(gs)

    def body(*refs):
        g, out = refs[:n], refs[n:2 * n]
        ssem, rsem = refs[2 * n:]
        x, y, c = _coords()
        sib = (x, y, 1 - c)
        cps = []
        for i in range(n):
            cp = _remote(g[i].at[1 - c], out[i], ssem.at[i], rsem.at[i], sib)
            cp.start()
            cps.append(cp)
        for cp in cps:
            cp.wait()

    out_shape = [jax.ShapeDtypeStruct(s.shape[1:], s.dtype) for s in gs]
    return pl.pallas_call(
        body, name=name, in_specs=[ANY] * n, out_specs=[ANY] * n, out_shape=out_shape,
        scratch_shapes=[pltpu.SemaphoreType.DMA((n,)), pltpu.SemaphoreType.DMA((n,))])(*gs)


def _sibling_join(bufs, name):
    n = len(bufs)

    def body(*refs):
        out = refs[n:2 * n]
        ssem, rsem = refs[2 * n:]
        x, y, c = _coords()
        sib = (x, y, 1 - c)
        cps = []
        for i in range(n):
            mine = out[i].at[c]
            cp = _remote(mine, mine, ssem.at[i], rsem.at[i], sib)
            cp.start()
            cps.append(cp)
        for i in range(n):
            slot = out[i].at[1 - c]
            _remote(slot, slot, ssem.at[i], rsem.at[i], sib).wait_recv()
        for cp in cps:
            cp.wait_send()

    out_shape = [jax.ShapeDtypeStruct(b.shape, b.dtype) for b in bufs]
    return pl.pallas_call(
        body, name=name, in_specs=[ANY] * n, out_specs=[ANY] * n, out_shape=out_shape,
        input_output_aliases={i: i for i in range(n)},
        scratch_shapes=[pltpu.SemaphoreType.DMA((n,)), pltpu.SemaphoreType.DMA((n,))])(*bufs)


def _allreduce_small(vec, name):
    r = vec.shape[0]

    def body(v_ref, o_ref, buf, ssem, rsem):
        x, y, c = _coords()
        my = 4 * x + 2 * y + c
        buf[my] = v_ref[...]
        cps = []
        for dd in range(1, N_DEV):
            px = 1 - x if (dd >> 2) & 1 else x
            py = 1 - y if (dd >> 1) & 1 else y
            pc = 1 - c if dd & 1 else c
            cp = _remote(v_ref, buf.at[my], ssem.at[dd - 1], rsem.at[dd - 1], (px, py, pc))
            cp.start()
            cps.append(cp)
        for dd in range(1, N_DEV):
            px = 1 - x if (dd >> 2) & 1 else x
            py = 1 - y if (dd >> 1) & 1 else y
            pc = 1 - c if dd & 1 else c
            slot = buf.at[4 * px + 2 * py + pc]
            _remote(slot, slot, ssem.at[dd - 1], rsem.at[dd - 1], (px, py, pc)).wait_recv()
        tot = buf[0]
        for k in range(1, N_DEV):
            tot = tot + buf[k]
        o_ref[...] = tot
        for cp in cps:
            cp.wait_send()

    vm = pl.BlockSpec(memory_space=pltpu.VMEM)
    return pl.pallas_call(
        body, name=name, in_specs=[vm], out_specs=vm, out_shape=jax.ShapeDtypeStruct((r, LANE), F32),
        scratch_shapes=[pltpu.VMEM((N_DEV, r, LANE), F32), pltpu.SemaphoreType.DMA((N_DEV - 1,)),
                        pltpu.SemaphoreType.DMA((N_DEV - 1,))])(vec)


def _pair_sum(g, l1, cidx, name):
    _, r, c = g.shape
    tr = _pick(r, (512, 256, 128, 64, 32, 16))

    def body(c_ref, g_ref, l_ref, o_ref):
        o_ref[...] = (g_ref[...].astype(F32) + l_ref[...].astype(F32)).astype(BF16)

    gs = pltpu.PrefetchScalarGridSpec(
        num_scalar_prefetch=1, grid=(r // tr,),
        in_specs=[pl.BlockSpec((None, tr, c), lambda i, cr: (cr[0], i, 0)), pl.BlockSpec((tr, c), lambda i, cr: (i, 0))],
        out_specs=pl.BlockSpec((tr, c), lambda i, cr: (i, 0)))
    return pl.pallas_call(body, name=name, grid_spec=gs, out_shape=jax.ShapeDtypeStruct((r, c), BF16),
                          compiler_params=_params())(cidx, g, l1)


def _chip_sum(p, l2, mc, name):
    _, r, c = l2.shape
    tr = _pick(r, (256, 128, 64, 32, 16))

    def body(mc_ref, p_ref, l0, l1, l2_, l3, o_ref):
        me = mc_ref[0]
        pv = p_ref[...].astype(F32)
        tot = None
        for k, lr in enumerate((l0, l1, l2_, l3)):
            term = jnp.where(me == k, pv, lr[...].astype(F32))
            tot = term if tot is None else tot + term
        o_ref[...] = tot

    def other(k):
        return lambda i, mr: (jnp.where(mr[0] == k, (k + 1) % N_CHIPS, k), i, 0)

    gs = pltpu.PrefetchScalarGridSpec(
        num_scalar_prefetch=1, grid=(r // tr,),
        in_specs=[pl.BlockSpec((None, tr, c), lambda i, mr: (mr[0], i, 0))]
        + [pl.BlockSpec((None, tr, c), other(k)) for k in range(N_CHIPS)],
        out_specs=pl.BlockSpec((None, tr, c), lambda i, mr: (mr[1], i, 0)))
    return pl.pallas_call(body, name=name, grid_spec=gs, out_shape=jax.ShapeDtypeStruct((2, r, c), F32),
                          compiler_params=_params())(mc, p, l2, l2, l2, l2)


def _adamw(g, w, m, v, name):
    r, c = w.shape
    tr = r
    if r * c * 4 > (2 << 20):
        tr = next(p for p in (256, 128, 64, 32, 16, 8) if r % p == 0 and (p * c * 4 <= (2 << 20) or p == 8))

    def body(g_ref, w_ref, m_ref, v_ref, d_ref, nm_ref, nv_ref):
        gv = g_ref[...]
        mn = ADAM_B1 * m_ref[...] + (1.0 - ADAM_B1) * gv
        vn = ADAM_B2 * v_ref[...] + (1.0 - ADAM_B2) * (gv * gv)
        m_hat = mn / (1.0 - ADAM_B1 ** ADAM_STEP)
        v_hat = vn / (1.0 - ADAM_B2 ** ADAM_STEP)
        d_ref[...] = -ADAM_LR * (m_hat / (jnp.sqrt(v_hat) + ADAM_EPS) + ADAM_WD * w_ref[...])
        nm_ref[...] = mn
        nv_ref[...] = vn

    blk = pl.BlockSpec((tr, c), lambda i: (i, 0))
    osd = jax.ShapeDtypeStruct((r, c), F32)
    return pl.pallas_call(body, name=name, grid=(r // tr,), in_specs=[blk] * 4, out_specs=[blk] * 3,
                          out_shape=[osd] * 3, compiler_params=_params())(g, w, m, v)


def _attn_probs(q, k, q0, s_len):
    tq = q.shape[0]
    sc = lax.dot_general(q, k, NT, preferred_element_type=F32) * (HEAD_DIM ** -0.5)
    dlt = (q0 + lax.broadcasted_iota(jnp.int32, (tq, s_len), 0)) - lax.broadcasted_iota(jnp.int32, (tq, s_len), 1)
    cnt = jnp.zeros((tq, s_len), F32)
    for window, dil in DILATED_CONFIGS:
        seen = (dlt >= 0) & (dlt <= window) & ((dlt & (dil - 1)) == 0)
        cnt = cnt + jnp.where(seen, 1.0, 0.0)
    live = cnt > 0.0
    m = jnp.max(jnp.where(live, sc, -jnp.inf), axis=-1, keepdims=True)
    p = cnt * jnp.exp(jnp.where(live, sc - m, -jnp.inf))
    return p / jnp.sum(p, axis=-1, keepdims=True)


def _attn_key_groups(nq):
    return next(g for g in (4, 2, 1) if nq % g == 0)


def _attn_fwd(proj, nh, s_len, name, comm):
    t = proj.shape[0]
    tq = min(256, s_len)
    nq = s_len // tq
    ng = _attn_key_groups(nq)
    per = nq // ng

    def body(q_ref, k_ref, v_ref, o_ref):
        qi = pl.program_id(2)
        for j in range(ng):
            klen = (j + 1) * per * tq

            @pl.when(qi // per == j)
            def _(klen=klen):
                p = _attn_probs(q_ref[...].astype(BF16), k_ref[:klen, :].astype(BF16), qi * tq, klen)
                o_ref[...] = jnp.dot(p.astype(BF16), v_ref[:klen, :].astype(BF16), preferred_element_type=F32)

    q_spec = pl.BlockSpec((tq, HEAD_DIM), lambda b, h, qi: (b * nq + qi, h))
    outs, couts = _call_carrying(
        body, name, (t // s_len, nh, nq),
        [q_spec, pl.BlockSpec((s_len, HEAD_DIM), lambda b, h, qi: (b, nh + h)),
         pl.BlockSpec((s_len, HEAD_DIM), lambda b, h, qi: (b, 2 * nh + h))],
        [q_spec], [jax.ShapeDtypeStruct((t, nh * HEAD_DIM), F32)], [], (proj, proj, proj), comm)
    return outs[0], couts


def _attn_bwd(proj, o, do, nh, s_len, name):
    t = proj.shape[0]
    tq = min(256, s_len)
    nq = s_len // tq
    ng = _attn_key_groups(nq)
    per = nq // ng
    scale = HEAD_DIM ** -0.5

    def body(q_ref, k_ref, v_ref, o_ref, do_ref, dq_ref, dk_ref, dv_ref, dk_acc, dv_acc):
        qi = pl.program_id(2)

        @pl.when(qi == 0)
        def _():
            dk_acc[...] = jnp.zeros(dk_acc.shape, F32)
            dv_acc[...] = jnp.zeros(dv_acc.shape, F32)

        for j in range(ng):
            klen = (j + 1) * per * tq

            @pl.when(qi // per == j)
            def _(klen=klen):
                q = q_ref[...].astype(BF16)
                k = k_ref[:klen, :].astype(BF16)
                p = _attn_probs(q, k, qi * tq, klen)
                dob = do_ref[...]
                dp = lax.dot_general(dob, v_ref[:klen, :].astype(BF16), NT, preferred_element_type=F32)
                delta = jnp.sum(dob.astype(F32) * o_ref[...], axis=-1, keepdims=True)
                ds = (p * (dp - delta)).astype(BF16)
                dq_ref[...] = (jnp.dot(ds, k, preferred_element_type=F32) * scale).astype(BF16)
                dk_acc[:klen, :] += lax.dot_general(ds, q, TN, preferred_element_type=F32) * scale
                dv_acc[:klen, :] += lax.dot_general(p.astype(BF16), dob, TN, preferred_element_type=F32)

        @pl.when(qi == nq - 1)
        def _():
            dk_ref[...] = dk_acc[...].astype(BF16)
            dv_ref[...] = dv_acc[...].astype(BF16)

    q_spec = pl.BlockSpec((tq, HEAD_DIM), lambda b, h, qi: (b * nq + qi, h))
    kv_out = pl.BlockSpec((s_len, HEAD_DIM), lambda b, h, qi: (b, h))
    osd = jax.ShapeDtypeStruct((t, nh * HEAD_DIM), BF16)
    return pl.pallas_call(
        body, name=name, grid=(t // s_len, nh, nq),
        in_specs=[q_spec, pl.BlockSpec((s_len, HEAD_DIM), lambda b, h, qi: (b, nh + h)),
                  pl.BlockSpec((s_len, HEAD_DIM), lambda b, h, qi: (b, 2 * nh + h)), q_spec, q_spec],
        out_specs=[q_spec, kv_out, kv_out], out_shape=[osd, osd, osd],
        scratch_shapes=[pltpu.VMEM((s_len, HEAD_DIM), F32), pltpu.VMEM((s_len, HEAD_DIM), F32)],
        compiler_params=_params())(proj, proj, proj, o, do)


def _conv_taps(x, w_ref, s_len):
    row = lax.broadcasted_iota(jnp.int32, x.shape, 0)
    c = w_ref[CONV_WIDTH - 1:CONV_WIDTH, :] * x
    for j in range(1, CONV_WIDTH):
        xs = jnp.where(row >= j, pltpu.roll(x, j, 0), 0.0)
        c = c + w_ref[CONV_WIDTH - 1 - j:CONV_WIDTH - j, :] * xs
    return c


def _conv_fwd(proj, conv8, col0, width, s_len, name):
    t = proj.shape[0]
    cb = _pick(width, (512, 256, 128))
    c0 = col0 // cb

    def body(x_ref, w_ref, o_ref):
        c = _conv_taps(x_ref[...], w_ref, s_len)
        o_ref[...] = c * _sigmoid(c)

    return pl.pallas_call(
        body, name=name, grid=(t // s_len, width // cb),
        in_specs=[pl.BlockSpec((s_len, cb), lambda b, j: (b, c0 + j)), pl.BlockSpec((8, cb), lambda b, j: (0, j))],
        out_specs=pl.BlockSpec((s_len, cb), lambda b, j: (b, j)),
        out_shape=jax.ShapeDtypeStruct((t, width), F32), compiler_params=_params())(proj, conv8)


def _conv_bwd(proj, conv8, du, col0, width, s_len, name):
    t = proj.shape[0]
    cb = _pick(width, (512, 256, 128))
    c0 = col0 // cb

    def body(x_ref, w_ref, du_ref, dx_ref, dw_ref):
        b = pl.program_id(1)
        x = x_ref[...]
        c = _conv_taps(x, w_ref, s_len)
        sg = _sigmoid(c)
        dc = du_ref[...] * (sg * (1.0 + c * (1.0 - sg)))
        row = lax.broadcasted_iota(jnp.int32, x.shape, 0)
        dx = w_ref[CONV_WIDTH - 1:CONV_WIDTH, :] * dc
        rows = [jnp.sum(dc * x, axis=0, keepdims=True)]
        for j in range(1, CONV_WIDTH):
            up = jnp.where(row < s_len - j, pltpu.roll(dc, s_len - j, 0), 0.0)
            dx = dx + w_ref[CONV_WIDTH - 1 - j:CONV_WIDTH - j, :] * up
            xs = jnp.where(row >= j, pltpu.roll(x, j, 0), 0.0)
            rows.append(jnp.sum(dc * xs, axis=0, keepdims=True))
        dx_ref[...] = dx.astype(BF16)
        part = jnp.concatenate(rows[::-1] + [jnp.zeros((8 - CONV_WIDTH, cb), F32)], axis=0)

        @pl.when(b == 0)
        def _():
            dw_ref[...] = part

        @pl.when(b > 0)
        def _():
            dw_ref[...] += part

    return pl.pallas_call(
        body, name=name, grid=(width // cb, t // s_len),
        in_specs=[pl.BlockSpec((s_len, cb), lambda j, b: (b, c0 + j)), pl.BlockSpec((8, cb), lambda j, b: (0, j)),
                  pl.BlockSpec((s_len, cb), lambda j, b: (b, j))],
        out_specs=[pl.BlockSpec((s_len, cb), lambda j, b: (b, j)), pl.BlockSpec((8, cb), lambda j, b: (0, j))],
        out_shape=[jax.ShapeDtypeStruct((t, width), BF16), jax.ShapeDtypeStruct((8, width), F32)],
        compiler_params=_params())(proj, conv8, du)


def _split_bf16(v):
    hi = v.astype(BF16)
    return hi, (v - hi.astype(F32)).astype(BF16)


def _bdot(a, b, dims):
    return lax.dot_general(a, b, (dims, ((0,), (0,))), preferred_element_type=F32)


def _dot3(a, b, dims, exact_a=False):
    ah, al = _split_bf16(a)
    bh, bl = _split_bf16(b)
    out = _bdot(ah, bh, dims) + _bdot(ah, bl, dims)
    return out if exact_a else out + _bdot(al, bh, dims)


@functools.partial(jax.custom_vjp, nondiff_argnums=(2,))
def _bmm(a, b, exact_a=False):
    return _dot3(a, b, ((2,), (1,)), exact_a)


def _bmm_fwd(a, b, exact_a):
    return _dot3(a, b, ((2,), (1,)), exact_a), (a, b)


def _bmm_bwd(exact_a, res, ct):
    a, b = res
    da = jnp.zeros_like(a) if exact_a else _dot3(ct, b, ((2,), (2,)))
    db = _dot3(a, ct, ((1,), (1,)), exact_a)
    return da, db


_bmm.defvjp(_bmm_fwd, _bmm_bwd)


@jax.custom_vjp
def _bmm_nt(a, b):
    return _bdot(a.astype(BF16), b.astype(BF16), ((2,), (2,)))


def _bmm_nt_fwd(a, b):
    return _bmm_nt(a, b), (a, b)


def _bmm_nt_bwd(res, ct):
    a, b = res
    ctb = ct.astype(BF16)
    return _bdot(ctb, b.astype(BF16), ((2,), (1,))), _bdot(ctb, a.astype(BF16), ((1,), (1,)))


_bmm_nt.defvjp(_bmm_nt_fwd, _bmm_nt_bwd)


def _unit_lower_inverse(nm):
    c = nm.shape[-1]
    eye = (lax.broadcasted_iota(jnp.int32, (c, c), 0) == lax.broadcasted_iota(jnp.int32, (c, c), 1)).astype(F32)
    x = -nm
    inv = eye[None] + x
    p = x
    for _ in range(int(math.log2(c)) - 1):
        p = _bmm(p, p)
        inv = inv + _bmm(inv, p)
    return inv


def _dn_chunk_terms(uq, uk, uv, a_col, b_col, alog, dtb):
    n, c, dh = uq.shape
    q = uq * lax.rsqrt(jnp.sum(uq * uq, axis=-1, keepdims=True) + EPS) * (HEAD_DIM ** -0.5)
    k = uk * lax.rsqrt(jnp.sum(uk * uk, axis=-1, keepdims=True) + EPS)
    beta = _sigmoid(b_col)
    xa = a_col + dtb
    g = -jnp.exp(alog) * (jnp.maximum(xa, 0.0) + jnp.log(1.0 + jnp.exp(-jnp.abs(xa))))
    ri = lax.broadcasted_iota(jnp.int32, (c, c), 0)
    ci = lax.broadcasted_iota(jnp.int32, (c, c), 1)
    incl = ri >= ci
    strict = ri > ci
    l_incl = jnp.broadcast_to(incl.astype(F32)[None], (n, c, c))
    gb = jnp.broadcast_to(g, (n, c, dh))
    l_sums = jnp.broadcast_to(jnp.concatenate([incl.astype(F32), jnp.ones((dh - c, c), F32)], axis=0)[None], (n, dh, c))
    sums = _bmm(l_sums, gb, True)
    gc, gtot = sums[:, :c], sums[:, c:2 * c]
    gdiff = _bmm(l_incl, jnp.broadcast_to(g, (n, c, c)) * strict.astype(F32)[None], True)
    decay = jnp.where(incl[None], jnp.exp(jnp.where(incl[None], gdiff, 0.0)), 0.0)
    kb = k * beta
    nm = jnp.where(strict[None], _bmm_nt(kb, k) * decay, 0.0)
    tinv = _unit_lower_inverse(nm)
    w = _bmm(tinv, kb * jnp.exp(gc))
    u = _bmm(tinv, uv * beta)
    qk = _bmm_nt(q, k) * decay
    q_dec = q * jnp.exp(gc)
    k_dec = k * jnp.exp(gtot - gc)
    g_last = jnp.exp(jnp.concatenate([gtot] * (dh // c), axis=1))
    return w, u, qk, q_dec, k_dec, g_last


DN_SUB = 8


def _dn_gather_inputs(uq_ref, uk_ref, uv_ref, ba_ref, prm_ref, h, nh, rows, nb):
    ba = ba_ref[rows, :]
    lane = lax.broadcasted_iota(jnp.int32, ba.shape, 1)
    b_col = jnp.sum(jnp.where(lane == h, ba, 0.0), axis=-1, keepdims=True).reshape(nb, CHUNK, 1)
    a_col = jnp.sum(jnp.where(lane == nh + h, ba, 0.0), axis=-1, keepdims=True).reshape(nb, CHUNK, 1)
    lane1 = lax.broadcasted_iota(jnp.int32, (1, LANE), 1)
    alog = jnp.sum(jnp.where(lane1 == h, prm_ref[1:2, :], 0.0), axis=-1, keepdims=True)
    dtb = jnp.sum(jnp.where(lane1 == h, prm_ref[2:3, :], 0.0), axis=-1, keepdims=True)
    shp = (nb, CHUNK, HEAD_DIM)
    return (uq_ref[rows, :].reshape(shp), uk_ref[rows, :].reshape(shp), uv_ref[rows, :].reshape(shp),
            a_col, b_col, alog, dtb)


def _dn_fill_terms(in_refs, h, nh, n, term_refs):
    nb = min(DN_SUB, n)

    def sub(i, carry):
        rows = pl.ds(pl.multiple_of(i * (nb * CHUNK), nb * CHUNK), nb * CHUNK)
        terms = _dn_chunk_terms(*_dn_gather_inputs(*in_refs, h, nh, rows, nb))
        for r, v in zip(term_refs, terms):
            r[pl.ds(i * nb, nb)] = v
        return carry

    lax.fori_loop(0, n // nb, sub, 0)


def _dn_scan(terms_refs, o_ref, st_ref, n):
    w_ref, u_ref, qk_ref, qd_ref, kd_ref, gl_ref = terms_refs

    def step(i, state):
        if st_ref is not None:
            st_ref[i] = state
        sb = state.astype(BF16)
        v_new = u_ref[i] - jnp.dot(w_ref[i].astype(BF16), sb, preferred_element_type=F32)
        vb = v_new.astype(BF16)
        o_ref[i] = (jnp.dot(qd_ref[i].astype(BF16), sb, preferred_element_type=F32)
                    + jnp.dot(qk_ref[i].astype(BF16), vb, preferred_element_type=F32))
        return state * gl_ref[i] + lax.dot_general(kd_ref[i].astype(BF16), vb, TN, preferred_element_type=F32)

    lax.fori_loop(0, n, step, jnp.zeros((HEAD_DIM, HEAD_DIM), F32))


def _dn_specs(nh, nh_a, s_len, zc0, bac):
    head = lambda off: pl.BlockSpec((s_len, HEAD_DIM), lambda b, h: (b, off + h))
    return dict(uq=head(0), uk=head(nh), uv=head(2 * nh), z=head(zc0),
                ba=pl.BlockSpec((s_len, LANE), lambda b, h: (b, bac)),
                prm=pl.BlockSpec((8, LANE), lambda b, h: (0, 0)), dout=head(nh_a), out=head(0))


def _dn_scratch(n, with_states):
    big = pltpu.VMEM((n, CHUNK, HEAD_DIM), F32)
    sc = [big, big, pltpu.VMEM((n, CHUNK, CHUNK), F32), big, big, pltpu.VMEM((n, HEAD_DIM, HEAD_DIM), F32), big]
    if with_states:
        sc.append(pltpu.VMEM((n, HEAD_DIM, HEAD_DIM), F32))
    return sc


def _dn_fwd(u, proj, prm, nh, s_len, zc0, bac, name, comm):
    t = u.shape[0]
    n = s_len // CHUNK
    sp = _dn_specs(nh, 0, s_len, zc0, bac)

    def body(uq_ref, uk_ref, uv_ref, z_ref, ba_ref, prm_ref, o_ref, *scr):
        h = pl.program_id(1)
        _dn_fill_terms((uq_ref, uk_ref, uv_ref, ba_ref, prm_ref), h, nh, n, scr[:6])
        _dn_scan(scr[:6], scr[6], None, n)
        o = scr[6][...].reshape(s_len, HEAD_DIM)
        z = z_ref[...]
        r = lax.rsqrt(jnp.mean(o * o, axis=-1, keepdims=True) + EPS)
        o_ref[...] = o * r * prm_ref[0:1, :] * (z * _sigmoid(z))

    outs, couts = _call_carrying(
        body, name, (t // s_len, nh), [sp["uq"], sp["uk"], sp["uv"], sp["z"], sp["ba"], sp["prm"]], [sp["out"]],
        [jax.ShapeDtypeStruct((t, nh * HEAD_DIM), F32)], _dn_scratch(n, False), (u, u, u, proj, proj, prm), comm)
    return outs[0], couts


def _dn_bwd(u, proj, prm, dcat, nh, nh_a, s_len, zc0, bac, name):
    t = u.shape[0]
    n = s_len // CHUNK
    sp = _dn_specs(nh, nh_a, s_len, zc0, bac)

    def body(uq_ref, uk_ref, uv_ref, z_ref, ba_ref, prm_ref, do_ref,
             duq_ref, duk_ref, duv_ref, dz_ref, dba_ref, dprm_ref, *scr):
        b, h = pl.program_id(0), pl.program_id(1)
        in_refs = (uq_ref, uk_ref, uv_ref, ba_ref, prm_ref)
        w_ref, u_ref, qk_ref, qd_ref, kd_ref, gl_ref, o_scr, st_ref = scr
        _dn_fill_terms(in_refs, h, nh, n, scr[:6])
        _dn_scan(scr[:6], o_scr, st_ref, n)

        o = o_scr[...].reshape(s_len, HEAD_DIM)
        z = z_ref[...]
        dout = do_ref[...].astype(F32)
        gain = prm_ref[0:1, :]
        sg = _sigmoid(z)
        sz = z * sg
        r = lax.rsqrt(jnp.mean(o * o, axis=-1, keepdims=True) + EPS)
        oh = o * r
        dgain = jnp.sum(dout * oh * sz, axis=0, keepdims=True)
        dz_ref[...] = (dout * oh * gain * (sg * (1.0 + z * (1.0 - sg)))).astype(BF16)
        doh = dout * gain * sz
        d_o = r * (doh - oh * jnp.mean(doh * oh, axis=-1, keepdims=True))
        o_scr[...] = d_o.reshape(n, CHUNK, HEAD_DIM)

        def step(j, ds):
            i = n - 1 - j
            st = st_ref[i]
            sb = st.astype(BF16)
            wi, qki, qdi, kdi, gli = w_ref[i], qk_ref[i], qd_ref[i], kd_ref[i], gl_ref[i]
            v_new = u_ref[i] - jnp.dot(wi.astype(BF16), sb, preferred_element_type=F32)
            vb = v_new.astype(BF16)
            don = o_scr[i].astype(BF16)
            dsb = ds.astype(BF16)
            dv = (lax.dot_general(qki.astype(BF16), don, TN, preferred_element_type=F32)
                  + jnp.dot(kdi.astype(BF16), dsb, preferred_element_type=F32))
            dvb = dv.astype(BF16)
            qd_ref[i] = lax.dot_general(don, sb, NT, preferred_element_type=F32)
            qk_ref[i] = lax.dot_general(don, vb, NT, preferred_element_type=F32)
            kd_ref[i] = lax.dot_general(vb, dsb, NT, preferred_element_type=F32)
            gl_ref[i] = ds * st
            u_ref[i] = dv
            w_ref[i] = -lax.dot_general(dvb, sb, NT, preferred_element_type=F32)
            return (ds * gli + lax.dot_general(qdi.astype(BF16), don, TN, preferred_element_type=F32)
                    - lax.dot_general(wi.astype(BF16), dvb, TN, preferred_element_type=F32))

        lax.fori_loop(0, n, step, jnp.zeros((HEAD_DIM, HEAD_DIM), F32))

        @pl.when(h == 0)
        def _():
            dba_ref[...] = jnp.zeros(dba_ref.shape, F32)

        nb = min(DN_SUB, n)

        def sub(i, carry):
            rows = pl.ds(pl.multiple_of(i * (nb * CHUNK), nb * CHUNK), nb * CHUNK)
            _, pull = jax.vjp(_dn_chunk_terms, *_dn_gather_inputs(*in_refs, h, nh, rows, nb))
            duq, duk, duv, da_col, db_col, dal, ddt = pull(tuple(r[pl.ds(i * nb, nb)] for r in scr[:6]))
            duq_ref[rows, :] = duq.reshape(nb * CHUNK, HEAD_DIM)
            duk_ref[rows, :] = duk.reshape(nb * CHUNK, HEAD_DIM)
            duv_ref[rows, :] = duv.reshape(nb * CHUNK, HEAD_DIM)
            lane = lax.broadcasted_iota(jnp.int32, (nb * CHUNK, LANE), 1)
            dba_ref[rows, :] += (jnp.where(lane == h, db_col.reshape(nb * CHUNK, 1), 0.0)
                                 + jnp.where(lane == nh + h, da_col.reshape(nb * CHUNK, 1), 0.0))
            return carry[0] + dal, carry[1] + ddt

        dalog, ddtb = lax.fori_loop(0, n // nb, sub, (jnp.zeros((1, 1), F32), jnp.zeros((1, 1), F32)))
        lane1 = lax.broadcasted_iota(jnp.int32, (1, LANE), 1)
        dprm = jnp.concatenate([dgain, jnp.where(lane1 == h, dalog, 0.0), jnp.where(lane1 == h, ddtb, 0.0),
                                jnp.zeros((5, LANE), F32)], axis=0)

        @pl.when((b == 0) & (h == 0))
        def _():
            dprm_ref[...] = dprm

        @pl.when((b > 0) | (h > 0))
        def _():
            dprm_ref[...] += dprm

    osd = jax.ShapeDtypeStruct((t, nh * HEAD_DIM), F32)
    return pl.pallas_call(
        body, name=name, grid=(t // s_len, nh),
        in_specs=[sp["uq"], sp["uk"], sp["uv"], sp["z"], sp["ba"], sp["prm"], sp["dout"]],
        out_specs=[sp["out"], sp["out"], sp["out"], sp["out"], pl.BlockSpec((s_len, LANE), lambda b, h: (b, 0)),
                   pl.BlockSpec((8, LANE), lambda b, h: (0, 0))],
        out_shape=[osd, osd, osd, jax.ShapeDtypeStruct((t, nh * HEAD_DIM), BF16),
                   jax.ShapeDtypeStruct((t, LANE), F32), jax.ShapeDtypeStruct((8, LANE), F32)],
        scratch_shapes=_dn_scratch(n, True), compiler_params=_params())(u, u, u, proj, proj, prm, dcat)


def _w_in_windows(ws):
    w0 = [(ws * k) // LANE * LANE for k in range(N_CHIPS)]
    sh = [ws * k - w0[k] for k in range(N_CHIPS)]
    ww = _ceil_to(max(sh) + ws, 2 * LANE)
    n_tiles = (w0[-1] + ww) // LANE
    tpw = ww // LANE
    tph = tpw // 2
    tab = np.zeros((7, n_tiles), np.int32)
    for t in range(n_tiles):
        ks = [k for k in range(N_CHIPS) if w0[k] // LANE <= t < w0[k] // LANE + tpw]
        k1 = ks[-1]
        lt = t - w0[k1] // LANE
        tab[0, t], tab[1, t], tab[2, t] = lt // tph, k1, lt % tph
        k2 = ks[0] if len(ks) > 1 else k1
        lt2 = t - w0[k2] // LANE
        tab[3, t], tab[4, t], tab[5, t] = lt2 // tph, k2, lt2 % tph
        tab[6, t] = 1 if len(ks) > 1 else 0
        assert len(ks) <= 2
    return w0, sh, ww, n_tiles, tab


def kernel(x, ffn1_norm, ffn1_w_gate, ffn1_w_up, ffn1_w_down, mix_norm, w_in, conv_w, a_log, dt_bias, dn_norm, w_out, ffn2_norm, ffn2_w_gate, ffn2_w_up, ffn2_w_down, final_norm, loss_target, m_ffn1_norm, m_ffn1_w_gate, m_ffn1_w_up, m_ffn1_w_down, m_mix_norm, m_w_in, m_conv_w, m_a_log, m_dt_bias, m_dn_norm, m_w_out, m_ffn2_norm, m_ffn2_w_gate, m_ffn2_w_up, m_ffn2_w_down, m_final_norm, v_ffn1_norm, v_ffn1_w_gate, v_ffn1_w_up, v_ffn1_w_down, v_mix_norm, v_w_in, v_conv_w, v_a_log, v_dt_bias, v_dn_norm, v_w_out, v_ffn2_norm, v_ffn2_w_gate, v_ffn2_w_up, v_ffn2_w_down, v_final_norm):
    bl, s_, d = x.shape
    t = bl * s_
    fs = ffn1_w_gate.shape[1]
    hp = _ceil_to(-(-fs // 2), LANE)
    ws = w_in.shape[1]
    d_mix = w_out.shape[0] * N_CHIPS
    d_attn = d_dn = d_mix // 2
    nh_d = d_dn // HEAD_DIM
    d_in = 3 * d_attn + 4 * d_dn + 2 * nh_d
    cs = conv_w.shape[1]
    assert ws * N_CHIPS == d_in and cs * N_CHIPS == 3 * d_dn

    xi, yi, ci = lax.axis_index("x"), lax.axis_index("y"), lax.axis_index("c")
    me = 2 * xi + yi
    cidx = jnp.reshape(ci, (1,)).astype(jnp.int32)
    meidx = jnp.reshape(me, (1,)).astype(jnp.int32)
    mcidx = jnp.stack([me, ci]).astype(jnp.int32)

    w0, sh, ww, n_tiles, tab = _w_in_windows(ws)
    shift = (ws * me) % LANE
    w_in_win_t = lax.dynamic_update_slice(jnp.zeros((ww, d), F32), w_in.T, (shift, jnp.int32(0)))
    rows_tr = math.gcd(hp, fs)
    conv_piece = jnp.pad(conv_w, ((0, 8 - CONV_WIDTH), (0, 0))).reshape(8, 2, cs // 2).transpose(1, 0, 2)
    z0 = jnp.int32(0)
    pieces = [
        _cast_split_cols_t(ffn1_w_gate.T, hp, meidx, "cast_g1"),
        _cast_split_cols_t(ffn1_w_up.T, hp, meidx, "cast_u1"),
        _cast_split_rows(ffn1_w_down, hp, rows_tr, meidx, "cast_d1"),
        _cast_split_cols_t(w_in_win_t, ww // 2, meidx, "cast_in"),
        _cast_split_rows(w_out, w_out.shape[0] // 2, w_out.shape[0] // 2, meidx, "cast_out"),
        _cast_split_cols_t(ffn2_w_gate.T, hp, meidx, "cast_g2"),
        _cast_split_cols_t(ffn2_w_up.T, hp, meidx, "cast_u2"),
        _cast_split_rows(ffn2_w_down, hp, rows_tr, meidx, "cast_d2"),
        lax.dynamic_update_slice(jnp.zeros((2, N_CHIPS, 8, cs // 2), F32), conv_piece[:, None], (z0, me, z0, z0)),
    ]
    p_g1, p_u1, p_d1, p_in, p_out, p_g2, p_u2, p_d2, p_conv = pieces
    npc = 8
    ident = lambda p: p
    cat_map = lambda p: 2 * (p % N_CHIPS) + p // N_CHIPS
    as_cols = lambda a: a.reshape(npc, d, hp)
    as_rows = lambda a: a.reshape(npc, hp, d)

    wg1, wu1 = _comm_call(_all_gather_comm([p_g1, p_u1]), "all_gather_first")
    wg1, wu1 = as_cols(wg1), as_cols(wu1)
    h0 = x.reshape(t, d)
    n1 = _rms_fwd(h0, ffn1_norm, "rms1")
    (a1, b1, s1), (wd1, win_all) = _ffn_up(n1, wg1, wu1, "ffn1_up", comm=_all_gather_comm([p_d1, p_in]))
    wd1 = as_rows(wd1)
    h1, (wout, conv_all) = _mm_pieces_resid(s1, wd1, h0, 0.5, ident, "ffn1_down",
                                            comm=_all_gather_comm([p_out, p_conv]))
    wout = wout.reshape(npc, w_out.shape[0] // 2, d)
    conv8 = conv_all.transpose(2, 1, 0, 3).reshape(8, 3 * d_dn)
    win_full = _combine_windows(win_all, jnp.asarray(tab), n_tiles, "combine_w_in")
    n2 = _rms_fwd(h1, mix_norm, "rms2")
    proj, (wg2,) = _mm2d(n2, win_full, NN, F32, "in_proj", (1024, 768, 4096), comm=_all_gather_comm([p_g2]))
    nh_a = d_attn // HEAD_DIM
    attn, (wu2,) = _attn_fwd(proj, nh_a, s_, "attn_fwd", _all_gather_comm([p_u2]))
    zc0 = (3 * d_attn + 3 * d_dn) // HEAD_DIM
    bac = (3 * d_attn + 4 * d_dn) // LANE
    row128 = lambda v: jnp.pad(v, (0, LANE - v.shape[0])).reshape(1, LANE)
    prm = jnp.concatenate([row128(dn_norm), row128(a_log), row128(dt_bias), jnp.zeros((5, LANE), F32)], axis=0)
    u_dn = _conv_fwd(proj, conv8, 3 * d_attn, 3 * d_dn, s_, "dn_conv")
    dn_out, (wd2,) = _dn_fwd(u_dn, proj, prm, nh_d, s_, zc0, bac, "dn_fwd", _all_gather_comm([p_d2]))
    wg2, wu2, wd2 = as_cols(wg2), as_cols(wu2), as_rows(wd2)
    cat_b = jnp.concatenate([attn, dn_out], axis=1).astype(BF16)
    h2 = _mm_pieces_resid(cat_b, wout, h1, 1.0, cat_map, "out_proj")
    n3 = _rms_fwd(h2, ffn2_norm, "rms3")
    a3, b3, s3 = _ffn_up(n3, wg2, wu2, "ffn2_up")
    h3 = _mm_pieces_resid(s3, wd2, h2, 0.5, ident, "ffn2_down")

    def rs_front(gs, tag):
        gs = [g.reshape((2, N_CHIPS * g.shape[-2], g.shape[-1])) for g in gs]
        from_sib = _sibling_take(gs, f"rs_sibling_take_{tag}")
        ps = [_pair_sum(g, l, cidx, f"rs_pair_sum_{tag}_{i}") for i, (g, l) in enumerate(zip(gs, from_sib))]
        return [p.reshape(N_CHIPS, p.shape[0] // N_CHIPS, p.shape[1]) for p in ps]

    dh3, dh3b, dwf_p, lsq_p = _final_loss(h3, final_norm, loss_target.reshape(t, d), "final_loss")
    da3, db3 = _ffn_bwd_hidden(dh3b, wd2, a3, b3, "ffn2_bwd_hidden")
    g_wd2 = _grad_rows_pieces(s3, dh3b, 0.5, ident, npc, "ffn2_grad_down")
    p_d2s = rs_front([g_wd2], "d2")
    (g_wg2, g_wu2), l_d2 = _grad_cols_pieces(n3, da3, db3, npc, "ffn2_grad_up", comm=_chip_all_to_all_comm(p_d2s))
    p_gu2 = rs_front([g_wg2, g_wu2], "gu2")
    dn3, l_gu2 = _ffn_bwd_input(da3, db3, wg2, wu2, "ffn2_bwd_input", comm=_chip_all_to_all_comm(p_gu2))
    dh2, dh2b, dw3_p = _rms_bwd(dn3, h2, ffn2_norm, dh3, "rms3_bwd")

    dcat = _mm_nt_pieces_out(dh2b, wout, cat_map, "out_proj_bwd")
    g_wout = _grad_rows_pieces(cat_b, dh2b, 1.0, cat_map, npc, "out_proj_grad")
    dq_a, dk_a, dv_a = _attn_bwd(proj, attn, dcat, nh_a, s_, "attn_bwd")
    duq, duk, duv, dz, dba, dprm = _dn_bwd(u_dn, proj, prm, dcat, nh_d, nh_a, s_, zc0, bac, "dn_bwd")
    dx_conv, dconv8 = _conv_bwd(proj, conv8, jnp.concatenate([duq, duk, duv], axis=1), 3 * d_attn, 3 * d_dn, s_,
                                "dn_conv_bwd")
    used = 3 * d_attn + 4 * d_dn + LANE
    dproj_b = jnp.concatenate([dq_a, dk_a, dv_a, dx_conv, dz, dba.astype(BF16),
                               jnp.zeros((t, proj.shape[1] - used), BF16)], axis=1)
    dconv, ddnn, dalog, ddtb = dconv8[:CONV_WIDTH], dprm[0, :dn_norm.shape[0]], dprm[1, :nh_d], dprm[2, :nh_d]
    g_win_full = _mm2d(n2, dproj_b, TN, BF16, "in_proj_grad", (1024, 2432, 512))
    wh = ww // 2
    g_win = jnp.stack([jnp.stack([g_win_full[:, w0[k] + wh * h: w0[k] + wh * (h + 1)] for k in range(N_CHIPS)])
                       for h in range(2)])
    p_ow = rs_front([g_wout, g_win], "ow")
    dn2, l_ow = _mm2d(dproj_b, win_full, NT, F32, "in_proj_bwd", (1024, 2048, 768), comm=_chip_all_to_all_comm(p_ow))
    dh1, dh1b, dwm_p = _rms_bwd(dn2, h1, mix_norm, dh2, "rms2_bwd")

    da1, db1 = _ffn_bwd_hidden(dh1b, wd1, a1, b1, "ffn1_bwd_hidden")
    g_wd1 = _grad_rows_pieces(s1, dh1b, 0.5, ident, npc, "ffn1_grad_down")
    p_d1s = rs_front([g_wd1], "d1")
    (g_wg1, g_wu1), l_d1 = _grad_cols_pieces(n1, da1, db1, npc, "ffn1_grad_up", comm=_chip_all_to_all_comm(p_d1s))
    p_gu1 = rs_front([g_wg1, g_wu1], "gu1")
    dn1, l_gu1 = _ffn_bwd_input(da1, db1, wg1, wu1, "ffn1_bwd_input", comm=_chip_all_to_all_comm(p_gu1))
    dh0, _, dw1_p = _rms_bwd(dn1, h0, ffn1_norm, dh1, "rms1_bwd")
    grad_x = dh0.reshape(bl, s_, d)

    pair = [p_gu1[0], p_gu1[1], p_d1s[0], p_ow[1], p_ow[0], p_gu2[0], p_gu2[1], p_d2s[0]]
    from_chips = [l_gu1[0], l_gu1[1], l_d1[0], l_ow[1], l_ow[0], l_gu2[0], l_gu2[1], l_d2[0]]
    halves = [_chip_sum(p, l, mcidx, f"rs_chip_sum_{i}") for i, (p, l) in enumerate(zip(pair, from_chips))]
    full = _sibling_join(halves, "rs_sibling_join")
    f_wg1, f_wu1, f_wd1, f_win, f_wout, f_wg2, f_wu2, f_wd2 = full

    unpad_cols = lambda f: jnp.concatenate([f[0], f[1][:, :fs - hp]], axis=1)
    unpad_rows = lambda f: f.reshape(2 * f.shape[1], f.shape[2])[:fs]
    gw = {
        "ffn1_w_gate": unpad_cols(f_wg1), "ffn1_w_up": unpad_cols(f_wu1), "ffn1_w_down": unpad_rows(f_wd1),
        "w_in": lax.dynamic_slice(jnp.concatenate([f_win[0], f_win[1]], axis=1), (jnp.int32(0), shift), (d, ws)),
        "w_out": f_wout.reshape(w_out.shape),
        "ffn2_w_gate": unpad_cols(f_wg2), "ffn2_w_up": unpad_cols(f_wu2), "ffn2_w_down": unpad_rows(f_wd2),
    }

    def lanes(v):
        v = v.reshape(-1)
        return jnp.pad(v, (0, _ceil_to(v.shape[0], LANE) - v.shape[0])).reshape(-1, LANE)

    small = [dw1_p.sum(0), dwm_p.sum(0), dw3_p.sum(0), dwf_p.sum(0), ddnn, dalog, ddtb,
             (0.5 / d) * jnp.sum(lsq_p).reshape(1), dconv]
    rows = [lanes(v) for v in small]
    offs = np.cumsum([0] + [r.shape[0] for r in rows])
    packed = jnp.concatenate(rows, axis=0)
    packed = jnp.pad(packed, ((0, _ceil_to(packed.shape[0], 8) - packed.shape[0]), (0, 0)))
    red = _allreduce_small(packed, "allreduce_small")
    take = lambda i, shape: red[offs[i]:offs[i + 1]].reshape(-1)[:int(np.prod(shape))].reshape(shape)
    gw["ffn1_norm"] = take(0, (d,))
    gw["mix_norm"] = take(1, (d,))
    gw["ffn2_norm"] = take(2, (d,))
    gw["final_norm"] = take(3, (d,))
    gw["dn_norm"] = take(4, dn_norm.shape)
    gw["a_log"] = take(5, a_log.shape)
    gw["dt_bias"] = take(6, dt_bias.shape)
    loss = take(7, (1,)).reshape(())
    gw["conv_w"] = lax.dynamic_slice(take(8, (CONV_WIDTH, 3 * d_dn)), (jnp.int32(0), me * cs), (CONV_WIDTH, cs))

    names = ['ffn1_norm', 'ffn1_w_gate', 'ffn1_w_up', 'ffn1_w_down', 'mix_norm', 'w_in', 'conv_w', 'a_log', 'dt_bias',
             'dn_norm', 'w_out', 'ffn2_norm', 'ffn2_w_gate', 'ffn2_w_up', 'ffn2_w_down', 'final_norm']
    wv = dict(zip(names, (ffn1_norm, ffn1_w_gate, ffn1_w_up, ffn1_w_down, mix_norm, w_in, conv_w, a_log, dt_bias,
                          dn_norm, w_out, ffn2_norm, ffn2_w_gate, ffn2_w_up, ffn2_w_down, final_norm)))
    mv = dict(zip(names, (m_ffn1_norm, m_ffn1_w_gate, m_ffn1_w_up, m_ffn1_w_down, m_mix_norm, m_w_in, m_conv_w, m_a_log,
                          m_dt_bias, m_dn_norm, m_w_out, m_ffn2_norm, m_ffn2_w_gate, m_ffn2_w_up, m_ffn2_w_down,
                          m_final_norm)))
    vv = dict(zip(names, (v_ffn1_norm, v_ffn1_w_gate, v_ffn1_w_up, v_ffn1_w_down, v_mix_norm, v_w_in, v_conv_w, v_a_log,
                          v_dt_bias, v_dn_norm, v_w_out, v_ffn2_norm, v_ffn2_w_gate, v_ffn2_w_up, v_ffn2_w_down,
                          v_final_norm)))
    delta, new_m, new_v = {}, {}, {}
    small_names = [n for n in names if wv[n].ndim == 1 or n == "conv_w"]
    transposed = ("ffn1_w_gate", "ffn1_w_up", "ffn2_w_gate", "ffn2_w_up")
    for n in names:
        if n in small_names:
            continue
        if n in transposed:
            res = _adamw(gw[n].T, wv[n].T, mv[n].T, vv[n].T, f"adamw_{n}")
            delta[n], new_m[n], new_v[n] = (r.T for r in res)
        else:
            delta[n], new_m[n], new_v[n] = _adamw(gw[n], wv[n], mv[n], vv[n], f"adamw_{n}")
    srows = {n: lanes(gw[n]).shape[0] for n in small_names}
    soffs = np.cumsum([0] + [srows[n] for n in small_names])
    stot = _ceil_to(int(soffs[-1]), 8)

    def pack(dct):
        p = jnp.concatenate([lanes(dct[n]) for n in small_names], axis=0)
        return jnp.pad(p, ((0, stot - p.shape[0]), (0, 0)))

    sd, sm, sv = _adamw(pack(gw), pack(wv), pack(mv), pack(vv), "adamw_small")
    for i, n in enumerate(small_names):
        cut = lambda p: p[soffs[i]:soffs[i + 1]].reshape(-1)[:wv[n].size].reshape(wv[n].shape)
        delta[n], new_m[n], new_v[n] = cut(sd), cut(sm), cut(sv)

    return (loss, grad_x, *[gw[n] for n in names], *[delta[n] for n in names], *[new_m[n] for n in names],
            *[new_v[n] for n in names])
```

```python
import functools
import math

import jax
import jax.numpy as jnp
import numpy as np
from jax import lax
from jax.experimental import pallas as pl
from jax.experimental.pallas import tpu as pltpu

F32 = jnp.float32
BF16 = jnp.bfloat16
MESH = pl.DeviceIdType.MESH
ANY = pl.BlockSpec(memory_space=pl.ANY)

LANE = 128
N_CHIPS = 4
N_DEV = 8
EPS = 1e-6
HEAD_DIM = 128
CONV_WIDTH = 4
CHUNK = 64
ATTN_BLOCK = 128
DILATED_CONFIGS = ((128, 1), (512, 4), (2048, 16))
VMEM_LIMIT = 52 * 1024 * 1024

ADAM_LR = 0.001
ADAM_B1 = 0.9
ADAM_B2 = 0.999
ADAM_EPS = 1e-08
ADAM_WD = 0.01
ADAM_STEP = 10

NN = (((1,), (0,)), ((), ()))
NT = (((1,), (1,)), ((), ()))
TN = (((0,), (0,)), ((), ()))


def _ceil_to(v, m):
    return -(-v // m) * m


def _params(vmem=VMEM_LIMIT):
    return pltpu.CompilerParams(vmem_limit_bytes=vmem)


class _Comm:
    def __init__(self, ins, out_shape, aliases, sems, start, mid, finish):
        self.ins, self.out_shape, self.aliases, self.sems = ins, out_shape, aliases, sems
        self.start, self.mid, self.finish = start, mid, finish


def _gemm(name, grid, pairs, dn, acc_shape, n_acc, extras, outs, epilogue, comm=None, epi_rows=None):
    n_pairs, n_ex, n_out = len(pairs), len(extras), len(outs)
    n_ci = len(comm.ins) if comm else 0
    n_co = len(comm.out_shape) if comm else 0
    n_sem = len(comm.sems) if comm else 0
    kax = len(grid) - 1
    nk = grid[kax]
    n_in = 2 * n_pairs + n_ex

    def body(*refs):
        ins = refs[: 2 * n_pairs]
        ex = refs[2 * n_pairs: n_in]
        c_in = refs[n_in: n_in + n_ci]
        out_refs = refs[n_in + n_ci: n_in + n_ci + n_out]
        c_out = refs[n_in + n_ci + n_out: n_in + n_ci + n_out + n_co]
        accs = refs[n_in + n_ci + n_out + n_co: n_in + n_ci + n_out + n_co + n_acc]
        sems = refs[n_in + n_ci + n_out + n_co + n_acc:]
        k = pl.program_id(kax)
        pids = [pl.program_id(a) for a in range(len(grid))]

        def at(point):
            cond = pids[0] == point[0]
            for pid, v in zip(pids[1:], point[1:]):
                cond = cond & (pid == v)
            return cond

        if comm:
            @pl.when(at([0] * len(grid)))
            def _():
                comm.start(c_in, c_out, sems)

            if comm.mid:
                @pl.when(at([grid[0] * 3 // 4] + [0] * (len(grid) - 1)))
                def _():
                    comm.mid(c_in, c_out, sems)

        @pl.when(k == 0)
        def _():
            for acc in accs:
                acc[...] = jnp.zeros(acc.shape, F32)

        for q in range(n_pairs):
            a = ins[2 * q][...]
            b = ins[2 * q + 1][...]
            if a.dtype != BF16:
                a = a.astype(BF16)
            if b.dtype != BF16:
                b = b.astype(BF16)
            accs[pairs[q][4]][...] += lax.dot_general(a, b, dn, preferred_element_type=F32)

        @pl.when(k == nk - 1)
        def _():
            rows = acc_shape[0]
            step = epi_rows if epi_rows and rows % epi_rows == 0 else rows
            for r0 in range(0, rows, step):
                sl = pl.ds(r0, step)
                res = epilogue([acc[sl, :] for acc in accs], [e[sl, :] for e in ex])
                for o, r in zip(out_refs, res):
                    o[sl, :] = r.astype(o.dtype)

        if comm:
            @pl.when(at([g - 1 for g in grid]))
            def _():
                comm.finish(c_in, c_out, sems)

    in_specs = []
    args = []
    for a, a_spec, b, b_spec, _ in pairs:
        in_specs += [a_spec, b_spec]
        args += [a, b]
    for e, e_spec in extras:
        in_specs.append(e_spec)
        args.append(e)
    out_shape = [o for o, _ in outs]
    out_specs = [s for _, s in outs]
    scratch = [pltpu.VMEM(acc_shape, F32) for _ in range(n_acc)]
    kwargs = {}
    if comm:
        in_specs += [ANY] * n_ci
        args += list(comm.ins)
        out_shape += list(comm.out_shape)
        out_specs += [ANY] * n_co
        scratch += list(comm.sems)
        kwargs["input_output_aliases"] = {n_in + i: n_out + o for i, o in comm.aliases.items()}
    res = pl.pallas_call(body, name=name, grid=grid, in_specs=in_specs, out_specs=out_specs,
                         out_shape=out_shape, scratch_shapes=scratch, compiler_params=_params(), **kwargs)(*args)
    if comm:
        return list(res[:n_out]), list(res[n_out:])
    return res


def _unpack(res, comm, single):
    if comm:
        outs, couts = res
        return (outs[0] if single else outs), couts
    return res[0] if single else res


def _call_carrying(body, name, grid, in_specs, out_specs, out_shape, scratch, args, comm):
    n_in, n_out, n_scr = len(in_specs), len(out_shape), len(scratch)
    n_ci, n_co = len(comm.ins), len(comm.out_shape)

    def wrapped(*refs):
        ins, c_in = refs[:n_in], refs[n_in:n_in + n_ci]
        outs = refs[n_in + n_ci:n_in + n_ci + n_out]
        c_out = refs[n_in + n_ci + n_out:n_in + n_ci + n_out + n_co]
        scr = refs[n_in + n_ci + n_out + n_co:n_in + n_ci + n_out + n_co + n_scr]
        sems = refs[n_in + n_ci + n_out + n_co + n_scr:]
        pids = [pl.program_id(a) for a in range(len(grid))]

        def at(point):
            cond = pids[0] == point[0]
            for pid, v in zip(pids[1:], point[1:]):
                cond = cond & (pid == v)
            return cond

        @pl.when(at([0] * len(grid)))
        def _():
            comm.start(c_in, c_out, sems)

        if comm.mid:
            @pl.when(at([grid[0] * 3 // 4] + [0] * (len(grid) - 1)))
            def _():
                comm.mid(c_in, c_out, sems)

        body(*ins, *outs, *scr)

        @pl.when(at([g - 1 for g in grid]))
        def _():
            comm.finish(c_in, c_out, sems)

    res = pl.pallas_call(
        wrapped, name=name, grid=grid, in_specs=list(in_specs) + [ANY] * n_ci, out_specs=list(out_specs) + [ANY] * n_co,
        out_shape=list(out_shape) + list(comm.out_shape), scratch_shapes=list(scratch) + list(comm.sems),
        input_output_aliases={n_in + i: n_out + o for i, o in comm.aliases.items()},
        compiler_params=_params())(*args, *comm.ins)
    return list(res[:n_out]), list(res[n_out:])


def _pick(n, prefs):
    for p in prefs:
        if n % p == 0:
            return p
    return n


def _sigmoid(v):
    return 1.0 / (1.0 + jnp.exp(-v))


def _ffn_up(n, wg, wu, name, comm=None):
    t, d = n.shape
    npieces, _, hp = wg.shape
    tm = _pick(t, (1024, 512, 256, 128, 64, 32, 16))
    tk = _pick(d, (512, 256, 128))
    grid = (t // tm, npieces, d // tk)
    a_spec = pl.BlockSpec((tm, tk), lambda i, p, k: (i, k))
    w_spec = pl.BlockSpec((None, tk, hp), lambda i, p, k: (p, k, 0))
    o_spec = pl.BlockSpec((tm, hp), lambda i, p, k: (i, p))
    osd = jax.ShapeDtypeStruct((t, npieces * hp), BF16)

    def epi(accs, ex):
        a, b = accs
        return a, b, a * _sigmoid(a) * b

    return _unpack(_gemm(name, grid, [(n, a_spec, wg, w_spec, 0), (n, a_spec, wu, w_spec, 1)], NN, (tm, hp), 2, [],
                         [(osd, o_spec)] * 3, epi, comm, epi_rows=256), comm, False)


def _mm_pieces_resid(a, w, resid, scale, amap, name, comm=None):
    t = a.shape[0]
    npieces, kp, n = w.shape
    tm = _pick(t, (1024, 512, 256, 128, 64, 32, 16))
    tn = _pick(n, (1024, 512, 256, 128))
    grid = (t // tm, n // tn, npieces)
    a_spec = pl.BlockSpec((tm, kp), lambda i, j, p: (i, amap(p)))
    w_spec = pl.BlockSpec((None, kp, tn), lambda i, j, p: (p, 0, j))
    r_spec = pl.BlockSpec((tm, tn), lambda i, j, p: (i, j))

    def epi(accs, ex):
        return (ex[0] + scale * accs[0],)

    return _unpack(_gemm(name, grid, [(a, a_spec, w, w_spec, 0)], NN, (tm, tn), 1, [(resid, r_spec)],
                         [(jax.ShapeDtypeStruct((t, n), F32), r_spec)], epi, comm), comm, True)


def _ffn_bwd_hidden(dh, wd, a, b, name):
    t, d = dh.shape
    npieces, hp, _ = wd.shape
    tm = _pick(t, (1024, 512, 256, 128, 64, 32, 16))
    tk = _pick(d, (512, 256, 128))
    grid = (t // tm, npieces, d // tk)
    a_spec = pl.BlockSpec((tm, tk), lambda i, p, k: (i, k))
    w_spec = pl.BlockSpec((None, hp, tk), lambda i, p, k: (p, 0, k))
    o_spec = pl.BlockSpec((tm, hp), lambda i, p, k: (i, p))
    osd = jax.ShapeDtypeStruct((t, npieces * hp), BF16)

    def epi(accs, ex):
        ds = 0.5 * accs[0]
        av = ex[0].astype(F32)
        bv = ex[1].astype(F32)
        sg = _sigmoid(av)
        da = ds * bv * (sg * (1.0 + av * (1.0 - sg)))
        db = ds * (av * sg)
        return da, db

    return _gemm(name, grid, [(dh, a_spec, wd, w_spec, 0)], NT, (tm, hp), 1, [(a, o_spec), (b, o_spec)],
                 [(osd, o_spec)] * 2, epi, epi_rows=256)


def _mm_nt_pieces_out(dh, w, omap, name):
    t, d = dh.shape
    npieces, npp, _ = w.shape
    tm = _pick(t, (1024, 512, 256, 128, 64, 32, 16))
    tk = _pick(d, (1024, 512, 256, 128))
    grid = (t // tm, npieces, d // tk)
    a_spec = pl.BlockSpec((tm, tk), lambda i, p, k: (i, k))
    w_spec = pl.BlockSpec((None, npp, tk), lambda i, p, k: (p, 0, k))
    o_spec = pl.BlockSpec((tm, npp), lambda i, p, k: (i, omap(p)))
    return _gemm(name, grid, [(dh, a_spec, w, w_spec, 0)], NT, (tm, npp), 1, [],
                 [(jax.ShapeDtypeStruct((t, npieces * npp), BF16), o_spec)], lambda accs, ex: (accs[0],))[0]


def _grad_rows_pieces(x, dy, scale, amap, npieces, name):
    t, n = dy.shape
    mp = x.shape[1] // npieces
    tn = _pick(n, (1024, 512, 256, 128))
    tk = _pick(t, (1024, 512, 256, 128, 64, 32, 16))
    grid = (npieces, n // tn, t // tk)
    x_spec = pl.BlockSpec((tk, mp), lambda p, j, k: (k, amap(p)))
    y_spec = pl.BlockSpec((tk, tn), lambda p, j, k: (k, j))
    o_spec = pl.BlockSpec((None, mp, tn), lambda p, j, k: (p, 0, j))
    return _gemm(name, grid, [(x, x_spec, dy, y_spec, 0)], TN, (mp, tn), 1, [],
                 [(jax.ShapeDtypeStruct((npieces, mp, n), BF16), o_spec)], lambda accs, ex: (scale * accs[0],))[0]


def _grad_cols_pieces(n, da, db, npieces, name, comm=None):
    t, d = n.shape
    hp = da.shape[1] // npieces
    tm = _pick(d, (1024, 512, 256, 128))
    tk = _pick(t, (512, 256, 128, 64, 32, 16))
    grid = (npieces, d // tm, t // tk)
    n_spec = pl.BlockSpec((tk, tm), lambda p, i, k: (k, i))
    g_spec = pl.BlockSpec((tk, hp), lambda p, i, k: (k, p))
    o_spec = pl.BlockSpec((None, tm, hp), lambda p, i, k: (p, i, 0))
    osd = jax.ShapeDtypeStruct((npieces, d, hp), BF16)
    return _unpack(_gemm(name, grid, [(n, n_spec, da, g_spec, 0), (n, n_spec, db, g_spec, 1)], TN, (tm, hp), 2, [],
                         [(osd, o_spec)] * 2, lambda accs, ex: (accs[0], accs[1]), comm), comm, False)


def _ffn_bwd_input(da, db, wg, wu, name, comm=None):
    t = da.shape[0]
    npieces, d, hp = wg.shape
    tm = _pick(t, (1024, 512, 256, 128, 64, 32, 16))
    tn = _pick(d, (1024, 512, 256, 128))
    grid = (t // tm, d // tn, npieces)
    g_spec = pl.BlockSpec((tm, hp), lambda i, j, p: (i, p))
    w_spec = pl.BlockSpec((None, tn, hp), lambda i, j, p: (p, j, 0))
    o_spec = pl.BlockSpec((tm, tn), lambda i, j, p: (i, j))
    return _unpack(_gemm(name, grid, [(da, g_spec, wg, w_spec, 0), (db, g_spec, wu, w_spec, 0)], NT, (tm, tn), 1, [],
                         [(jax.ShapeDtypeStruct((t, d), F32), o_spec)], lambda accs, ex: (accs[0],), comm), comm, True)


def _mm2d(a, b, dn, out_dtype, name, tiles, comm=None):
    if dn == NN:
        m, kk = a.shape
        n = b.shape[1]
    elif dn == NT:
        m, kk = a.shape
        n = b.shape[0]
    else:
        kk, m = a.shape
        n = b.shape[1]
    tm = _pick(m, (tiles[0],) + (1024, 512, 256, 128, 64, 32, 16))
    tn = _pick(n, (tiles[1], 768, 1024, 512, 256, 128))
    tk = _pick(kk, (tiles[2], 768, 1024, 512, 256, 128, 64, 32, 16))
    grid = (m // tm, n // tn, kk // tk)
    if dn == TN:
        a_spec = pl.BlockSpec((tk, tm), lambda i, j, k: (k, i))
    else:
        a_spec = pl.BlockSpec((tm, tk), lambda i, j, k: (i, k))
    if dn == NT:
        b_spec = pl.BlockSpec((tn, tk), lambda i, j, k: (j, k))
    else:
        b_spec = pl.BlockSpec((tk, tn), lambda i, j, k: (k, j))
    o_spec = pl.BlockSpec((tm, tn), lambda i, j, k: (i, j))
    return _unpack(_gemm(name, grid, [(a, a_spec, b, b_spec, 0)], dn, (tm, tn), 1, [],
                         [(jax.ShapeDtypeStruct((m, n), out_dtype), o_spec)], lambda accs, ex: (accs[0],), comm), comm, True)


def _row_tile(t):
    return _pick(t, (256, 128, 64, 32, 16, 8))


def _rms_fwd(x, w, name):
    t, d = x.shape
    tm = _row_tile(t)

    def body(x_ref, w_ref, o_ref):
        xv = x_ref[...]
        r = lax.rsqrt(jnp.mean(xv * xv, axis=-1, keepdims=True) + EPS)
        o_ref[...] = (xv * r * w_ref[...]).astype(BF16)

    return pl.pallas_call(
        body, name=name, grid=(t // tm,),
        in_specs=[pl.BlockSpec((tm, d), lambda i: (i, 0)), pl.BlockSpec((1, d), lambda i: (0, 0))],
        out_specs=pl.BlockSpec((tm, d), lambda i: (i, 0)),
        out_shape=jax.ShapeDtypeStruct((t, d), BF16), compiler_params=_params())(x, w.reshape(1, d))


def _rms_bwd(dn, x, w, dres, name):
    t, d = x.shape
    tm = _row_tile(t)

    def body(dn_ref, x_ref, w_ref, r_ref, o_ref, ob_ref, dw_ref):
        i = pl.program_id(0)
        xv = x_ref[...]
        r = lax.rsqrt(jnp.mean(xv * xv, axis=-1, keepdims=True) + EPS)
        xh = xv * r
        dy = dn_ref[...].astype(F32)
        g = dy * w_ref[...]
        dx = r * (g - xh * jnp.mean(g * xh, axis=-1, keepdims=True))
        tot = r_ref[...] + dx
        o_ref[...] = tot
        ob_ref[...] = tot.astype(BF16)
        part = (dy * xh).reshape(tm // 8, 8, d).sum(axis=0)

        @pl.when(i == 0)
        def _():
            dw_ref[...] = part

        @pl.when(i > 0)
        def _():
            dw_ref[...] += part

    row = pl.BlockSpec((tm, d), lambda i: (i, 0))
    return pl.pallas_call(
        body, name=name, grid=(t // tm,),
        in_specs=[row, row, pl.BlockSpec((1, d), lambda i: (0, 0)), row],
        out_specs=[row, row, pl.BlockSpec((8, d), lambda i: (0, 0))],
        out_shape=[jax.ShapeDtypeStruct((t, d), F32), jax.ShapeDtypeStruct((t, d), BF16),
                   jax.ShapeDtypeStruct((8, d), F32)],
        compiler_params=_params())(dn, x, w.reshape(1, d), dres)


def _final_loss(h, w, target, name):
    t, d = h.shape
    tm = _row_tile(t)

    def body(h_ref, w_ref, t_ref, o_ref, ob_ref, dw_ref, ls_ref):
        i = pl.program_id(0)
        xv = h_ref[...]
        r = lax.rsqrt(jnp.mean(xv * xv, axis=-1, keepdims=True) + EPS)
        xh = xv * r
        err = xh * w_ref[...] - t_ref[...]
        dy = err * (1.0 / d)
        g = dy * w_ref[...]
        dx = r * (g - xh * jnp.mean(g * xh, axis=-1, keepdims=True))
        o_ref[...] = dx
        ob_ref[...] = dx.astype(BF16)
        part = (dy * xh).reshape(tm // 8, 8, d).sum(axis=0)
        lpart = (err * err).reshape(tm // 8, 8, d).sum(axis=0)

        @pl.when(i == 0)
        def _():
            dw_ref[...] = part
            ls_ref[...] = lpart

        @pl.when(i > 0)
        def _():
            dw_ref[...] += part
            ls_ref[...] += lpart

    row = pl.BlockSpec((tm, d), lambda i: (i, 0))
    acc = pl.BlockSpec((8, d), lambda i: (0, 0))
    return pl.pallas_call(
        body, name=name, grid=(t // tm,),
        in_specs=[row, pl.BlockSpec((1, d), lambda i: (0, 0)), row],
        out_specs=[row, row, acc, acc],
        out_shape=[jax.ShapeDtypeStruct((t, d), F32), jax.ShapeDtypeStruct((t, d), BF16),
                   jax.ShapeDtypeStruct((8, d), F32), jax.ShapeDtypeStruct((8, d), F32)],
        compiler_params=_params())(h, w.reshape(1, d), target)


def _cast_split_cols(w, hp, me, name):
    r, fs = w.shape
    v1 = fs - hp
    tm = _pick(r, (256, 128, 64, 32, 16))

    def body(me_ref, w_ref, o_ref):
        o_ref[0] = w_ref[:, :hp].astype(BF16)
        if v1 < hp:
            o_ref[1] = jnp.zeros((tm, hp), BF16)
        o_ref[1, :, :v1] = w_ref[:, hp:].astype(BF16)

    gs = pltpu.PrefetchScalarGridSpec(
        num_scalar_prefetch=1, grid=(r // tm,),
        in_specs=[pl.BlockSpec((tm, fs), lambda i, mr: (i, 0))],
        out_specs=pl.BlockSpec((2, None, tm, hp), lambda i, mr: (0, mr[0], i, 0)))
    return pl.pallas_call(body, name=name, grid_spec=gs, out_shape=jax.ShapeDtypeStruct((2, N_CHIPS, r, hp), BF16),
                          compiler_params=_params())(me, w)


def _cast_split_cols_t(wt, hp, me, name):
    fs, r = wt.shape
    v1 = fs - hp
    tm = _pick(r, (256, 128))

    def body(me_ref, w_ref, o_ref):
        o_ref[0] = w_ref[:hp, :].T.astype(BF16)
        if v1 < hp:
            o_ref[1] = jnp.zeros((tm, hp), BF16)
        o_ref[1, :, :v1] = w_ref[hp:, :].T.astype(BF16)

    gs = pltpu.PrefetchScalarGridSpec(
        num_scalar_prefetch=1, grid=(r // tm,),
        in_specs=[pl.BlockSpec((fs, tm), lambda i, mr: (0, i))],
        out_specs=pl.BlockSpec((2, None, tm, hp), lambda i, mr: (0, mr[0], i, 0)))
    return pl.pallas_call(body, name=name, grid_spec=gs, out_shape=jax.ShapeDtypeStruct((2, N_CHIPS, r, hp), BF16),
                          compiler_params=_params())(me, wt)


def _cast_split_rows(w, hp, tr, me, name):
    fs, c = w.shape
    nvalid = fs // tr
    per = hp // tr

    def body(me_ref, w_ref, o_ref):
        i = pl.program_id(0)

        @pl.when(i < nvalid)
        def _():
            o_ref[...] = w_ref[...].astype(BF16)

        @pl.when(i >= nvalid)
        def _():
            o_ref[...] = jnp.zeros(o_ref.shape, BF16)

    gs = pltpu.PrefetchScalarGridSpec(
        num_scalar_prefetch=1, grid=(2 * per,),
        in_specs=[pl.BlockSpec((tr, c), lambda i, mr: (jnp.minimum(i, nvalid - 1), 0))],
        out_specs=pl.BlockSpec((None, None, tr, c), lambda i, mr: (i // per, mr[0], i % per, 0)))
    return pl.pallas_call(body, name=name, grid_spec=gs, out_shape=jax.ShapeDtypeStruct((2, N_CHIPS, hp, c), BF16),
                          compiler_params=_params())(me, w)


def _combine_windows(wall, tables, n_tiles, name):
    _, _, d, wh = wall.shape
    tpw = wh // LANE

    def body(tab_ref, a_ref, b_ref, o_ref):
        t = pl.program_id(0)
        both = tab_ref[6, t] == 1
        av = a_ref[...]
        bv = b_ref[...]
        o_ref[...] = jnp.where(both, av + bv, av)

    def amap(t, tab):
        return (tab[0, t], tab[1, t], 0, tab[2, t])

    def bmap(t, tab):
        return (tab[3, t], tab[4, t], 0, tab[5, t])

    gs = pltpu.PrefetchScalarGridSpec(
        num_scalar_prefetch=1, grid=(n_tiles,),
        in_specs=[pl.BlockSpec((None, None, d, LANE), amap), pl.BlockSpec((None, None, d, LANE), bmap)],
        out_specs=pl.BlockSpec((d, LANE), lambda t, tab: (0, t)))
    del tpw
    return pl.pallas_call(body, name=name, grid_spec=gs, out_shape=jax.ShapeDtypeStruct((d, n_tiles * LANE), BF16),
                          compiler_params=_params())(tables, wall, wall)


def _coords():
    return lax.axis_index("x"), lax.axis_index("y"), lax.axis_index("c")


def _remote(src, dst, ssem, rsem, dev):
    return pltpu.make_async_remote_copy(src_ref=src, dst_ref=dst, send_sem=ssem, recv_sem=rsem, device_id=dev,
                                        device_id_type=MESH)


def _mesh_places():
    x, y, c = _coords()
    return c, 2 * x + y, (x, y, 1 - c), [(1 - x, y), (x, 1 - y), (1 - x, 1 - y)]


def _all_gather_comm(bufs):
    n = len(bufs)

    def start(ins, outs, sems):
        c, me, _, chips = _mesh_places()
        for i in range(n):
            for j, (px, py) in enumerate(chips):
                mine = outs[i].at[c, me]
                _remote(mine, mine, sems[0].at[i, j], sems[1].at[i, j], (px, py, c)).start()

    def mid(ins, outs, sems):
        c, _, sib, chips = _mesh_places()
        for i in range(n):
            for j, (px, py) in enumerate(chips):
                slot = outs[i].at[c, 2 * px + py]
                _remote(slot, slot, sems[0].at[i, j], sems[1].at[i, j], (px, py, c)).wait_recv()
                _remote(slot, slot, sems[2].at[i, j], sems[3].at[i, j], sib).start()

    def finish(ins, outs, sems):
        c, me, sib, chips = _mesh_places()
        for i in range(n):
            for j, (px, py) in enumerate(chips):
                slot = outs[i].at[1 - c, 2 * px + py]
                _remote(slot, slot, sems[2].at[i, j], sems[3].at[i, j], sib).wait_recv()
        for i in range(n):
            for j, (px, py) in enumerate(chips):
                mine = outs[i].at[c, me]
                _remote(mine, mine, sems[0].at[i, j], sems[1].at[i, j], (px, py, c)).wait_send()
                slot = outs[i].at[c, 2 * px + py]
                _remote(slot, slot, sems[2].at[i, j], sems[3].at[i, j], sib).wait_send()

    return _Comm(list(bufs), [jax.ShapeDtypeStruct(b.shape, b.dtype) for b in bufs], {i: i for i in range(n)},
                 [pltpu.SemaphoreType.DMA((n, 3))] * 4, start, mid, finish)


def _chip_all_to_all_comm(ps):
    n = len(ps)

    def start(ins, outs, sems):
        c, me, _, chips = _mesh_places()
        for i in range(n):
            for j, (px, py) in enumerate(chips):
                _remote(ins[i].at[2 * px + py], outs[i].at[me], sems[0].at[i, j], sems[1].at[i, j], (px, py, c)).start()

    def finish(ins, outs, sems):
        c, me, _, chips = _mesh_places()
        for i in range(n):
            for j, (px, py) in enumerate(chips):
                slot = outs[i].at[2 * px + py]
                _remote(slot, slot, sems[0].at[i, j], sems[1].at[i, j], (px, py, c)).wait_recv()
        for i in range(n):
            for j, (px, py) in enumerate(chips):
                _remote(ins[i].at[2 * px + py], outs[i].at[me], sems[0].at[i, j], sems[1].at[i, j],
                        (px, py, c)).wait_send()

    return _Comm(list(ps), [jax.ShapeDtypeStruct(p.shape, p.dtype) for p in ps], {},
                 [pltpu.SemaphoreType.DMA((n, 3))] * 2, start, None, finish)


def _comm_call(comm, name):
    n_in, n_out = len(comm.ins), len(comm.out_shape)

    def body(*refs):
        ins, outs, sems = refs[:n_in], refs[n_in:n_in + n_out], refs[n_in + n_out:]
        comm.start(ins, outs, sems)
        if comm.mid:
            comm.mid(ins, outs, sems)
        comm.finish(ins, outs, sems)

    return pl.pallas_call(body, name=name, in_specs=[ANY] * n_in, out_specs=[ANY] * n_out, out_shape=comm.out_shape,
                          input_output_aliases=dict(comm.aliases), scratch_shapes=list(comm.sems))(*comm.ins)


def _sibling_take(gs, name):
    n = len(gs)

    def body(*refs):
        g, out = refs[:n], refs[n:2 * n]
        ssem, rsem = refs[2 * n:]
        x, y, c = _coords()
        sib = (x, y, 1 - c)
        cps = []
        for i in range(n):
            cp = _remote(g[i].at[1 - c], out[i], ssem.at[i], rsem.at[i], sib)
            cp.start()
            cps.append(cp)
        for cp in cps:
            cp.wait()

    out_shape = [jax.ShapeDtypeStruct(s.shape[1:], s.dtype) for s in gs]
    return pl.pallas_call(
        body, name=name, in_specs=[ANY] * n, out_specs=[ANY] * n, out_shape=out_shape,
        scratch_shapes=[pltpu.SemaphoreType.DMA((n,)), pltpu.SemaphoreType.DMA((n,))])(*gs)


def _sibling_join(bufs, name):
    n = len(bufs)

    def body(*refs):
        out = refs[n:2 * n]
        ssem, rsem = refs[2 * n:]
        x, y, c = _coords()
        sib = (x, y, 1 - c)
        cps = []
        for i in range(n):
            mine = out[i].at[c]
            cp = _remote(mine, mine, ssem.at[i], rsem.at[i], sib)
            cp.start()
            cps.append(cp)
        for i in range(n):
            slot = out[i].at[1 - c]
            _remote(slot, slot, ssem.at[i], rsem.at[i], sib).wait_recv()
        for cp in cps:
            cp.wait_send()

    out_shape = [jax.ShapeDtypeStruct(b.shape, b.dtype) for b in bufs]
    return pl.pallas_call(
        body, name=name, in_specs=[ANY] * n, out_specs=[ANY] * n, out_shape=out_shape,
        input_output_aliases={i: i for i in range(n)},
        scratch_shapes=[pltpu.SemaphoreType.DMA((n,)), pltpu.SemaphoreType.DMA((n,))])(*bufs)


def _allreduce_small(vec, name):
    r = vec.shape[0]

    def body(v_ref, o_ref, buf, ssem, rsem):
        x, y, c = _coords()
        my = 4 * x + 2 * y + c
        buf[my] = v_ref[...]
        cps = []
        for dd in range(1, N_DEV):
            px = 1 - x if (dd >> 2) & 1 else x
            py = 1 - y if (dd >> 1) & 1 else y
            pc = 1 - c if dd & 1 else c
            cp = _remote(v_ref, buf.at[my], ssem.at[dd - 1], rsem.at[dd - 1], (px, py, pc))
            cp.start()
            cps.append(cp)
        for dd in range(1, N_DEV):
            px = 1 - x if (dd >> 2) & 1 else x
            py = 1 - y if (dd >> 1) & 1 else y
            pc = 1 - c if dd & 1 else c
            slot = buf.at[4 * px + 2 * py + pc]
            _remote(slot, slot, ssem.at[dd - 1], rsem.at[dd - 1], (px, py, pc)).wait_recv()
        tot = buf[0]
        for k in range(1, N_DEV):
            tot = tot + buf[k]
        o_ref[...] = tot
        for cp in cps:
            cp.wait_send()

    vm = pl.BlockSpec(memory_space=pltpu.VMEM)
    return pl.pallas_call(
        body, name=name, in_specs=[vm], out_specs=vm, out_shape=jax.ShapeDtypeStruct((r, LANE), F32),
        scratch_shapes=[pltpu.VMEM((N_DEV, r, LANE), F32), pltpu.SemaphoreType.DMA((N_DEV - 1,)),
                        pltpu.SemaphoreType.DMA((N_DEV - 1,))])(vec)


def _pair_sum(g, l1, cidx, name):
    _, r, c = g.shape
    tr = _pick(r, (512, 256, 128, 64, 32, 16))

    def body(c_ref, g_ref, l_ref, o_ref):
        o_ref[...] = (g_ref[...].astype(F32) + l_ref[...].astype(F32)).astype(BF16)

    gs = pltpu.PrefetchScalarGridSpec(
        num_scalar_prefetch=1, grid=(r // tr,),
        in_specs=[pl.BlockSpec((None, tr, c), lambda i, cr: (cr[0], i, 0)), pl.BlockSpec((tr, c), lambda i, cr: (i, 0))],
        out_specs=pl.BlockSpec((tr, c), lambda i, cr: (i, 0)))
    return pl.pallas_call(body, name=name, grid_spec=gs, out_shape=jax.ShapeDtypeStruct((r, c), BF16),
                          compiler_params=_params())(cidx, g, l1)


def _chip_sum(p, l2, mc, name):
    _, r, c = l2.shape
    tr = _pick(r, (256, 128, 64, 32, 16))

    def body(mc_ref, p_ref, l0, l1, l2_, l3, o_ref):
        me = mc_ref[0]
        pv = p_ref[...].astype(F32)
        tot = None
        for k, lr in enumerate((l0, l1, l2_, l3)):
            term = jnp.where(me == k, pv, lr[...].astype(F32))
            tot = term if tot is None else tot + term
        o_ref[...] = tot

    def other(k):
        return lambda i, mr: (jnp.where(mr[0] == k, (k + 1) % N_CHIPS, k), i, 0)

    gs = pltpu.PrefetchScalarGridSpec(
        num_scalar_prefetch=1, grid=(r // tr,),
        in_specs=[pl.BlockSpec((None, tr, c), lambda i, mr: (mr[0], i, 0))]
        + [pl.BlockSpec((None, tr, c), other(k)) for k in range(N_CHIPS)],
        out_specs=pl.BlockSpec((None, tr, c), lambda i, mr: (mr[1], i, 0)))
    return pl.pallas_call(body, name=name, grid_spec=gs, out_shape=jax.ShapeDtypeStruct((2, r, c), F32),
                          compiler_params=_params())(mc, p, l2, l2, l2, l2)


def _adamw(g, w, m, v, name):
    r, c = w.shape
    tr = r
    if r * c * 4 > (2 << 20):
        tr = next(p for p in (256, 128, 64, 32, 16, 8) if r % p == 0 and (p * c * 4 <= (2 << 20) or p == 8))

    def body(g_ref, w_ref, m_ref, v_ref, d_ref, nm_ref, nv_ref):
        gv = g_ref[...]
        mn = ADAM_B1 * m_ref[...] + (1.0 - ADAM_B1) * gv
        vn = ADAM_B2 * v_ref[...] + (1.0 - ADAM_B2) * (gv * gv)
        m_hat = mn / (1.0 - ADAM_B1 ** ADAM_STEP)
        v_hat = vn / (1.0 - ADAM_B2 ** ADAM_STEP)
        d_ref[...] = -ADAM_LR * (m_hat / (jnp.sqrt(v_hat) + ADAM_EPS) + ADAM_WD * w_ref[...])
        nm_ref[...] = mn
        nv_ref[...] = vn

    blk = pl.BlockSpec((tr, c), lambda i: (i, 0))
    osd = jax.ShapeDtypeStruct((r, c), F32)
    return pl.pallas_call(body, name=name, grid=(r // tr,), in_specs=[blk] * 4, out_specs=[blk] * 3,
                          out_shape=[osd] * 3, compiler_params=_params())(g, w, m, v)


def _attn_probs(q, k, q0, s_len):
    tq = q.shape[0]
    sc = lax.dot_general(q, k, NT, preferred_element_type=F32) * (HEAD_DIM ** -0.5)
    dlt = (q0 + lax.broadcasted_iota(jnp.int32, (tq, s_len), 0)) - lax.broadcasted_iota(jnp.int32, (tq, s_len), 1)
    cnt = jnp.zeros((tq, s_len), F32)
    for window, dil in DILATED_CONFIGS:
        seen = (dlt >= 0) & (dlt <= window) & ((dlt & (dil - 1)) == 0)
        cnt = cnt + jnp.where(seen, 1.0, 0.0)
    live = cnt > 0.0
    m = jnp.max(jnp.where(live, sc, -jnp.inf), axis=-1, keepdims=True)
    p = cnt * jnp.exp(jnp.where(live, sc - m, -jnp.inf))
    return p / jnp.sum(p, axis=-1, keepdims=True)


def _attn_key_groups(nq):
    return next(g for g in (4, 2, 1) if nq % g == 0)


def _attn_fwd(proj, nh, s_len, name, comm):
    t = proj.shape[0]
    tq = min(256, s_len)
    nq = s_len // tq
    ng = _attn_key_groups(nq)
    per = nq // ng

    def body(q_ref, k_ref, v_ref, o_ref):
        qi = pl.program_id(2)
        for j in range(ng):
            klen = (j + 1) * per * tq

            @pl.when(qi // per == j)
            def _(klen=klen):
                p = _attn_probs(q_ref[...].astype(BF16), k_ref[:klen, :].astype(BF16), qi * tq, klen)
                o_ref[...] = jnp.dot(p.astype(BF16), v_ref[:klen, :].astype(BF16), preferred_element_type=F32)

    q_spec = pl.BlockSpec((tq, HEAD_DIM), lambda b, h, qi: (b * nq + qi, h))
    outs, couts = _call_carrying(
        body, name, (t // s_len, nh, nq),
        [q_spec, pl.BlockSpec((s_len, HEAD_DIM), lambda b, h, qi: (b, nh + h)),
         pl.BlockSpec((s_len, HEAD_DIM), lambda b, h, qi: (b, 2 * nh + h))],
        [q_spec], [jax.ShapeDtypeStruct((t, nh * HEAD_DIM), F32)], [], (proj, proj, proj), comm)
    return outs[0], couts


def _attn_bwd(proj, o, do, nh, s_len, name):
    t = proj.shape[0]
    tq = min(256, s_len)
    nq = s_len // tq
    ng = _attn_key_groups(nq)
    per = nq // ng
    scale = HEAD_DIM ** -0.5

    def body(q_ref, k_ref, v_ref, o_ref, do_ref, dq_ref, dk_ref, dv_ref, dk_acc, dv_acc):
        qi = pl.program_id(2)

        @pl.when(qi == 0)
        def _():
            dk_acc[...] = jnp.zeros(dk_acc.shape, F32)
            dv_acc[...] = jnp.zeros(dv_acc.shape, F32)

        for j in range(ng):
            klen = (j + 1) * per * tq

            @pl.when(qi // per == j)
            def _(klen=klen):
                q = q_ref[...].astype(BF16)
                k = k_ref[:klen, :].astype(BF16)
                p = _attn_probs(q, k, qi * tq, klen)
                dob = do_ref[...]
                dp = lax.dot_general(dob, v_ref[:klen, :].astype(BF16), NT, preferred_element_type=F32)
                delta = jnp.sum(dob.astype(F32) * o_ref[...], axis=-1, keepdims=True)
                ds = (p * (dp - delta)).astype(BF16)
                dq_ref[...] = (jnp.dot(ds, k, preferred_element_type=F32) * scale).astype(BF16)
                dk_acc[:klen, :] += lax.dot_general(ds, q, TN, preferred_element_type=F32) * scale
                dv_acc[:klen, :] += lax.dot_general(p.astype(BF16), dob, TN, preferred_element_type=F32)

        @pl.when(qi == nq - 1)
        def _():
            dk_ref[...] = dk_acc[...].astype(BF16)
            dv_ref[...] = dv_acc[...].astype(BF16)

    q_spec = pl.BlockSpec((tq, HEAD_DIM), lambda b, h, qi: (b * nq + qi, h))
    kv_out = pl.BlockSpec((s_len, HEAD_DIM), lambda b, h, qi: (b, h))
    osd = jax.ShapeDtypeStruct((t, nh * HEAD_DIM), BF16)
    return pl.pallas_call(
        body, name=name, grid=(t // s_len, nh, nq),
        in_specs=[q_spec, pl.BlockSpec((s_len, HEAD_DIM), lambda b, h, qi: (b, nh + h)),
                  pl.BlockSpec((s_len, HEAD_DIM), lambda b, h, qi: (b, 2 * nh + h)), q_spec, q_spec],
        out_specs=[q_spec, kv_out, kv_out], out_shape=[osd, osd, osd],
        scratch_shapes=[pltpu.VMEM((s_len, HEAD_DIM), F32), pltpu.VMEM((s_len, HEAD_DIM), F32)],
        compiler_params=_params())(proj, proj, proj, o, do)


def _conv_taps(x, w_ref, s_len):
    row = lax.broadcasted_iota(jnp.int32, x.shape, 0)
    c = w_ref[CONV_WIDTH - 1:CONV_WIDTH, :] * x
    for j in range(1, CONV_WIDTH):
        xs = jnp.where(row >= j, pltpu.roll(x, j, 0), 0.0)
        c = c + w_ref[CONV_WIDTH - 1 - j:CONV_WIDTH - j, :] * xs
    return c


def _conv_fwd(proj, conv8, col0, width, s_len, name):
    t = proj.shape[0]
    cb = _pick(width, (512, 256, 128))
    c0 = col0 // cb

    def body(x_ref, w_ref, o_ref):
        c = _conv_taps(x_ref[...], w_ref, s_len)
        o_ref[...] = c * _sigmoid(c)

    return pl.pallas_call(
        body, name=name, grid=(t // s_len, width // cb),
        in_specs=[pl.BlockSpec((s_len, cb), lambda b, j: (b, c0 + j)), pl.BlockSpec((8, cb), lambda b, j: (0, j))],
        out_specs=pl.BlockSpec((s_len, cb), lambda b, j: (b, j)),
        out_shape=jax.ShapeDtypeStruct((t, width), F32), compiler_params=_params())(proj, conv8)


def _conv_bwd(proj, conv8, du, col0, width, s_len, name):
    t = proj.shape[0]
    cb = _pick(width, (512, 256, 128))
    c0 = col0 // cb

    def body(x_ref, w_ref, du_ref, dx_ref, dw_ref):
        b = pl.program_id(1)
        x = x_ref[...]
        c = _conv_taps(x, w_ref, s_len)
        sg = _sigmoid(c)
        dc = du_ref[...] * (sg * (1.0 + c * (1.0 - sg)))
        row = lax.broadcasted_iota(jnp.int32, x.shape, 0)
        dx = w_ref[CONV_WIDTH - 1:CONV_WIDTH, :] * dc
        rows = [jnp.sum(dc * x, axis=0, keepdims=True)]
        for j in range(1, CONV_WIDTH):
            up = jnp.where(row < s_len - j, pltpu.roll(dc, s_len - j, 0), 0.0)
            dx = dx + w_ref[CONV_WIDTH - 1 - j:CONV_WIDTH - j, :] * up
            xs = jnp.where(row >= j, pltpu.roll(x, j, 0), 0.0)
            rows.append(jnp.sum(dc * xs, axis=0, keepdims=True))
        dx_ref[...] = dx.astype(BF16)
        part = jnp.concatenate(rows[::-1] + [jnp.zeros((8 - CONV_WIDTH, cb), F32)], axis=0)

        @pl.when(b == 0)
        def _():
            dw_ref[...] = part

        @pl.when(b > 0)
        def _():
            dw_ref[...] += part

    return pl.pallas_call(
        body, name=name, grid=(width // cb, t // s_len),
        in_specs=[pl.BlockSpec((s_len, cb), lambda j, b: (b, c0 + j)), pl.BlockSpec((8, cb), lambda j, b: (0, j)),
                  pl.BlockSpec((s_len, cb), lambda j, b: (b, j))],
        out_specs=[pl.BlockSpec((s_len, cb), lambda j, b: (b, j)), pl.BlockSpec((8, cb), lambda j, b: (0, j))],
        out_shape=[jax.ShapeDtypeStruct((t, width), BF16), jax.ShapeDtypeStruct((8, width), F32)],
        compiler_params=_params())(proj, conv8, du)


def _split_bf16(v):
    hi = v.astype(BF16)
    return hi, (v - hi.astype(F32)).astype(BF16)


def _bdot(a, b, dims):
    return lax.dot_general(a, b, (dims, ((0,), (0,))), preferred_element_type=F32)


def _dot3(a, b, dims, exact_a=False):
    ah, al = _split_bf16(a)
    bh, bl = _split_bf16(b)
    out = _bdot(ah, bh, dims) + _bdot(ah, bl, dims)
    return out if exact_a else out + _bdot(al, bh, dims)


@functools.partial(jax.custom_vjp, nondiff_argnums=(2,))
def _bmm(a, b, exact_a=False):
    return _dot3(a, b, ((2,), (1,)), exact_a)


def _bmm_fwd(a, b, exact_a):
    return _dot3(a, b, ((2,), (1,)), exact_a), (a, b)


def _bmm_bwd(exact_a, res, ct):
    a, b = res
    da = jnp.zeros_like(a) if exact_a else _dot3(ct, b, ((2,), (2,)))
    db = _dot3(a, ct, ((1,), (1,)), exact_a)
    return da, db


_bmm.defvjp(_bmm_fwd, _bmm_bwd)


@jax.custom_vjp
def _bmm_nt(a, b):
    return _bdot(a.astype(BF16), b.astype(BF16), ((2,), (2,)))


def _bmm_nt_fwd(a, b):
    return _bmm_nt(a, b), (a, b)


def _bmm_nt_bwd(res, ct):
    a, b = res
    ctb = ct.astype(BF16)
    return _bdot(ctb, b.astype(BF16), ((2,), (1,))), _bdot(ctb, a.astype(BF16), ((1,), (1,)))


_bmm_nt.defvjp(_bmm_nt_fwd, _bmm_nt_bwd)


def _unit_lower_inverse(nm):
    c = nm.shape[-1]
    eye = (lax.broadcasted_iota(jnp.int32, (c, c), 0) == lax.broadcasted_iota(jnp.int32, (c, c), 1)).astype(F32)
    x = -nm
    inv = eye[None] + x
    p = x
    for _ in range(int(math.log2(c)) - 1):
        p = _bmm(p, p)
        inv = inv + _bmm(inv, p)
    return inv


def _dn_chunk_terms(uq, uk, uv, a_col, b_col, alog, dtb):
    n, c, dh = uq.shape
    q = uq * lax.rsqrt(jnp.sum(uq * uq, axis=-1, keepdims=True) + EPS) * (HEAD_DIM ** -0.5)
    k = uk * lax.rsqrt(jnp.sum(uk * uk, axis=-1, keepdims=True) + EPS)
    beta = _sigmoid(b_col)
    xa = a_col + dtb
    g = -jnp.exp(alog) * (jnp.maximum(xa, 0.0) + jnp.log(1.0 + jnp.exp(-jnp.abs(xa))))
    ri = lax.broadcasted_iota(jnp.int32, (c, c), 0)
    ci = lax.broadcasted_iota(jnp.int32, (c, c), 1)
    incl = ri >= ci
    strict = ri > ci
    l_incl = jnp.broadcast_to(incl.astype(F32)[None], (n, c, c))
    gb = jnp.broadcast_to(g, (n, c, dh))
    l_sums = jnp.broadcast_to(jnp.concatenate([incl.astype(F32), jnp.ones((dh - c, c), F32)], axis=0)[None], (n, dh, c))
    sums = _bmm(l_sums, gb, True)
    gc, gtot = sums[:, :c], sums[:, c:2 * c]
    gdiff = _bmm(l_incl, jnp.broadcast_to(g, (n, c, c)) * strict.astype(F32)[None], True)
    decay = jnp.where(incl[None], jnp.exp(jnp.where(incl[None], gdiff, 0.0)), 0.0)
    kb = k * beta
    nm = jnp.where(strict[None], _bmm_nt(kb, k) * decay, 0.0)
    tinv = _unit_lower_inverse(nm)
    w = _bmm(tinv, kb * jnp.exp(gc))
    u = _bmm(tinv, uv * beta)
    qk = _bmm_nt(q, k) * decay
    q_dec = q * jnp.exp(gc)
    k_dec = k * jnp.exp(gtot - gc)
    g_last = jnp.exp(jnp.concatenate([gtot] * (dh // c), axis=1))
    return w, u, qk, q_dec, k_dec, g_last


DN_SUB = 8


def _dn_gather_inputs(uq_ref, uk_ref, uv_ref, ba_ref, prm_ref, h, nh, rows, nb):
    ba = ba_ref[rows, :]
    lane = lax.broadcasted_iota(jnp.int32, ba.shape, 1)
    b_col = jnp.sum(jnp.where(lane == h, ba, 0.0), axis=-1, keepdims=True).reshape(nb, CHUNK, 1)
    a_col = jnp.sum(jnp.where(lane == nh + h, ba, 0.0), axis=-1, keepdims=True).reshape(nb, CHUNK, 1)
    lane1 = lax.broadcasted_iota(jnp.int32, (1, LANE), 1)
    alog = jnp.sum(jnp.where(lane1 == h, prm_ref[1:2, :], 0.0), axis=-1, keepdims=True)
    dtb = jnp.sum(jnp.where(lane1 == h, prm_ref[2:3, :], 0.0), axis=-1, keepdims=True)
    shp = (nb, CHUNK, HEAD_DIM)
    return (uq_ref[rows, :].reshape(shp), uk_ref[rows, :].reshape(shp), uv_ref[rows, :].reshape(shp),
            a_col, b_col, alog, dtb)


def _dn_fill_terms(in_refs, h, nh, n, term_refs):
    nb = min(DN_SUB, n)

    def sub(i, carry):
        rows = pl.ds(pl.multiple_of(i * (nb * CHUNK), nb * CHUNK), nb * CHUNK)
        terms = _dn_chunk_terms(*_dn_gather_inputs(*in_refs, h, nh, rows, nb))
        for r, v in zip(term_refs, terms):
            r[pl.ds(i * nb, nb)] = v
        return carry

    lax.fori_loop(0, n // nb, sub, 0)


def _dn_scan(terms_refs, o_ref, st_ref, n):
    w_ref, u_ref, qk_ref, qd_ref, kd_ref, gl_ref = terms_refs

    def step(i, state):
        if st_ref is not None:
            st_ref[i] = state
        sb = state.astype(BF16)
        v_new = u_ref[i] - jnp.dot(w_ref[i].astype(BF16), sb, preferred_element_type=F32)
        vb = v_new.astype(BF16)
        o_ref[i] = (jnp.dot(qd_ref[i].astype(BF16), sb, preferred_element_type=F32)
                    + jnp.dot(qk_ref[i].astype(BF16), vb, preferred_element_type=F32))
        return state * gl_ref[i] + lax.dot_general(kd_ref[i].astype(BF16), vb, TN, preferred_element_type=F32)

    lax.fori_loop(0, n, step, jnp.zeros((HEAD_DIM, HEAD_DIM), F32), unroll=2)


def _dn_specs(nh, nh_a, s_len, zc0, bac):
    head = lambda off: pl.BlockSpec((s_len, HEAD_DIM), lambda b, h: (b, off + h))
    return dict(uq=head(0), uk=head(nh), uv=head(2 * nh), z=head(zc0),
                ba=pl.BlockSpec((s_len, LANE), lambda b, h: (b, bac)),
                prm=pl.BlockSpec((8, LANE), lambda b, h: (0, 0)), dout=head(nh_a), out=head(0))


def _dn_scratch(n, with_states):
    big = pltpu.VMEM((n, CHUNK, HEAD_DIM), F32)
    sc = [big, big, pltpu.VMEM((n, CHUNK, CHUNK), F32), big, big, pltpu.VMEM((n, HEAD_DIM, HEAD_DIM), F32), big]
    if with_states:
        sc.append(pltpu.VMEM((n, HEAD_DIM, HEAD_DIM), F32))
    return sc


def _dn_fwd(u, proj, prm, nh, s_len, zc0, bac, name, comm):
    t = u.shape[0]
    n = s_len // CHUNK
    sp = _dn_specs(nh, 0, s_len, zc0, bac)

    def body(uq_ref, uk_ref, uv_ref, z_ref, ba_ref, prm_ref, o_ref, *scr):
        h = pl.program_id(1)
        _dn_fill_terms((uq_ref, uk_ref, uv_ref, ba_ref, prm_ref), h, nh, n, scr[:6])
        _dn_scan(scr[:6], scr[6], None, n)
        o = scr[6][...].reshape(s_len, HEAD_DIM)
        z = z_ref[...]
        r = lax.rsqrt(jnp.mean(o * o, axis=-1, keepdims=True) + EPS)
        o_ref[...] = o * r * prm_ref[0:1, :] * (z * _sigmoid(z))

    outs, couts = _call_carrying(
        body, name, (t // s_len, nh), [sp["uq"], sp["uk"], sp["uv"], sp["z"], sp["ba"], sp["prm"]], [sp["out"]],
        [jax.ShapeDtypeStruct((t, nh * HEAD_DIM), F32)], _dn_scratch(n, False), (u, u, u, proj, proj, prm), comm)
    return outs[0], couts


def _dn_bwd(u, proj, prm, dcat, nh, nh_a, s_len, zc0, bac, name):
    t = u.shape[0]
    n = s_len // CHUNK
    sp = _dn_specs(nh, nh_a, s_len, zc0, bac)

    def body(uq_ref, uk_ref, uv_ref, z_ref, ba_ref, prm_ref, do_ref,
             duq_ref, duk_ref, duv_ref, dz_ref, dba_ref, dprm_ref, *scr):
        b, h = pl.program_id(0), pl.program_id(1)
        in_refs = (uq_ref, uk_ref, uv_ref, ba_ref, prm_ref)
        w_ref, u_ref, qk_ref, qd_ref, kd_ref, gl_ref, o_scr, st_ref = scr
        _dn_fill_terms(in_refs, h, nh, n, scr[:6])
        _dn_scan(scr[:6], o_scr, st_ref, n)

        o = o_scr[...].reshape(s_len, HEAD_DIM)
        z = z_ref[...]
        dout = do_ref[...].astype(F32)
        gain = prm_ref[0:1, :]
        sg = _sigmoid(z)
        sz = z * sg
        r = lax.rsqrt(jnp.mean(o * o, axis=-1, keepdims=True) + EPS)
        oh = o * r
        dgain = jnp.sum(dout * oh * sz, axis=0, keepdims=True)
        dz_ref[...] = (dout * oh * gain * (sg * (1.0 + z * (1.0 - sg)))).astype(BF16)
        doh = dout * gain * sz
        d_o = r * (doh - oh * jnp.mean(doh * oh, axis=-1, keepdims=True))
        o_scr[...] = d_o.reshape(n, CHUNK, HEAD_DIM)

        def step(j, ds):
            i = n - 1 - j
            st = st_ref[i]
            sb = st.astype(BF16)
            wi, qki, qdi, kdi, gli = w_ref[i], qk_ref[i], qd_ref[i], kd_ref[i], gl_ref[i]
            v_new = u_ref[i] - jnp.dot(wi.astype(BF16), sb, preferred_element_type=F32)
            vb = v_new.astype(BF16)
            don = o_scr[i].astype(BF16)
            dsb = ds.astype(BF16)
            dv = (lax.dot_general(qki.astype(BF16), don, TN, preferred_element_type=F32)
                  + jnp.dot(kdi.astype(BF16), dsb, preferred_element_type=F32))
            dvb = dv.astype(BF16)
            qd_ref[i] = lax.dot_general(don, sb, NT, preferred_element_type=F32)
            qk_ref[i] = lax.dot_general(don, vb, NT, preferred_element_type=F32)
            kd_ref[i] = lax.dot_general(vb, dsb, NT, preferred_element_type=F32)
            gl_ref[i] = ds * st
            u_ref[i] = dv
            w_ref[i] = -lax.dot_general(dvb, sb, NT, preferred_element_type=F32)
            return (ds * gli + lax.dot_general(qdi.astype(BF16), don, TN, preferred_element_type=F32)
                    - lax.dot_general(wi.astype(BF16), dvb, TN, preferred_element_type=F32))

        lax.fori_loop(0, n, step, jnp.zeros((HEAD_DIM, HEAD_DIM), F32), unroll=2)

        @pl.when(h == 0)
        def _():
            dba_ref[...] = jnp.zeros(dba_ref.shape, F32)

        nb = min(DN_SUB, n)

        def sub(i, carry):
            rows = pl.ds(pl.multiple_of(i * (nb * CHUNK), nb * CHUNK), nb * CHUNK)
            _, pull = jax.vjp(_dn_chunk_terms, *_dn_gather_inputs(*in_refs, h, nh, rows, nb))
            duq, duk, duv, da_col, db_col, dal, ddt = pull(tuple(r[pl.ds(i * nb, nb)] for r in scr[:6]))
            duq_ref[rows, :] = duq.reshape(nb * CHUNK, HEAD_DIM)
            duk_ref[rows, :] = duk.reshape(nb * CHUNK, HEAD_DIM)
            duv_ref[rows, :] = duv.reshape(nb * CHUNK, HEAD_DIM)
            lane = lax.broadcasted_iota(jnp.int32, (nb * CHUNK, LANE), 1)
            dba_ref[rows, :] += (jnp.where(lane == h, db_col.reshape(nb * CHUNK, 1), 0.0)
                                 + jnp.where(lane == nh + h, da_col.reshape(nb * CHUNK, 1), 0.0))
            return carry[0] + dal, carry[1] + ddt

        dalog, ddtb = lax.fori_loop(0, n // nb, sub, (jnp.zeros((1, 1), F32), jnp.zeros((1, 1), F32)))
        lane1 = lax.broadcasted_iota(jnp.int32, (1, LANE), 1)
        dprm = jnp.concatenate([dgain, jnp.where(lane1 == h, dalog, 0.0), jnp.where(lane1 == h, ddtb, 0.0),
                                jnp.zeros((5, LANE), F32)], axis=0)

        @pl.when((b == 0) & (h == 0))
        def _():
            dprm_ref[...] = dprm

        @pl.when((b > 0) | (h > 0))
        def _():
            dprm_ref[...] += dprm

    osd = jax.ShapeDtypeStruct((t, nh * HEAD_DIM), F32)
    return pl.pallas_call(
        body, name=name, grid=(t // s_len, nh),
        in_specs=[sp["uq"], sp["uk"], sp["uv"], sp["z"], sp["ba"], sp["prm"], sp["dout"]],
        out_specs=[sp["out"], sp["out"], sp["out"], sp["out"], pl.BlockSpec((s_len, LANE), lambda b, h: (b, 0)),
                   pl.BlockSpec((8, LANE), lambda b, h: (0, 0))],
        out_shape=[osd, osd, osd, jax.ShapeDtypeStruct((t, nh * HEAD_DIM), BF16),
                   jax.ShapeDtypeStruct((t, LANE), F32), jax.ShapeDtypeStruct((8, LANE), F32)],
        scratch_shapes=_dn_scratch(n, True), compiler_params=_params())(u, u, u, proj, proj, prm, dcat)


def _w_in_windows(ws):
    w0 = [(ws * k) // LANE * LANE for k in range(N_CHIPS)]
    sh = [ws * k - w0[k] for k in range(N_CHIPS)]
    ww = _ceil_to(max(sh) + ws, 2 * LANE)
    n_tiles = (w0[-1] + ww) // LANE
    tpw = ww // LANE
    tph = tpw // 2
    tab = np.zeros((7, n_tiles), np.int32)
    for t in range(n_tiles):
        ks = [k for k in range(N_CHIPS) if w0[k] // LANE <= t < w0[k] // LANE + tpw]
        k1 = ks[-1]
        lt = t - w0[k1] // LANE
        tab[0, t], tab[1, t], tab[2, t] = lt // tph, k1, lt % tph
        k2 = ks[0] if len(ks) > 1 else k1
        lt2 = t - w0[k2] // LANE
        tab[3, t], tab[4, t], tab[5, t] = lt2 // tph, k2, lt2 % tph
        tab[6, t] = 1 if len(ks) > 1 else 0
        assert len(ks) <= 2
    return w0, sh, ww, n_tiles, tab


def kernel(x, ffn1_norm, ffn1_w_gate, ffn1_w_up, ffn1_w_down, mix_norm, w_in, conv_w, a_log, dt_bias, dn_norm, w_out, ffn2_norm, ffn2_w_gate, ffn2_w_up, ffn2_w_down, final_norm, loss_target, m_ffn1_norm, m_ffn1_w_gate, m_ffn1_w_up, m_ffn1_w_down, m_mix_norm, m_w_in, m_conv_w, m_a_log, m_dt_bias, m_dn_norm, m_w_out, m_ffn2_norm, m_ffn2_w_gate, m_ffn2_w_up, m_ffn2_w_down, m_final_norm, v_ffn1_norm, v_ffn1_w_gate, v_ffn1_w_up, v_ffn1_w_down, v_mix_norm, v_w_in, v_conv_w, v_a_log, v_dt_bias, v_dn_norm, v_w_out, v_ffn2_norm, v_ffn2_w_gate, v_ffn2_w_up, v_ffn2_w_down, v_final_norm):
    bl, s_, d = x.shape
    t = bl * s_
    fs = ffn1_w_gate.shape[1]
    hp = _ceil_to(-(-fs // 2), LANE)
    ws = w_in.shape[1]
    d_mix = w_out.shape[0] * N_CHIPS
    d_attn = d_dn = d_mix // 2
    nh_d = d_dn // HEAD_DIM
    d_in = 3 * d_attn + 4 * d_dn + 2 * nh_d
    cs = conv_w.shape[1]
    assert ws * N_CHIPS == d_in and cs * N_CHIPS == 3 * d_dn

    xi, yi, ci = lax.axis_index("x"), lax.axis_index("y"), lax.axis_index("c")
    me = 2 * xi + yi
    cidx = jnp.reshape(ci, (1,)).astype(jnp.int32)
    meidx = jnp.reshape(me, (1,)).astype(jnp.int32)
    mcidx = jnp.stack([me, ci]).astype(jnp.int32)

    w0, sh, ww, n_tiles, tab = _w_in_windows(ws)
    shift = (ws * me) % LANE
    w_in_win_t = lax.dynamic_update_slice(jnp.zeros((ww, d), F32), w_in.T, (shift, jnp.int32(0)))
    rows_tr = math.gcd(hp, fs)
    conv_piece = jnp.pad(conv_w, ((0, 8 - CONV_WIDTH), (0, 0))).reshape(8, 2, cs // 2).transpose(1, 0, 2)
    z0 = jnp.int32(0)
    pieces = [
        _cast_split_cols_t(ffn1_w_gate.T, hp, meidx, "cast_g1"),
        _cast_split_cols_t(ffn1_w_up.T, hp, meidx, "cast_u1"),
        _cast_split_rows(ffn1_w_down, hp, rows_tr, meidx, "cast_d1"),
        _cast_split_cols_t(w_in_win_t, ww // 2, meidx, "cast_in"),
        _cast_split_rows(w_out, w_out.shape[0] // 2, w_out.shape[0] // 2, meidx, "cast_out"),
        _cast_split_cols_t(ffn2_w_gate.T, hp, meidx, "cast_g2"),
        _cast_split_cols_t(ffn2_w_up.T, hp, meidx, "cast_u2"),
        _cast_split_rows(ffn2_w_down, hp, rows_tr, meidx, "cast_d2"),
        lax.dynamic_update_slice(jnp.zeros((2, N_CHIPS, 8, cs // 2), F32), conv_piece[:, None], (z0, me, z0, z0)),
    ]
    p_g1, p_u1, p_d1, p_in, p_out, p_g2, p_u2, p_d2, p_conv = pieces
    npc = 8
    ident = lambda p: p
    cat_map = lambda p: 2 * (p % N_CHIPS) + p // N_CHIPS
    as_cols = lambda a: a.reshape(npc, d, hp)
    as_rows = lambda a: a.reshape(npc, hp, d)

    wg1, wu1 = _comm_call(_all_gather_comm([p_g1, p_u1]), "all_gather_first")
    wg1, wu1 = as_cols(wg1), as_cols(wu1)
    h0 = x.reshape(t, d)
    n1 = _rms_fwd(h0, ffn1_norm, "rms1")
    (a1, b1, s1), (wd1, win_all) = _ffn_up(n1, wg1, wu1, "ffn1_up", comm=_all_gather_comm([p_d1, p_in]))
    wd1 = as_rows(wd1)
    h1, (wout, conv_all) = _mm_pieces_resid(s1, wd1, h0, 0.5, ident, "ffn1_down",
                                            comm=_all_gather_comm([p_out, p_conv]))
    wout = wout.reshape(npc, w_out.shape[0] // 2, d)
    conv8 = conv_all.transpose(2, 1, 0, 3).reshape(8, 3 * d_dn)
    win_full = _combine_windows(win_all, jnp.asarray(tab), n_tiles, "combine_w_in")
    n2 = _rms_fwd(h1, mix_norm, "rms2")
    proj, (wg2,) = _mm2d(n2, win_full, NN, F32, "in_proj", (1024, 768, 4096), comm=_all_gather_comm([p_g2]))
    nh_a = d_attn // HEAD_DIM
    attn, (wu2,) = _attn_fwd(proj, nh_a, s_, "attn_fwd", _all_gather_comm([p_u2]))
    zc0 = (3 * d_attn + 3 * d_dn) // HEAD_DIM
    bac = (3 * d_attn + 4 * d_dn) // LANE
    row128 = lambda v: jnp.pad(v, (0, LANE - v.shape[0])).reshape(1, LANE)
    prm = jnp.concatenate([row128(dn_norm), row128(a_log), row128(dt_bias), jnp.zeros((5, LANE), F32)], axis=0)
    u_dn = _conv_fwd(proj, conv8, 3 * d_attn, 3 * d_dn, s_, "dn_conv")
    dn_out, (wd2,) = _dn_fwd(u_dn, proj, prm, nh_d, s_, zc0, bac, "dn_fwd", _all_gather_comm([p_d2]))
    wg2, wu2, wd2 = as_cols(wg2), as_cols(wu2), as_rows(wd2)
    cat_b = jnp.concatenate([attn, dn_out], axis=1).astype(BF16)
    h2 = _mm_pieces_resid(cat_b, wout, h1, 1.0, cat_map, "out_proj")
    n3 = _rms_fwd(h2, ffn2_norm, "rms3")
    a3, b3, s3 = _ffn_up(n3, wg2, wu2, "ffn2_up")
    h3 = _mm_pieces_resid(s3, wd2, h2, 0.5, ident, "ffn2_down")

    def rs_front(gs, tag):
        gs = [g.reshape((2, N_CHIPS * g.shape[-2], g.shape[-1])) for g in gs]
        from_sib = _sibling_take(gs, f"rs_sibling_take_{tag}")
        ps = [_pair_sum(g, l, cidx, f"rs_pair_sum_{tag}_{i}") for i, (g, l) in enumerate(zip(gs, from_sib))]
        return [p.reshape(N_CHIPS, p.shape[0] // N_CHIPS, p.shape[1]) for p in ps]

    dh3, dh3b, dwf_p, lsq_p = _final_loss(h3, final_norm, loss_target.reshape(t, d), "final_loss")
    da3, db3 = _ffn_bwd_hidden(dh3b, wd2, a3, b3, "ffn2_bwd_hidden")
    g_wd2 = _grad_rows_pieces(s3, dh3b, 0.5, ident, npc, "ffn2_grad_down")
    p_d2s = rs_front([g_wd2], "d2")
    (g_wg2, g_wu2), l_d2 = _grad_cols_pieces(n3, da3, db3, npc, "ffn2_grad_up", comm=_chip_all_to_all_comm(p_d2s))
    p_gu2 = rs_front([g_wg2, g_wu2], "gu2")
    dn3, l_gu2 = _ffn_bwd_input(da3, db3, wg2, wu2, "ffn2_bwd_input", comm=_chip_all_to_all_comm(p_gu2))
    dh2, dh2b, dw3_p = _rms_bwd(dn3, h2, ffn2_norm, dh3, "rms3_bwd")

    dcat = _mm_nt_pieces_out(dh2b, wout, cat_map, "out_proj_bwd")
    g_wout = _grad_rows_pieces(cat_b, dh2b, 1.0, cat_map, npc, "out_proj_grad")
    dq_a, dk_a, dv_a = _attn_bwd(proj, attn, dcat, nh_a, s_, "attn_bwd")
    duq, duk, duv, dz, dba, dprm = _dn_bwd(u_dn, proj, prm, dcat, nh_d, nh_a, s_, zc0, bac, "dn_bwd")
    dx_conv, dconv8 = _conv_bwd(proj, conv8, jnp.concatenate([duq, duk, duv], axis=1), 3 * d_attn, 3 * d_dn, s_,
                                "dn_conv_bwd")
    used = 3 * d_attn + 4 * d_dn + LANE
    dproj_b = jnp.concatenate([dq_a, dk_a, dv_a, dx_conv, dz, dba.astype(BF16),
                               jnp.zeros((t, proj.shape[1] - used), BF16)], axis=1)
    dconv, ddnn, dalog, ddtb = dconv8[:CONV_WIDTH], dprm[0, :dn_norm.shape[0]], dprm[1, :nh_d], dprm[2, :nh_d]
    g_win_full = _mm2d(n2, dproj_b, TN, BF16, "in_proj_grad", (1024, 2432, 512))
    wh = ww // 2
    g_win = jnp.stack([jnp.stack([g_win_full[:, w0[k] + wh * h: w0[k] + wh * (h + 1)] for k in range(N_CHIPS)])
                       for h in range(2)])
    p_ow = rs_front([g_wout, g_win], "ow")
    dn2, l_ow = _mm2d(dproj_b, win_full, NT, F32, "in_proj_bwd", (1024, 2048, 768), comm=_chip_all_to_all_comm(p_ow))
    dh1, dh1b, dwm_p = _rms_bwd(dn2, h1, mix_norm, dh2, "rms2_bwd")

    da1, db1 = _ffn_bwd_hidden(dh1b, wd1, a1, b1, "ffn1_bwd_hidden")
    g_wd1 = _grad_rows_pieces(s1, dh1b, 0.5, ident, npc, "ffn1_grad_down")
    p_d1s = rs_front([g_wd1], "d1")
    (g_wg1, g_wu1), l_d1 = _grad_cols_pieces(n1, da1, db1, npc, "ffn1_grad_up", comm=_chip_all_to_all_comm(p_d1s))
    p_gu1 = rs_front([g_wg1, g_wu1], "gu1")
    dn1, l_gu1 = _ffn_bwd_input(da1, db1, wg1, wu1, "ffn1_bwd_input", comm=_chip_all_to_all_comm(p_gu1))
    dh0, _, dw1_p = _rms_bwd(dn1, h0, ffn1_norm, dh1, "rms1_bwd")
    grad_x = dh0.reshape(bl, s_, d)

    pair = [p_gu1[0], p_gu1[1], p_d1s[0], p_ow[1], p_ow[0], p_gu2[0], p_gu2[1], p_d2s[0]]
    from_chips = [l_gu1[0], l_gu1[1], l_d1[0], l_ow[1], l_ow[0], l_gu2[0], l_gu2[1], l_d2[0]]
    halves = [_chip_sum(p, l, mcidx, f"rs_chip_sum_{i}") for i, (p, l) in enumerate(zip(pair, from_chips))]
    full = _sibling_join(halves, "rs_sibling_join")
    f_wg1, f_wu1, f_wd1, f_win, f_wout, f_wg2, f_wu2, f_wd2 = full

    unpad_cols = lambda f: jnp.concatenate([f[0], f[1][:, :fs - hp]], axis=1)
    unpad_rows = lambda f: f.reshape(2 * f.shape[1], f.shape[2])[:fs]
    gw = {
        "ffn1_w_gate": unpad_cols(f_wg1), "ffn1_w_up": unpad_cols(f_wu1), "ffn1_w_down": unpad_rows(f_wd1),
        "w_in": lax.dynamic_slice(jnp.concatenate([f_win[0], f_win[1]], axis=1), (jnp.int32(0), shift), (d, ws)),
        "w_out": f_wout.reshape(w_out.shape),
        "ffn2_w_gate": unpad_cols(f_wg2), "ffn2_w_up": unpad_cols(f_wu2), "ffn2_w_down": unpad_rows(f_wd2),
    }

    def lanes(v):
        v = v.reshape(-1)
        return jnp.pad(v, (0, _ceil_to(v.shape[0], LANE) - v.shape[0])).reshape(-1, LANE)

    small = [dw1_p.sum(0), dwm_p.sum(0), dw3_p.sum(0), dwf_p.sum(0), ddnn, dalog, ddtb,
             (0.5 / d) * jnp.sum(lsq_p).reshape(1), dconv]
    rows = [lanes(v) for v in small]
    offs = np.cumsum([0] + [r.shape[0] for r in rows])
    packed = jnp.concatenate(rows, axis=0)
    packed = jnp.pad(packed, ((0, _ceil_to(packed.shape[0], 8) - packed.shape[0]), (0, 0)))
    red = _allreduce_small(packed, "allreduce_small")
    take = lambda i, shape: red[offs[i]:offs[i + 1]].reshape(-1)[:int(np.prod(shape))].reshape(shape)
    gw["ffn1_norm"] = take(0, (d,))
    gw["mix_norm"] = take(1, (d,))
    gw["ffn2_norm"] = take(2, (d,))
    gw["final_norm"] = take(3, (d,))
    gw["dn_norm"] = take(4, dn_norm.shape)
    gw["a_log"] = take(5, a_log.shape)
    gw["dt_bias"] = take(6, dt_bias.shape)
    loss = take(7, (1,)).reshape(())
    gw["conv_w"] = lax.dynamic_slice(take(8, (CONV_WIDTH, 3 * d_dn)), (jnp.int32(0), me * cs), (CONV_WIDTH, cs))

    names = ['ffn1_norm', 'ffn1_w_gate', 'ffn1_w_up', 'ffn1_w_down', 'mix_norm', 'w_in', 'conv_w', 'a_log', 'dt_bias',
             'dn_norm', 'w_out', 'ffn2_norm', 'ffn2_w_gate', 'ffn2_w_up', 'ffn2_w_down', 'final_norm']
    wv = dict(zip(names, (ffn1_norm, ffn1_w_gate, ffn1_w_up, ffn1_w_down, mix_norm, w_in, conv_w, a_log, dt_bias,
                          dn_norm, w_out, ffn2_norm, ffn2_w_gate, ffn2_w_up, ffn2_w_down, final_norm)))
    mv = dict(zip(names, (m_ffn1_norm, m_ffn1_w_gate, m_ffn1_w_up, m_ffn1_w_down, m_mix_norm, m_w_in, m_conv_w, m_a_log,
                          m_dt_bias, m_dn_norm, m_w_out, m_ffn2_norm, m_ffn2_w_gate, m_ffn2_w_up, m_ffn2_w_down,
                          m_final_norm)))
    vv = dict(zip(names, (v_ffn1_norm, v_ffn1_w_gate, v_ffn1_w_up, v_ffn1_w_down, v_mix_norm, v_w_in, v_conv_w, v_a_log,
                          v_dt_bias, v_dn_norm, v_w_out, v_ffn2_norm, v_ffn2_w_gate, v_ffn2_w_up, v_ffn2_w_down,
                          v_final_norm)))
    delta, new_m, new_v = {}, {}, {}
    small_names = [n for n in names if wv[n].ndim == 1 or n == "conv_w"]
    transposed = ("ffn1_w_gate", "ffn1_w_up", "ffn2_w_gate", "ffn2_w_up")
    for n in names:
        if n in small_names:
            continue
        if n in transposed:
            res = _adamw(gw[n].T, wv[n].T, mv[n].T, vv[n].T, f"adamw_{n}")
            delta[n], new_m[n], new_v[n] = (r.T for r in res)
        else:
            delta[n], new_m[n], new_v[n] = _adamw(gw[n], wv[n], mv[n], vv[n], f"adamw_{n}")
    srows = {n: lanes(gw[n]).shape[0] for n in small_names}
    soffs = np.cumsum([0] + [srows[n] for n in small_names])
    stot = _ceil_to(int(soffs[-1]), 8)

    def pack(dct):
        p = jnp.concatenate([lanes(dct[n]) for n in small_names], axis=0)
        return jnp.pad(p, ((0, stot - p.shape[0]), (0, 0)))

    sd, sm, sv = _adamw(pack(gw), pack(wv), pack(mv), pack(vv), "adamw_small")
    for i, n in enumerate(small_names):
        cut = lambda p: p[soffs[i]:soffs[i + 1]].reshape(-1)[:wv[n].size].reshape(wv[n].shape)
        delta[n], new_m[n], new_v[n] = cut(sd), cut(sm), cut(sv)

    return (loss, grad_x, *[gw[n] for n in names], *[delta[n] for n in names], *[new_m[n] for n in names],
            *[new_v[n] for n in names])
```

```python
import functools
import math

import jax
import jax.numpy as jnp
import numpy as np
from jax import lax
from jax.experimental import pallas as pl
from jax.experimental.pallas import tpu as pltpu

F32 = jnp.float32
BF16 = jnp.bfloat16
MESH = pl.DeviceIdType.MESH
ANY = pl.BlockSpec(memory_space=pl.ANY)

LANE = 128
N_CHIPS = 4
N_DEV = 8
EPS = 1e-6
HEAD_DIM = 128
CONV_WIDTH = 4
CHUNK = 64
ATTN_BLOCK = 128
DILATED_CONFIGS = ((128, 1), (512, 4), (2048, 16))
VMEM_LIMIT = 52 * 1024 * 1024

ADAM_LR = 0.001
ADAM_B1 = 0.9
ADAM_B2 = 0.999
ADAM_EPS = 1e-08
ADAM_WD = 0.01
ADAM_STEP = 10

NN = (((1,), (0,)), ((), ()))
NT = (((1,), (1,)), ((), ()))
TN = (((0,), (0,)), ((), ()))


def _ceil_to(v, m):
    return -(-v // m) * m


def _params(vmem=VMEM_LIMIT):
    return pltpu.CompilerParams(vmem_limit_bytes=vmem)


class _Comm:
    def __init__(self, ins, out_shape, aliases, sems, start, mid, finish, mid_frac=0.75):
        self.ins, self.out_shape, self.aliases, self.sems = ins, out_shape, aliases, sems
        self.start, self.mid, self.finish, self.mid_frac = start, mid, finish, mid_frac

    def mid_point(self, grid):
        steps = grid[0] * grid[1]
        idx = min(int(self.mid_frac * steps), steps - 1)
        return [idx // grid[1], idx % grid[1]] + [0] * (len(grid) - 2)


def _gemm(name, grid, pairs, dn, acc_shape, n_acc, extras, outs, epilogue, comm=None, epi_rows=None):
    n_pairs, n_ex, n_out = len(pairs), len(extras), len(outs)
    n_ci = len(comm.ins) if comm else 0
    n_co = len(comm.out_shape) if comm else 0
    n_sem = len(comm.sems) if comm else 0
    kax = len(grid) - 1
    nk = grid[kax]
    n_in = 2 * n_pairs + n_ex

    def body(*refs):
        ins = refs[: 2 * n_pairs]
        ex = refs[2 * n_pairs: n_in]
        c_in = refs[n_in: n_in + n_ci]
        out_refs = refs[n_in + n_ci: n_in + n_ci + n_out]
        c_out = refs[n_in + n_ci + n_out: n_in + n_ci + n_out + n_co]
        accs = refs[n_in + n_ci + n_out + n_co: n_in + n_ci + n_out + n_co + n_acc]
        sems = refs[n_in + n_ci + n_out + n_co + n_acc:]
        k = pl.program_id(kax)
        pids = [pl.program_id(a) for a in range(len(grid))]

        def at(point):
            cond = pids[0] == point[0]
            for pid, v in zip(pids[1:], point[1:]):
                cond = cond & (pid == v)
            return cond

        if comm:
            @pl.when(at([0] * len(grid)))
            def _():
                comm.start(c_in, c_out, sems)

            if comm.mid:
                @pl.when(at(comm.mid_point(grid)))
                def _():
                    comm.mid(c_in, c_out, sems)

        @pl.when(k == 0)
        def _():
            for acc in accs:
                acc[...] = jnp.zeros(acc.shape, F32)

        for q in range(n_pairs):
            a = ins[2 * q][...]
            b = ins[2 * q + 1][...]
            if a.dtype != BF16:
                a = a.astype(BF16)
            if b.dtype != BF16:
                b = b.astype(BF16)
            accs[pairs[q][4]][...] += lax.dot_general(a, b, dn, preferred_element_type=F32)

        @pl.when(k == nk - 1)
        def _():
            rows = acc_shape[0]
            step = epi_rows if epi_rows and rows % epi_rows == 0 else rows
            for r0 in range(0, rows, step):
                sl = pl.ds(r0, step)
                res = epilogue([acc[sl, :] for acc in accs], [e[sl, :] for e in ex])
                for o, r in zip(out_refs, res):
                    o[sl, :] = r.astype(o.dtype)

        if comm:
            @pl.when(at([g - 1 for g in grid]))
            def _():
                comm.finish(c_in, c_out, sems)

    in_specs = []
    args = []
    for a, a_spec, b, b_spec, _ in pairs:
        in_specs += [a_spec, b_spec]
        args += [a, b]
    for e, e_spec in extras:
        in_specs.append(e_spec)
        args.append(e)
    out_shape = [o for o, _ in outs]
    out_specs = [s for _, s in outs]
    scratch = [pltpu.VMEM(acc_shape, F32) for _ in range(n_acc)]
    kwargs = {}
    if comm:
        in_specs += [ANY] * n_ci
        args += list(comm.ins)
        out_shape += list(comm.out_shape)
        out_specs += [ANY] * n_co
        scratch += list(comm.sems)
        kwargs["input_output_aliases"] = {n_in + i: n_out + o for i, o in comm.aliases.items()}
    res = pl.pallas_call(body, name=name, grid=grid, in_specs=in_specs, out_specs=out_specs,
                         out_shape=out_shape, scratch_shapes=scratch, compiler_params=_params(), **kwargs)(*args)
    if comm:
        return list(res[:n_out]), list(res[n_out:])
    return res


def _unpack(res, comm, single):
    if comm:
        outs, couts = res
        return (outs[0] if single else outs), couts
    return res[0] if single else res


def _call_carrying(body, name, grid, in_specs, out_specs, out_shape, scratch, args, comm):
    n_in, n_out, n_scr = len(in_specs), len(out_shape), len(scratch)
    n_ci, n_co = len(comm.ins), len(comm.out_shape)

    def wrapped(*refs):
        ins, c_in = refs[:n_in], refs[n_in:n_in + n_ci]
        outs = refs[n_in + n_ci:n_in + n_ci + n_out]
        c_out = refs[n_in + n_ci + n_out:n_in + n_ci + n_out + n_co]
        scr = refs[n_in + n_ci + n_out + n_co:n_in + n_ci + n_out + n_co + n_scr]
        sems = refs[n_in + n_ci + n_out + n_co + n_scr:]
        pids = [pl.program_id(a) for a in range(len(grid))]

        def at(point):
            cond = pids[0] == point[0]
            for pid, v in zip(pids[1:], point[1:]):
                cond = cond & (pid == v)
            return cond

        @pl.when(at([0] * len(grid)))
        def _():
            comm.start(c_in, c_out, sems)

        if comm.mid:
            @pl.when(at(comm.mid_point(grid)))
            def _():
                comm.mid(c_in, c_out, sems)

        body(*ins, *outs, *scr)

        @pl.when(at([g - 1 for g in grid]))
        def _():
            comm.finish(c_in, c_out, sems)

    res = pl.pallas_call(
        wrapped, name=name, grid=grid, in_specs=list(in_specs) + [ANY] * n_ci, out_specs=list(out_specs) + [ANY] * n_co,
        out_shape=list(out_shape) + list(comm.out_shape), scratch_shapes=list(scratch) + list(comm.sems),
        input_output_aliases={n_in + i: n_out + o for i, o in comm.aliases.items()},
        compiler_params=_params())(*args, *comm.ins)
    return list(res[:n_out]), list(res[n_out:])


def _pick(n, prefs):
    for p in prefs:
        if n % p == 0:
            return p
    return n


def _sigmoid(v):
    return 1.0 / (1.0 + jnp.exp(-v))


def _ffn_up(n, wg, wu, name, comm=None):
    t, d = n.shape
    npieces, _, hp = wg.shape
    tm = _pick(t, (1024, 512, 256, 128, 64, 32, 16))
    tk = _pick(d, (512, 256, 128))
    grid = (t // tm, npieces, d // tk)
    a_spec = pl.BlockSpec((tm, tk), lambda i, p, k: (i, k))
    w_spec = pl.BlockSpec((None, tk, hp), lambda i, p, k: (p, k, 0))
    o_spec = pl.BlockSpec((tm, hp), lambda i, p, k: (i, p))
    osd = jax.ShapeDtypeStruct((t, npieces * hp), BF16)

    def epi(accs, ex):
        a, b = accs
        return a, b, a * _sigmoid(a) * b

    return _unpack(_gemm(name, grid, [(n, a_spec, wg, w_spec, 0), (n, a_spec, wu, w_spec, 1)], NN, (tm, hp), 2, [],
                         [(osd, o_spec)] * 3, epi, comm, epi_rows=256), comm, False)


def _mm_pieces_resid(a, w, resid, scale, amap, name, comm=None):
    t = a.shape[0]
    npieces, kp, n = w.shape
    tm = _pick(t, (1024, 512, 256, 128, 64, 32, 16))
    tn = _pick(n, (1024, 512, 256, 128))
    grid = (t // tm, n // tn, npieces)
    a_spec = pl.BlockSpec((tm, kp), lambda i, j, p: (i, amap(p)))
    w_spec = pl.BlockSpec((None, kp, tn), lambda i, j, p: (p, 0, j))
    r_spec = pl.BlockSpec((tm, tn), lambda i, j, p: (i, j))

    def epi(accs, ex):
        return (ex[0] + scale * accs[0],)

    return _unpack(_gemm(name, grid, [(a, a_spec, w, w_spec, 0)], NN, (tm, tn), 1, [(resid, r_spec)],
                         [(jax.ShapeDtypeStruct((t, n), F32), r_spec)], epi, comm), comm, True)


def _ffn_bwd_hidden(dh, wd, a, b, name):
    t, d = dh.shape
    npieces, hp, _ = wd.shape
    tm = _pick(t, (1024, 512, 256, 128, 64, 32, 16))
    tk = _pick(d, (512, 256, 128))
    grid = (t // tm, npieces, d // tk)
    a_spec = pl.BlockSpec((tm, tk), lambda i, p, k: (i, k))
    w_spec = pl.BlockSpec((None, hp, tk), lambda i, p, k: (p, 0, k))
    o_spec = pl.BlockSpec((tm, hp), lambda i, p, k: (i, p))
    osd = jax.ShapeDtypeStruct((t, npieces * hp), BF16)

    def epi(accs, ex):
        ds = 0.5 * accs[0]
        av = ex[0].astype(F32)
        bv = ex[1].astype(F32)
        sg = _sigmoid(av)
        da = ds * bv * (sg * (1.0 + av * (1.0 - sg)))
        db = ds * (av * sg)
        return da, db

    return _gemm(name, grid, [(dh, a_spec, wd, w_spec, 0)], NT, (tm, hp), 1, [(a, o_spec), (b, o_spec)],
                 [(osd, o_spec)] * 2, epi, epi_rows=256)


def _mm_nt_pieces_out(dh, w, omap, name):
    t, d = dh.shape
    npieces, npp, _ = w.shape
    tm = _pick(t, (1024, 512, 256, 128, 64, 32, 16))
    tk = _pick(d, (1024, 512, 256, 128))
    grid = (t // tm, npieces, d // tk)
    a_spec = pl.BlockSpec((tm, tk), lambda i, p, k: (i, k))
    w_spec = pl.BlockSpec((None, npp, tk), lambda i, p, k: (p, 0, k))
    o_spec = pl.BlockSpec((tm, npp), lambda i, p, k: (i, omap(p)))
    return _gemm(name, grid, [(dh, a_spec, w, w_spec, 0)], NT, (tm, npp), 1, [],
                 [(jax.ShapeDtypeStruct((t, npieces * npp), BF16), o_spec)], lambda accs, ex: (accs[0],))[0]


def _grad_rows_pieces(x, dy, scale, amap, npieces, name):
    t, n = dy.shape
    mp = x.shape[1] // npieces
    tn = _pick(n, (1024, 512, 256, 128))
    tk = _pick(t, (1024, 512, 256, 128, 64, 32, 16))
    grid = (npieces, n // tn, t // tk)
    x_spec = pl.BlockSpec((tk, mp), lambda p, j, k: (k, amap(p)))
    y_spec = pl.BlockSpec((tk, tn), lambda p, j, k: (k, j))
    o_spec = pl.BlockSpec((None, mp, tn), lambda p, j, k: (p, 0, j))
    return _gemm(name, grid, [(x, x_spec, dy, y_spec, 0)], TN, (mp, tn), 1, [],
                 [(jax.ShapeDtypeStruct((npieces, mp, n), BF16), o_spec)], lambda accs, ex: (scale * accs[0],))[0]


def _grad_cols_pieces(n, da, db, npieces, name, comm=None):
    t, d = n.shape
    hp = da.shape[1] // npieces
    tm = _pick(d, (1024, 512, 256, 128))
    tk = _pick(t, (512, 256, 128, 64, 32, 16))
    grid = (npieces, d // tm, t // tk)
    n_spec = pl.BlockSpec((tk, tm), lambda p, i, k: (k, i))
    g_spec = pl.BlockSpec((tk, hp), lambda p, i, k: (k, p))
    o_spec = pl.BlockSpec((None, tm, hp), lambda p, i, k: (p, i, 0))
    osd = jax.ShapeDtypeStruct((npieces, d, hp), BF16)
    return _unpack(_gemm(name, grid, [(n, n_spec, da, g_spec, 0), (n, n_spec, db, g_spec, 1)], TN, (tm, hp), 2, [],
                         [(osd, o_spec)] * 2, lambda accs, ex: (accs[0], accs[1]), comm), comm, False)


def _ffn_bwd_input(da, db, wg, wu, name, comm=None):
    t = da.shape[0]
    npieces, d, hp = wg.shape
    tm = _pick(t, (1024, 512, 256, 128, 64, 32, 16))
    tn = _pick(d, (1024, 512, 256, 128))
    grid = (t // tm, d // tn, npieces)
    g_spec = pl.BlockSpec((tm, hp), lambda i, j, p: (i, p))
    w_spec = pl.BlockSpec((None, tn, hp), lambda i, j, p: (p, j, 0))
    o_spec = pl.BlockSpec((tm, tn), lambda i, j, p: (i, j))
    return _unpack(_gemm(name, grid, [(da, g_spec, wg, w_spec, 0), (db, g_spec, wu, w_spec, 0)], NT, (tm, tn), 1, [],
                         [(jax.ShapeDtypeStruct((t, d), F32), o_spec)], lambda accs, ex: (accs[0],), comm), comm, True)


def _mm2d(a, b, dn, out_dtype, name, tiles, comm=None):
    if dn == NN:
        m, kk = a.shape
        n = b.shape[1]
    elif dn == NT:
        m, kk = a.shape
        n = b.shape[0]
    else:
        kk, m = a.shape
        n = b.shape[1]
    tm = _pick(m, (tiles[0],) + (1024, 512, 256, 128, 64, 32, 16))
    tn = _pick(n, (tiles[1], 768, 1024, 512, 256, 128))
    tk = _pick(kk, (tiles[2], 768, 1024, 512, 256, 128, 64, 32, 16))
    grid = (m // tm, n // tn, kk // tk)
    if dn == TN:
        a_spec = pl.BlockSpec((tk, tm), lambda i, j, k: (k, i))
    else:
        a_spec = pl.BlockSpec((tm, tk), lambda i, j, k: (i, k))
    if dn == NT:
        b_spec = pl.BlockSpec((tn, tk), lambda i, j, k: (j, k))
    else:
        b_spec = pl.BlockSpec((tk, tn), lambda i, j, k: (k, j))
    o_spec = pl.BlockSpec((tm, tn), lambda i, j, k: (i, j))
    return _unpack(_gemm(name, grid, [(a, a_spec, b, b_spec, 0)], dn, (tm, tn), 1, [],
                         [(jax.ShapeDtypeStruct((m, n), out_dtype), o_spec)], lambda accs, ex: (accs[0],), comm), comm, True)


def _row_tile(t):
    return _pick(t, (256, 128, 64, 32, 16, 8))


def _rms_fwd(x, w, name):
    t, d = x.shape
    tm = _row_tile(t)

    def body(x_ref, w_ref, o_ref):
        xv = x_ref[...]
        r = lax.rsqrt(jnp.mean(xv * xv, axis=-1, keepdims=True) + EPS)
        o_ref[...] = (xv * r * w_ref[...]).astype(BF16)

    return pl.pallas_call(
        body, name=name, grid=(t // tm,),
        in_specs=[pl.BlockSpec((tm, d), lambda i: (i, 0)), pl.BlockSpec((1, d), lambda i: (0, 0))],
        out_specs=pl.BlockSpec((tm, d), lambda i: (i, 0)),
        out_shape=jax.ShapeDtypeStruct((t, d), BF16), compiler_params=_params())(x, w.reshape(1, d))


def _rms_bwd(dn, x, w, dres, name):
    t, d = x.shape
    tm = _row_tile(t)

    def body(dn_ref, x_ref, w_ref, r_ref, o_ref, ob_ref, dw_ref):
        i = pl.program_id(0)
        xv = x_ref[...]
        r = lax.rsqrt(jnp.mean(xv * xv, axis=-1, keepdims=True) + EPS)
        xh = xv * r
        dy = dn_ref[...].astype(F32)
        g = dy * w_ref[...]
        dx = r * (g - xh * jnp.mean(g * xh, axis=-1, keepdims=True))
        tot = r_ref[...] + dx
        o_ref[...] = tot
        ob_ref[...] = tot.astype(BF16)
        part = (dy * xh).reshape(tm // 8, 8, d).sum(axis=0)

        @pl.when(i == 0)
        def _():
            dw_ref[...] = part

        @pl.when(i > 0)
        def _():
            dw_ref[...] += part

    row = pl.BlockSpec((tm, d), lambda i: (i, 0))
    return pl.pallas_call(
        body, name=name, grid=(t // tm,),
        in_specs=[row, row, pl.BlockSpec((1, d), lambda i: (0, 0)), row],
        out_specs=[row, row, pl.BlockSpec((8, d), lambda i: (0, 0))],
        out_shape=[jax.ShapeDtypeStruct((t, d), F32), jax.ShapeDtypeStruct((t, d), BF16),
                   jax.ShapeDtypeStruct((8, d), F32)],
        compiler_params=_params())(dn, x, w.reshape(1, d), dres)


def _final_loss(h, w, target, name):
    t, d = h.shape
    tm = _row_tile(t)

    def body(h_ref, w_ref, t_ref, o_ref, ob_ref, dw_ref, ls_ref):
        i = pl.program_id(0)
        xv = h_ref[...]
        r = lax.rsqrt(jnp.mean(xv * xv, axis=-1, keepdims=True) + EPS)
        xh = xv * r
        err = xh * w_ref[...] - t_ref[...]
        dy = err * (1.0 / d)
        g = dy * w_ref[...]
        dx = r * (g - xh * jnp.mean(g * xh, axis=-1, keepdims=True))
        o_ref[...] = dx
        ob_ref[...] = dx.astype(BF16)
        part = (dy * xh).reshape(tm // 8, 8, d).sum(axis=0)
        lpart = (err * err).reshape(tm // 8, 8, d).sum(axis=0)

        @pl.when(i == 0)
        def _():
            dw_ref[...] = part
            ls_ref[...] = lpart

        @pl.when(i > 0)
        def _():
            dw_ref[...] += part
            ls_ref[...] += lpart

    row = pl.BlockSpec((tm, d), lambda i: (i, 0))
    acc = pl.BlockSpec((8, d), lambda i: (0, 0))
    return pl.pallas_call(
        body, name=name, grid=(t // tm,),
        in_specs=[row, pl.BlockSpec((1, d), lambda i: (0, 0)), row],
        out_specs=[row, row, acc, acc],
        out_shape=[jax.ShapeDtypeStruct((t, d), F32), jax.ShapeDtypeStruct((t, d), BF16),
                   jax.ShapeDtypeStruct((8, d), F32), jax.ShapeDtypeStruct((8, d), F32)],
        compiler_params=_params())(h, w.reshape(1, d), target)


def _cast_split_cols(w, hp, me, name):
    r, fs = w.shape
    v1 = fs - hp
    tm = _pick(r, (256, 128, 64, 32, 16))

    def body(me_ref, w_ref, o_ref):
        o_ref[0] = w_ref[:, :hp].astype(BF16)
        if v1 < hp:
            o_ref[1] = jnp.zeros((tm, hp), BF16)
        o_ref[1, :, :v1] = w_ref[:, hp:].astype(BF16)

    gs = pltpu.PrefetchScalarGridSpec(
        num_scalar_prefetch=1, grid=(r // tm,),
        in_specs=[pl.BlockSpec((tm, fs), lambda i, mr: (i, 0))],
        out_specs=pl.BlockSpec((2, None, tm, hp), lambda i, mr: (0, mr[0], i, 0)))
    return pl.pallas_call(body, name=name, grid_spec=gs, out_shape=jax.ShapeDtypeStruct((2, N_CHIPS, r, hp), BF16),
                          compiler_params=_params())(me, w)


def _cast_split_cols_t(wt, hp, me, name):
    fs, r = wt.shape
    v1 = fs - hp
    tm = _pick(r, (256, 128))

    def body(me_ref, w_ref, o_ref):
        o_ref[0] = w_ref[:hp, :].T.astype(BF16)
        if v1 < hp:
            o_ref[1] = jnp.zeros((tm, hp), BF16)
        o_ref[1, :, :v1] = w_ref[hp:, :].T.astype(BF16)

    gs = pltpu.PrefetchScalarGridSpec(
        num_scalar_prefetch=1, grid=(r // tm,),
        in_specs=[pl.BlockSpec((fs, tm), lambda i, mr: (0, i))],
        out_specs=pl.BlockSpec((2, None, tm, hp), lambda i, mr: (0, mr[0], i, 0)))
    return pl.pallas_call(body, name=name, grid_spec=gs, out_shape=jax.ShapeDtypeStruct((2, N_CHIPS, r, hp), BF16),
                          compiler_params=_params())(me, wt)


def _cast_split_rows(w, hp, tr, me, name):
    fs, c = w.shape
    nvalid = fs // tr
    per = hp // tr

    def body(me_ref, w_ref, o_ref):
        i = pl.program_id(0)

        @pl.when(i < nvalid)
        def _():
            o_ref[...] = w_ref[...].astype(BF16)

        @pl.when(i >= nvalid)
        def _():
            o_ref[...] = jnp.zeros(o_ref.shape, BF16)

    gs = pltpu.PrefetchScalarGridSpec(
        num_scalar_prefetch=1, grid=(2 * per,),
        in_specs=[pl.BlockSpec((tr, c), lambda i, mr: (jnp.minimum(i, nvalid - 1), 0))],
        out_specs=pl.BlockSpec((None, None, tr, c), lambda i, mr: (i // per, mr[0], i % per, 0)))
    return pl.pallas_call(body, name=name, grid_spec=gs, out_shape=jax.ShapeDtypeStruct((2, N_CHIPS, hp, c), BF16),
                          compiler_params=_params())(me, w)


def _combine_windows(wall, tables, n_tiles, name):
    _, _, d, wh = wall.shape
    tpw = wh // LANE

    def body(tab_ref, a_ref, b_ref, o_ref):
        t = pl.program_id(0)
        both = tab_ref[6, t] == 1
        av = a_ref[...]
        bv = b_ref[...]
        o_ref[...] = jnp.where(both, av + bv, av)

    def amap(t, tab):
        return (tab[0, t], tab[1, t], 0, tab[2, t])

    def bmap(t, tab):
        return (tab[3, t], tab[4, t], 0, tab[5, t])

    gs = pltpu.PrefetchScalarGridSpec(
        num_scalar_prefetch=1, grid=(n_tiles,),
        in_specs=[pl.BlockSpec((None, None, d, LANE), amap), pl.BlockSpec((None, None, d, LANE), bmap)],
        out_specs=pl.BlockSpec((d, LANE), lambda t, tab: (0, t)))
    del tpw
    return pl.pallas_call(body, name=name, grid_spec=gs, out_shape=jax.ShapeDtypeStruct((d, n_tiles * LANE), BF16),
                          compiler_params=_params())(tables, wall, wall)


def _coords():
    return lax.axis_index("x"), lax.axis_index("y"), lax.axis_index("c")


def _remote(src, dst, ssem, rsem, dev):
    return pltpu.make_async_remote_copy(src_ref=src, dst_ref=dst, send_sem=ssem, recv_sem=rsem, device_id=dev,
                                        device_id_type=MESH)


def _mesh_places():
    x, y, c = _coords()
    return c, 2 * x + y, (x, y, 1 - c), [(1 - x, y), (x, 1 - y), (1 - x, 1 - y)]


def _all_gather_comm(bufs, mid_frac=0.75):
    n = len(bufs)

    def start(ins, outs, sems):
        c, me, _, chips = _mesh_places()
        for i in range(n):
            for j, (px, py) in enumerate(chips):
                mine = outs[i].at[c, me]
                _remote(mine, mine, sems[0].at[i, j], sems[1].at[i, j], (px, py, c)).start()

    def mid(ins, outs, sems):
        c, _, sib, chips = _mesh_places()
        for i in range(n):
            for j, (px, py) in enumerate(chips):
                slot = outs[i].at[c, 2 * px + py]
                _remote(slot, slot, sems[0].at[i, j], sems[1].at[i, j], (px, py, c)).wait_recv()
                _remote(slot, slot, sems[2].at[i, j], sems[3].at[i, j], sib).start()

    def finish(ins, outs, sems):
        c, me, sib, chips = _mesh_places()
        for i in range(n):
            for j, (px, py) in enumerate(chips):
                slot = outs[i].at[1 - c, 2 * px + py]
                _remote(slot, slot, sems[2].at[i, j], sems[3].at[i, j], sib).wait_recv()
        for i in range(n):
            for j, (px, py) in enumerate(chips):
                mine = outs[i].at[c, me]
                _remote(mine, mine, sems[0].at[i, j], sems[1].at[i, j], (px, py, c)).wait_send()
                slot = outs[i].at[c, 2 * px + py]
                _remote(slot, slot, sems[2].at[i, j], sems[3].at[i, j], sib).wait_send()

    return _Comm(list(bufs), [jax.ShapeDtypeStruct(b.shape, b.dtype) for b in bufs], {i: i for i in range(n)},
                 [pltpu.SemaphoreType.DMA((n, 3))] * 4, start, mid, finish, mid_frac)


def _chip_all_to_all_comm(ps):
    n = len(ps)

    def start(ins, outs, sems):
        c, me, _, chips = _mesh_places()
        for i in range(n):
            for j, (px, py) in enumerate(chips):
                _remote(ins[i].at[2 * px + py], outs[i].at[me], sems[0].at[i, j], sems[1].at[i, j], (px, py, c)).start()

    def finish(ins, outs, sems):
        c, me, _, chips = _mesh_places()
        for i in range(n):
            for j, (px, py) in enumerate(chips):
                slot = outs[i].at[2 * px + py]
                _remote(slot, slot, sems[0].at[i, j], sems[1].at[i, j], (px, py, c)).wait_recv()
        for i in range(n):
            for j, (px, py) in enumerate(chips):
                _remote(ins[i].at[2 * px + py], outs[i].at[me], sems[0].at[i, j], sems[1].at[i, j],
                        (px, py, c)).wait_send()

    return _Comm(list(ps), [jax.ShapeDtypeStruct(p.shape, p.dtype) for p in ps], {},
                 [pltpu.SemaphoreType.DMA((n, 3))] * 2, start, None, finish)


def _comm_call(comm, name):
    n_in, n_out = len(comm.ins), len(comm.out_shape)

    def body(*refs):
        ins, outs, sems = refs[:n_in], refs[n_in:n_in + n_out], refs[n_in + n_out:]
        comm.start(ins, outs, sems)
        if comm.mid:
            comm.mid(ins, outs, sems)
        comm.finish(ins, outs, sems)

    return pl.pallas_call(body, name=name, in_specs=[ANY] * n_in, out_specs=[ANY] * n_out, out_shape=comm.out_shape,
                          input_output_aliases=dict(comm.aliases), scratch_shapes=list(comm.sems))(*comm.ins)


def _sibling_take(gs, name):
    n = len(gs)

    def body(*refs):
        g, out = refs[:n], refs[n:2 * n]
        ssem, rsem = refs[2 * n:]
        x, y, c = _coords()
        sib = (x, y, 1 - c)
        cps = []
        for i in range(n):
            cp = _remote(g[i].at[1 - c], out[i], ssem.at[i], rsem.at[i], sib)
            cp.start()
            cps.append(cp)
        for cp in cps:
            cp.wait()

    out_shape = [jax.ShapeDtypeStruct(s.shape[1:], s.dtype) for s in gs]
    return pl.pallas_call(
        body, name=name, in_specs=[ANY] * n, out_specs=[ANY] * n, out_shape=out_shape,
        scratch_shapes=[pltpu.SemaphoreType.DMA((n,)), pltpu.SemaphoreType.DMA((n,))])(*gs)


def _sibling_join(bufs, name):
    n = len(bufs)

    def body(*refs):
        out = refs[n:2 * n]
        ssem, rsem = refs[2 * n:]
        x, y, c = _coords()
        sib = (x, y, 1 - c)
        cps = []
        for i in range(n):
            mine = out[i].at[c]
            cp = _remote(mine, mine, ssem.at[i], rsem.at[i], sib)
            cp.start()
            cps.append(cp)
        for i in range(n):
            slot = out[i].at[1 - c]
            _remote(slot, slot, ssem.at[i], rsem.at[i], sib).wait_recv()
        for cp in cps:
            cp.wait_send()

    out_shape = [jax.ShapeDtypeStruct(b.shape, b.dtype) for b in bufs]
    return pl.pallas_call(
        body, name=name, in_specs=[ANY] * n, out_specs=[ANY] * n, out_shape=out_shape,
        input_output_aliases={i: i for i in range(n)},
        scratch_shapes=[pltpu.SemaphoreType.DMA((n,)), pltpu.SemaphoreType.DMA((n,))])(*bufs)


def _allreduce_small(vec, name):
    r = vec.shape[0]

    def body(v_ref, o_ref, buf, ssem, rsem):
        x, y, c = _coords()
        my = 4 * x + 2 * y + c
        buf[my] = v_ref[...]
        cps = []
        for dd in range(1, N_DEV):
            px = 1 - x if (dd >> 2) & 1 else x
            py = 1 - y if (dd >> 1) & 1 else y
            pc = 1 - c if dd & 1 else c
            cp = _remote(v_ref, buf.at[my], ssem.at[dd - 1], rsem.at[dd - 1], (px, py, pc))
            cp.start()
            cps.append(cp)
        for dd in range(1, N_DEV):
            px = 1 - x if (dd >> 2) & 1 else x
            py = 1 - y if (dd >> 1) & 1 else y
            pc = 1 - c if dd & 1 else c
            slot = buf.at[4 * px + 2 * py + pc]
            _remote(slot, slot, ssem.at[dd - 1], rsem.at[dd - 1], (px, py, pc)).wait_recv()
        tot = buf[0]
        for k in range(1, N_DEV):
            tot = tot + buf[k]
        o_ref[...] = tot
        for cp in cps:
            cp.wait_send()

    vm = pl.BlockSpec(memory_space=pltpu.VMEM)
    return pl.pallas_call(
        body, name=name, in_specs=[vm], out_specs=vm, out_shape=jax.ShapeDtypeStruct((r, LANE), F32),
        scratch_shapes=[pltpu.VMEM((N_DEV, r, LANE), F32), pltpu.SemaphoreType.DMA((N_DEV - 1,)),
                        pltpu.SemaphoreType.DMA((N_DEV - 1,))])(vec)


def _pair_sum(g, l1, cidx, name):
    _, r, c = g.shape
    tr = _pick(r, (512, 256, 128, 64, 32, 16))

    def body(c_ref, g_ref, l_ref, o_ref):
        o_ref[...] = (g_ref[...].astype(F32) + l_ref[...].astype(F32)).astype(BF16)

    gs = pltpu.PrefetchScalarGridSpec(
        num_scalar_prefetch=1, grid=(r // tr,),
        in_specs=[pl.BlockSpec((None, tr, c), lambda i, cr: (cr[0], i, 0)), pl.BlockSpec((tr, c), lambda i, cr: (i, 0))],
        out_specs=pl.BlockSpec((tr, c), lambda i, cr: (i, 0)))
    return pl.pallas_call(body, name=name, grid_spec=gs, out_shape=jax.ShapeDtypeStruct((r, c), BF16),
                          compiler_params=_params())(cidx, g, l1)


def _chip_sum(p, l2, mc, name):
    _, r, c = l2.shape
    tr = _pick(r, (256, 128, 64, 32, 16))

    def body(mc_ref, p_ref, l0, l1, l2_, l3, o_ref):
        me = mc_ref[0]
        pv = p_ref[...].astype(F32)
        tot = None
        for k, lr in enumerate((l0, l1, l2_, l3)):
            term = jnp.where(me == k, pv, lr[...].astype(F32))
            tot = term if tot is None else tot + term
        o_ref[...] = tot

    def other(k):
        return lambda i, mr: (jnp.where(mr[0] == k, (k + 1) % N_CHIPS, k), i, 0)

    gs = pltpu.PrefetchScalarGridSpec(
        num_scalar_prefetch=1, grid=(r // tr,),
        in_specs=[pl.BlockSpec((None, tr, c), lambda i, mr: (mr[0], i, 0))]
        + [pl.BlockSpec((None, tr, c), other(k)) for k in range(N_CHIPS)],
        out_specs=pl.BlockSpec((None, tr, c), lambda i, mr: (mr[1], i, 0)))
    return pl.pallas_call(body, name=name, grid_spec=gs, out_shape=jax.ShapeDtypeStruct((2, r, c), F32),
                          compiler_params=_params())(mc, p, l2, l2, l2, l2)


def _adamw(g, w, m, v, name):
    r, c = w.shape
    tr = r
    if r * c * 4 > (2 << 20):
        tr = next(p for p in (256, 128, 64, 32, 16, 8) if r % p == 0 and (p * c * 4 <= (2 << 20) or p == 8))

    def body(g_ref, w_ref, m_ref, v_ref, d_ref, nm_ref, nv_ref):
        gv = g_ref[...]
        mn = ADAM_B1 * m_ref[...] + (1.0 - ADAM_B1) * gv
        vn = ADAM_B2 * v_ref[...] + (1.0 - ADAM_B2) * (gv * gv)
        m_hat = mn / (1.0 - ADAM_B1 ** ADAM_STEP)
        v_hat = vn / (1.0 - ADAM_B2 ** ADAM_STEP)
        d_ref[...] = -ADAM_LR * (m_hat / (jnp.sqrt(v_hat) + ADAM_EPS) + ADAM_WD * w_ref[...])
        nm_ref[...] = mn
        nv_ref[...] = vn

    blk = pl.BlockSpec((tr, c), lambda i: (i, 0))
    osd = jax.ShapeDtypeStruct((r, c), F32)
    return pl.pallas_call(body, name=name, grid=(r // tr,), in_specs=[blk] * 4, out_specs=[blk] * 3,
                          out_shape=[osd] * 3, compiler_params=_params())(g, w, m, v)


def _attn_probs(q, k, q0, s_len):
    tq = q.shape[0]
    sc = lax.dot_general(q, k, NT, preferred_element_type=F32) * (HEAD_DIM ** -0.5)
    dlt = (q0 + lax.broadcasted_iota(jnp.int32, (tq, s_len), 0)) - lax.broadcasted_iota(jnp.int32, (tq, s_len), 1)
    cnt = jnp.zeros((tq, s_len), F32)
    for window, dil in DILATED_CONFIGS:
        seen = (dlt >= 0) & (dlt <= window) & ((dlt & (dil - 1)) == 0)
        cnt = cnt + jnp.where(seen, 1.0, 0.0)
    live = cnt > 0.0
    m = jnp.max(jnp.where(live, sc, -jnp.inf), axis=-1, keepdims=True)
    p = cnt * jnp.exp(jnp.where(live, sc - m, -jnp.inf))
    return p / jnp.sum(p, axis=-1, keepdims=True)


def _attn_key_groups(nq):
    return next(g for g in (4, 2, 1) if nq % g == 0)


def _attn_fwd(proj, nh, s_len, name, comm):
    t = proj.shape[0]
    tq = min(256, s_len)
    nq = s_len // tq
    ng = _attn_key_groups(nq)
    per = nq // ng

    def body(q_ref, k_ref, v_ref, o_ref):
        qi = pl.program_id(2)
        for j in range(ng):
            klen = (j + 1) * per * tq

            @pl.when(qi // per == j)
            def _(klen=klen):
                p = _attn_probs(q_ref[...].astype(BF16), k_ref[:klen, :].astype(BF16), qi * tq, klen)
                o_ref[...] = jnp.dot(p.astype(BF16), v_ref[:klen, :].astype(BF16), preferred_element_type=F32)

    q_spec = pl.BlockSpec((tq, HEAD_DIM), lambda b, h, qi: (b * nq + qi, h))
    outs, couts = _call_carrying(
        body, name, (t // s_len, nh, nq),
        [q_spec, pl.BlockSpec((s_len, HEAD_DIM), lambda b, h, qi: (b, nh + h)),
         pl.BlockSpec((s_len, HEAD_DIM), lambda b, h, qi: (b, 2 * nh + h))],
        [q_spec], [jax.ShapeDtypeStruct((t, nh * HEAD_DIM), F32)], [], (proj, proj, proj), comm)
    return outs[0], couts


def _attn_bwd(proj, o, do, nh, s_len, name):
    t = proj.shape[0]
    tq = min(256, s_len)
    nq = s_len // tq
    ng = _attn_key_groups(nq)
    per = nq // ng
    scale = HEAD_DIM ** -0.5

    def body(q_ref, k_ref, v_ref, o_ref, do_ref, dq_ref, dk_ref, dv_ref, dk_acc, dv_acc):
        qi = pl.program_id(2)

        @pl.when(qi == 0)
        def _():
            dk_acc[...] = jnp.zeros(dk_acc.shape, F32)
            dv_acc[...] = jnp.zeros(dv_acc.shape, F32)

        for j in range(ng):
            klen = (j + 1) * per * tq

            @pl.when(qi // per == j)
            def _(klen=klen):
                q = q_ref[...].astype(BF16)
                k = k_ref[:klen, :].astype(BF16)
                p = _attn_probs(q, k, qi * tq, klen)
                dob = do_ref[...]
                dp = lax.dot_general(dob, v_ref[:klen, :].astype(BF16), NT, preferred_element_type=F32)
                delta = jnp.sum(dob.astype(F32) * o_ref[...], axis=-1, keepdims=True)
                ds = (p * (dp - delta)).astype(BF16)
                dq_ref[...] = (jnp.dot(ds, k, preferred_element_type=F32) * scale).astype(BF16)
                dk_acc[:klen, :] += lax.dot_general(ds, q, TN, preferred_element_type=F32) * scale
                dv_acc[:klen, :] += lax.dot_general(p.astype(BF16), dob, TN, preferred_element_type=F32)

        @pl.when(qi == nq - 1)
        def _():
            dk_ref[...] = dk_acc[...].astype(BF16)
            dv_ref[...] = dv_acc[...].astype(BF16)

    q_spec = pl.BlockSpec((tq, HEAD_DIM), lambda b, h, qi: (b * nq + qi, h))
    kv_out = pl.BlockSpec((s_len, HEAD_DIM), lambda b, h, qi: (b, h))
    osd = jax.ShapeDtypeStruct((t, nh * HEAD_DIM), BF16)
    return pl.pallas_call(
        body, name=name, grid=(t // s_len, nh, nq),
        in_specs=[q_spec, pl.BlockSpec((s_len, HEAD_DIM), lambda b, h, qi: (b, nh + h)),
                  pl.BlockSpec((s_len, HEAD_DIM), lambda b, h, qi: (b, 2 * nh + h)), q_spec, q_spec],
        out_specs=[q_spec, kv_out, kv_out], out_shape=[osd, osd, osd],
        scratch_shapes=[pltpu.VMEM((s_len, HEAD_DIM), F32), pltpu.VMEM((s_len, HEAD_DIM), F32)],
        compiler_params=_params())(proj, proj, proj, o, do)


def _conv_taps(x, w_ref, s_len):
    row = lax.broadcasted_iota(jnp.int32, x.shape, 0)
    c = w_ref[CONV_WIDTH - 1:CONV_WIDTH, :] * x
    for j in range(1, CONV_WIDTH):
        xs = jnp.where(row >= j, pltpu.roll(x, j, 0), 0.0)
        c = c + w_ref[CONV_WIDTH - 1 - j:CONV_WIDTH - j, :] * xs
    return c


def _conv_fwd(proj, conv8, col0, width, s_len, name):
    t = proj.shape[0]
    cb = _pick(width, (512, 256, 128))
    c0 = col0 // cb

    def body(x_ref, w_ref, o_ref):
        c = _conv_taps(x_ref[...], w_ref, s_len)
        o_ref[...] = c * _sigmoid(c)

    return pl.pallas_call(
        body, name=name, grid=(t // s_len, width // cb),
        in_specs=[pl.BlockSpec((s_len, cb), lambda b, j: (b, c0 + j)), pl.BlockSpec((8, cb), lambda b, j: (0, j))],
        out_specs=pl.BlockSpec((s_len, cb), lambda b, j: (b, j)),
        out_shape=jax.ShapeDtypeStruct((t, width), F32), compiler_params=_params())(proj, conv8)


def _conv_bwd(proj, conv8, du, col0, width, s_len, name):
    t = proj.shape[0]
    cb = _pick(width, (512, 256, 128))
    c0 = col0 // cb

    def body(x_ref, w_ref, du_ref, dx_ref, dw_ref):
        b = pl.program_id(1)
        x = x_ref[...]
        c = _conv_taps(x, w_ref, s_len)
        sg = _sigmoid(c)
        dc = du_ref[...] * (sg * (1.0 + c * (1.0 - sg)))
        row = lax.broadcasted_iota(jnp.int32, x.shape, 0)
        dx = w_ref[CONV_WIDTH - 1:CONV_WIDTH, :] * dc
        rows = [jnp.sum(dc * x, axis=0, keepdims=True)]
        for j in range(1, CONV_WIDTH):
            up = jnp.where(row < s_len - j, pltpu.roll(dc, s_len - j, 0), 0.0)
            dx = dx + w_ref[CONV_WIDTH - 1 - j:CONV_WIDTH - j, :] * up
            xs = jnp.where(row >= j, pltpu.roll(x, j, 0), 0.0)
            rows.append(jnp.sum(dc * xs, axis=0, keepdims=True))
        dx_ref[...] = dx.astype(BF16)
        part = jnp.concatenate(rows[::-1] + [jnp.zeros((8 - CONV_WIDTH, cb), F32)], axis=0)

        @pl.when(b == 0)
        def _():
            dw_ref[...] = part

        @pl.when(b > 0)
        def _():
            dw_ref[...] += part

    return pl.pallas_call(
        body, name=name, grid=(width // cb, t // s_len),
        in_specs=[pl.BlockSpec((s_len, cb), lambda j, b: (b, c0 + j)), pl.BlockSpec((8, cb), lambda j, b: (0, j)),
                  pl.BlockSpec((s_len, cb), lambda j, b: (b, j))],
        out_specs=[pl.BlockSpec((s_len, cb), lambda j, b: (b, j)), pl.BlockSpec((8, cb), lambda j, b: (0, j))],
        out_shape=[jax.ShapeDtypeStruct((t, width), BF16), jax.ShapeDtypeStruct((8, width), F32)],
        compiler_params=_params())(proj, conv8, du)


def _split_bf16(v):
    hi = v.astype(BF16)
    return hi, (v - hi.astype(F32)).astype(BF16)


def _bdot(a, b, dims):
    return lax.dot_general(a, b, (dims, ((0,), (0,))), preferred_element_type=F32)


def _dot3(a, b, dims, exact_a=False):
    ah, al = _split_bf16(a)
    bh, bl = _split_bf16(b)
    out = _bdot(ah, bh, dims) + _bdot(ah, bl, dims)
    return out if exact_a else out + _bdot(al, bh, dims)


@functools.partial(jax.custom_vjp, nondiff_argnums=(2,))
def _bmm(a, b, exact_a=False):
    return _dot3(a, b, ((2,), (1,)), exact_a)


def _bmm_fwd(a, b, exact_a):
    return _dot3(a, b, ((2,), (1,)), exact_a), (a, b)


def _bmm_bwd(exact_a, res, ct):
    a, b = res
    da = jnp.zeros_like(a) if exact_a else _dot3(ct, b, ((2,), (2,)))
    db = _dot3(a, ct, ((1,), (1,)), exact_a)
    return da, db


_bmm.defvjp(_bmm_fwd, _bmm_bwd)


@jax.custom_vjp
def _bmm_nt(a, b):
    return _bdot(a.astype(BF16), b.astype(BF16), ((2,), (2,)))


def _bmm_nt_fwd(a, b):
    return _bmm_nt(a, b), (a, b)


def _bmm_nt_bwd(res, ct):
    a, b = res
    ctb = ct.astype(BF16)
    return _bdot(ctb, b.astype(BF16), ((2,), (1,))), _bdot(ctb, a.astype(BF16), ((1,), (1,)))


_bmm_nt.defvjp(_bmm_nt_fwd, _bmm_nt_bwd)


def _unit_lower_inverse(nm):
    c = nm.shape[-1]
    eye = (lax.broadcasted_iota(jnp.int32, (c, c), 0) == lax.broadcasted_iota(jnp.int32, (c, c), 1)).astype(F32)
    x = -nm
    inv = eye[None] + x
    p = x
    for _ in range(int(math.log2(c)) - 1):
        p = _bmm(p, p)
        inv = inv + _bmm(inv, p)
    return inv


def _dn_chunk_terms(uq, uk, uv, a_col, b_col, alog, dtb):
    n, c, dh = uq.shape
    q = uq * lax.rsqrt(jnp.sum(uq * uq, axis=-1, keepdims=True) + EPS) * (HEAD_DIM ** -0.5)
    k = uk * lax.rsqrt(jnp.sum(uk * uk, axis=-1, keepdims=True) + EPS)
    beta = _sigmoid(b_col)
    xa = a_col + dtb
    g = -jnp.exp(alog) * (jnp.maximum(xa, 0.0) + jnp.log(1.0 + jnp.exp(-jnp.abs(xa))))
    ri = lax.broadcasted_iota(jnp.int32, (c, c), 0)
    ci = lax.broadcasted_iota(jnp.int32, (c, c), 1)
    incl = ri >= ci
    strict = ri > ci
    l_incl = jnp.broadcast_to(incl.astype(F32)[None], (n, c, c))
    gb = jnp.broadcast_to(g, (n, c, dh))
    l_sums = jnp.broadcast_to(jnp.concatenate([incl.astype(F32), jnp.ones((dh - c, c), F32)], axis=0)[None], (n, dh, c))
    sums = _bmm(l_sums, gb, True)
    gc, gtot = sums[:, :c], sums[:, c:2 * c]
    gdiff = _bmm(l_incl, jnp.broadcast_to(g, (n, c, c)) * strict.astype(F32)[None], True)
    decay = jnp.where(incl[None], jnp.exp(jnp.where(incl[None], gdiff, 0.0)), 0.0)
    kb = k * beta
    nm = jnp.where(strict[None], _bmm_nt(kb, k) * decay, 0.0)
    tinv = _unit_lower_inverse(nm)
    w = _bmm(tinv, kb * jnp.exp(gc))
    u = _bmm(tinv, uv * beta)
    qk = _bmm_nt(q, k) * decay
    q_dec = q * jnp.exp(gc)
    k_dec = k * jnp.exp(gtot - gc)
    g_last = jnp.exp(jnp.concatenate([gtot] * (dh // c), axis=1))
    return w, u, qk, q_dec, k_dec, g_last


DN_SUB = 8


def _dn_gather_inputs(uq_ref, uk_ref, uv_ref, ba_ref, prm_ref, h, nh, rows, nb):
    ba = ba_ref[rows, :]
    lane = lax.broadcasted_iota(jnp.int32, ba.shape, 1)
    b_col = jnp.sum(jnp.where(lane == h, ba, 0.0), axis=-1, keepdims=True).reshape(nb, CHUNK, 1)
    a_col = jnp.sum(jnp.where(lane == nh + h, ba, 0.0), axis=-1, keepdims=True).reshape(nb, CHUNK, 1)
    lane1 = lax.broadcasted_iota(jnp.int32, (1, LANE), 1)
    alog = jnp.sum(jnp.where(lane1 == h, prm_ref[1:2, :], 0.0), axis=-1, keepdims=True)
    dtb = jnp.sum(jnp.where(lane1 == h, prm_ref[2:3, :], 0.0), axis=-1, keepdims=True)
    shp = (nb, CHUNK, HEAD_DIM)
    return (uq_ref[rows, :].reshape(shp), uk_ref[rows, :].reshape(shp), uv_ref[rows, :].reshape(shp),
            a_col, b_col, alog, dtb)


def _dn_fill_terms(in_refs, h, nh, n, term_refs):
    nb = min(DN_SUB, n)

    def sub(i, carry):
        rows = pl.ds(pl.multiple_of(i * (nb * CHUNK), nb * CHUNK), nb * CHUNK)
        terms = _dn_chunk_terms(*_dn_gather_inputs(*in_refs, h, nh, rows, nb))
        for r, v in zip(term_refs, terms):
            r[pl.ds(i * nb, nb)] = v
        return carry

    lax.fori_loop(0, n // nb, sub, 0)


def _dn_scan(terms_refs, o_ref, st_ref, n):
    w_ref, u_ref, qk_ref, qd_ref, kd_ref, gl_ref = terms_refs

    def step(i, state):
        if st_ref is not None:
            st_ref[i] = state
        sb = state.astype(BF16)
        v_new = u_ref[i] - jnp.dot(w_ref[i].astype(BF16), sb, preferred_element_type=F32)
        vb = v_new.astype(BF16)
        o_ref[i] = (jnp.dot(qd_ref[i].astype(BF16), sb, preferred_element_type=F32)
                    + jnp.dot(qk_ref[i].astype(BF16), vb, preferred_element_type=F32))
        return state * gl_ref[i] + lax.dot_general(kd_ref[i].astype(BF16), vb, TN, preferred_element_type=F32)

    lax.fori_loop(0, n, step, jnp.zeros((HEAD_DIM, HEAD_DIM), F32), unroll=2)


def _dn_specs(nh, nh_a, s_len, zc0, bac):
    head = lambda off: pl.BlockSpec((s_len, HEAD_DIM), lambda b, h: (b, off + h))
    return dict(uq=head(0), uk=head(nh), uv=head(2 * nh), z=head(zc0),
                ba=pl.BlockSpec((s_len, LANE), lambda b, h: (b, bac)),
                prm=pl.BlockSpec((8, LANE), lambda b, h: (0, 0)), dout=head(nh_a), out=head(0))


def _dn_scratch(n, with_states):
    big = pltpu.VMEM((n, CHUNK, HEAD_DIM), F32)
    sc = [big, big, pltpu.VMEM((n, CHUNK, CHUNK), F32), big, big, pltpu.VMEM((n, HEAD_DIM, HEAD_DIM), F32), big]
    if with_states:
        sc.append(pltpu.VMEM((n, HEAD_DIM, HEAD_DIM), F32))
    return sc


def _dn_fwd(u, proj, prm, nh, s_len, zc0, bac, name, comm):
    t = u.shape[0]
    n = s_len // CHUNK
    sp = _dn_specs(nh, 0, s_len, zc0, bac)

    def body(uq_ref, uk_ref, uv_ref, z_ref, ba_ref, prm_ref, o_ref, *scr):
        h = pl.program_id(1)
        _dn_fill_terms((uq_ref, uk_ref, uv_ref, ba_ref, prm_ref), h, nh, n, scr[:6])
        _dn_scan(scr[:6], scr[6], None, n)
        o = scr[6][...].reshape(s_len, HEAD_DIM)
        z = z_ref[...]
        r = lax.rsqrt(jnp.mean(o * o, axis=-1, keepdims=True) + EPS)
        o_ref[...] = o * r * prm_ref[0:1, :] * (z * _sigmoid(z))

    outs, couts = _call_carrying(
        body, name, (t // s_len, nh), [sp["uq"], sp["uk"], sp["uv"], sp["z"], sp["ba"], sp["prm"]], [sp["out"]],
        [jax.ShapeDtypeStruct((t, nh * HEAD_DIM), F32)], _dn_scratch(n, False), (u, u, u, proj, proj, prm), comm)
    return outs[0], couts


def _dn_bwd(u, proj, prm, dcat, nh, nh_a, s_len, zc0, bac, name):
    t = u.shape[0]
    n = s_len // CHUNK
    sp = _dn_specs(nh, nh_a, s_len, zc0, bac)

    def body(uq_ref, uk_ref, uv_ref, z_ref, ba_ref, prm_ref, do_ref,
             duq_ref, duk_ref, duv_ref, dz_ref, dba_ref, dprm_ref, *scr):
        b, h = pl.program_id(0), pl.program_id(1)
        in_refs = (uq_ref, uk_ref, uv_ref, ba_ref, prm_ref)
        w_ref, u_ref, qk_ref, qd_ref, kd_ref, gl_ref, o_scr, st_ref = scr
        _dn_fill_terms(in_refs, h, nh, n, scr[:6])
        _dn_scan(scr[:6], o_scr, st_ref, n)

        o = o_scr[...].reshape(s_len, HEAD_DIM)
        z = z_ref[...]
        dout = do_ref[...].astype(F32)
        gain = prm_ref[0:1, :]
        sg = _sigmoid(z)
        sz = z * sg
        r = lax.rsqrt(jnp.mean(o * o, axis=-1, keepdims=True) + EPS)
        oh = o * r
        dgain = jnp.sum(dout * oh * sz, axis=0, keepdims=True)
        dz_ref[...] = (dout * oh * gain * (sg * (1.0 + z * (1.0 - sg)))).astype(BF16)
        doh = dout * gain * sz
        d_o = r * (doh - oh * jnp.mean(doh * oh, axis=-1, keepdims=True))
        o_scr[...] = d_o.reshape(n, CHUNK, HEAD_DIM)

        def step(j, ds):
            i = n - 1 - j
            st = st_ref[i]
            sb = st.astype(BF16)
            wi, qki, qdi, kdi, gli = w_ref[i], qk_ref[i], qd_ref[i], kd_ref[i], gl_ref[i]
            v_new = u_ref[i] - jnp.dot(wi.astype(BF16), sb, preferred_element_type=F32)
            vb = v_new.astype(BF16)
            don = o_scr[i].astype(BF16)
            dsb = ds.astype(BF16)
            dv = (lax.dot_general(qki.astype(BF16), don, TN, preferred_element_type=F32)
                  + jnp.dot(kdi.astype(BF16), dsb, preferred_element_type=F32))
            dvb = dv.astype(BF16)
            qd_ref[i] = lax.dot_general(don, sb, NT, preferred_element_type=F32)
            qk_ref[i] = lax.dot_general(don, vb, NT, preferred_element_type=F32)
            kd_ref[i] = lax.dot_general(vb, dsb, NT, preferred_element_type=F32)
            gl_ref[i] = ds * st
            u_ref[i] = dv
            w_ref[i] = -lax.dot_general(dvb, sb, NT, preferred_element_type=F32)
            return (ds * gli + lax.dot_general(qdi.astype(BF16), don, TN, preferred_element_type=F32)
                    - lax.dot_general(wi.astype(BF16), dvb, TN, preferred_element_type=F32))

        lax.fori_loop(0, n, step, jnp.zeros((HEAD_DIM, HEAD_DIM), F32), unroll=2)

        @pl.when(h == 0)
        def _():
            dba_ref[...] = jnp.zeros(dba_ref.shape, F32)

        nb = min(DN_SUB, n)

        def sub(i, carry):
            rows = pl.ds(pl.multiple_of(i * (nb * CHUNK), nb * CHUNK), nb * CHUNK)
            _, pull = jax.vjp(_dn_chunk_terms, *_dn_gather_inputs(*in_refs, h, nh, rows, nb))
            duq, duk, duv, da_col, db_col, dal, ddt = pull(tuple(r[pl.ds(i * nb, nb)] for r in scr[:6]))
            duq_ref[rows, :] = duq.reshape(nb * CHUNK, HEAD_DIM)
            duk_ref[rows, :] = duk.reshape(nb * CHUNK, HEAD_DIM)
            duv_ref[rows, :] = duv.reshape(nb * CHUNK, HEAD_DIM)
            lane = lax.broadcasted_iota(jnp.int32, (nb * CHUNK, LANE), 1)
            dba_ref[rows, :] += (jnp.where(lane == h, db_col.reshape(nb * CHUNK, 1), 0.0)
                                 + jnp.where(lane == nh + h, da_col.reshape(nb * CHUNK, 1), 0.0))
            return carry[0] + dal, carry[1] + ddt

        dalog, ddtb = lax.fori_loop(0, n // nb, sub, (jnp.zeros((1, 1), F32), jnp.zeros((1, 1), F32)))
        lane1 = lax.broadcasted_iota(jnp.int32, (1, LANE), 1)
        dprm = jnp.concatenate([dgain, jnp.where(lane1 == h, dalog, 0.0), jnp.where(lane1 == h, ddtb, 0.0),
                                jnp.zeros((5, LANE), F32)], axis=0)

        @pl.when((b == 0) & (h == 0))
        def _():
            dprm_ref[...] = dprm

        @pl.when((b > 0) | (h > 0))
        def _():
            dprm_ref[...] += dprm

    osd = jax.ShapeDtypeStruct((t, nh * HEAD_DIM), F32)
    return pl.pallas_call(
        body, name=name, grid=(t // s_len, nh),
        in_specs=[sp["uq"], sp["uk"], sp["uv"], sp["z"], sp["ba"], sp["prm"], sp["dout"]],
        out_specs=[sp["out"], sp["out"], sp["out"], sp["out"], pl.BlockSpec((s_len, LANE), lambda b, h: (b, 0)),
                   pl.BlockSpec((8, LANE), lambda b, h: (0, 0))],
        out_shape=[osd, osd, osd, jax.ShapeDtypeStruct((t, nh * HEAD_DIM), BF16),
                   jax.ShapeDtypeStruct((t, LANE), F32), jax.ShapeDtypeStruct((8, LANE), F32)],
        scratch_shapes=_dn_scratch(n, True), compiler_params=_params())(u, u, u, proj, proj, prm, dcat)


def _w_in_windows(ws):
    w0 = [(ws * k) // LANE * LANE for k in range(N_CHIPS)]
    sh = [ws * k - w0[k] for k in range(N_CHIPS)]
    ww = _ceil_to(max(sh) + ws, 2 * LANE)
    n_tiles = (w0[-1] + ww) // LANE
    tpw = ww // LANE
    tph = tpw // 2
    tab = np.zeros((7, n_tiles), np.int32)
    for t in range(n_tiles):
        ks = [k for k in range(N_CHIPS) if w0[k] // LANE <= t < w0[k] // LANE + tpw]
        k1 = ks[-1]
        lt = t - w0[k1] // LANE
        tab[0, t], tab[1, t], tab[2, t] = lt // tph, k1, lt % tph
        k2 = ks[0] if len(ks) > 1 else k1
        lt2 = t - w0[k2] // LANE
        tab[3, t], tab[4, t], tab[5, t] = lt2 // tph, k2, lt2 % tph
        tab[6, t] = 1 if len(ks) > 1 else 0
        assert len(ks) <= 2
    return w0, sh, ww, n_tiles, tab


def kernel(x, ffn1_norm, ffn1_w_gate, ffn1_w_up, ffn1_w_down, mix_norm, w_in, conv_w, a_log, dt_bias, dn_norm, w_out, ffn2_norm, ffn2_w_gate, ffn2_w_up, ffn2_w_down, final_norm, loss_target, m_ffn1_norm, m_ffn1_w_gate, m_ffn1_w_up, m_ffn1_w_down, m_mix_norm, m_w_in, m_conv_w, m_a_log, m_dt_bias, m_dn_norm, m_w_out, m_ffn2_norm, m_ffn2_w_gate, m_ffn2_w_up, m_ffn2_w_down, m_final_norm, v_ffn1_norm, v_ffn1_w_gate, v_ffn1_w_up, v_ffn1_w_down, v_mix_norm, v_w_in, v_conv_w, v_a_log, v_dt_bias, v_dn_norm, v_w_out, v_ffn2_norm, v_ffn2_w_gate, v_ffn2_w_up, v_ffn2_w_down, v_final_norm):
    bl, s_, d = x.shape
    t = bl * s_
    fs = ffn1_w_gate.shape[1]
    hp = _ceil_to(-(-fs // 2), LANE)
    ws = w_in.shape[1]
    d_mix = w_out.shape[0] * N_CHIPS
    d_attn = d_dn = d_mix // 2
    nh_d = d_dn // HEAD_DIM
    d_in = 3 * d_attn + 4 * d_dn + 2 * nh_d
    cs = conv_w.shape[1]
    assert ws * N_CHIPS == d_in and cs * N_CHIPS == 3 * d_dn

    xi, yi, ci = lax.axis_index("x"), lax.axis_index("y"), lax.axis_index("c")
    me = 2 * xi + yi
    cidx = jnp.reshape(ci, (1,)).astype(jnp.int32)
    meidx = jnp.reshape(me, (1,)).astype(jnp.int32)
    mcidx = jnp.stack([me, ci]).astype(jnp.int32)

    w0, sh, ww, n_tiles, tab = _w_in_windows(ws)
    shift = (ws * me) % LANE
    w_in_win_t = lax.dynamic_update_slice(jnp.zeros((ww, d), F32), w_in.T, (shift, jnp.int32(0)))
    rows_tr = math.gcd(hp, fs)
    conv_piece = jnp.pad(conv_w, ((0, 8 - CONV_WIDTH), (0, 0))).reshape(8, 2, cs // 2).transpose(1, 0, 2)
    z0 = jnp.int32(0)
    pieces = [
        _cast_split_cols_t(ffn1_w_gate.T, hp, meidx, "cast_g1"),
        _cast_split_cols_t(ffn1_w_up.T, hp, meidx, "cast_u1"),
        _cast_split_rows(ffn1_w_down, hp, rows_tr, meidx, "cast_d1"),
        _cast_split_cols_t(w_in_win_t, ww // 2, meidx, "cast_in"),
        _cast_split_rows(w_out, w_out.shape[0] // 2, w_out.shape[0] // 2, meidx, "cast_out"),
        _cast_split_cols_t(ffn2_w_gate.T, hp, meidx, "cast_g2"),
        _cast_split_cols_t(ffn2_w_up.T, hp, meidx, "cast_u2"),
        _cast_split_rows(ffn2_w_down, hp, rows_tr, meidx, "cast_d2"),
        lax.dynamic_update_slice(jnp.zeros((2, N_CHIPS, 8, cs // 2), F32), conv_piece[:, None], (z0, me, z0, z0)),
    ]
    p_g1, p_u1, p_d1, p_in, p_out, p_g2, p_u2, p_d2, p_conv = pieces
    npc = 8
    ident = lambda p: p
    cat_map = lambda p: 2 * (p % N_CHIPS) + p // N_CHIPS
    as_cols = lambda a: a.reshape(npc, d, hp)
    as_rows = lambda a: a.reshape(npc, hp, d)

    wg1, wu1 = _comm_call(_all_gather_comm([p_g1, p_u1]), "all_gather_first")
    wg1, wu1 = as_cols(wg1), as_cols(wu1)
    h0 = x.reshape(t, d)
    n1 = _rms_fwd(h0, ffn1_norm, "rms1")
    (a1, b1, s1), (wd1, win_all) = _ffn_up(n1, wg1, wu1, "ffn1_up", comm=_all_gather_comm([p_d1, p_in], 0.95))
    wd1 = as_rows(wd1)
    h1, (wout, conv_all) = _mm_pieces_resid(s1, wd1, h0, 0.5, ident, "ffn1_down",
                                            comm=_all_gather_comm([p_out, p_conv], 0.5))
    wout = wout.reshape(npc, w_out.shape[0] // 2, d)
    conv8 = conv_all.transpose(2, 1, 0, 3).reshape(8, 3 * d_dn)
    win_full = _combine_windows(win_all, jnp.asarray(tab), n_tiles, "combine_w_in")
    n2 = _rms_fwd(h1, mix_norm, "rms2")
    proj, (wg2,) = _mm2d(n2, win_full, NN, F32, "in_proj", (1024, 768, 4096), comm=_all_gather_comm([p_g2], 0.85))
    nh_a = d_attn // HEAD_DIM
    attn, (wu2,) = _attn_fwd(proj, nh_a, s_, "attn_fwd", _all_gather_comm([p_u2], 0.9))
    zc0 = (3 * d_attn + 3 * d_dn) // HEAD_DIM
    bac = (3 * d_attn + 4 * d_dn) // LANE
    row128 = lambda v: jnp.pad(v, (0, LANE - v.shape[0])).reshape(1, LANE)
    prm = jnp.concatenate([row128(dn_norm), row128(a_log), row128(dt_bias), jnp.zeros((5, LANE), F32)], axis=0)
    u_dn = _conv_fwd(proj, conv8, 3 * d_attn, 3 * d_dn, s_, "dn_conv")
    dn_out, (wd2,) = _dn_fwd(u_dn, proj, prm, nh_d, s_, zc0, bac, "dn_fwd", _all_gather_comm([p_d2], 0.75))
    wg2, wu2, wd2 = as_cols(wg2), as_cols(wu2), as_rows(wd2)
    cat_b = jnp.concatenate([attn, dn_out], axis=1).astype(BF16)
    h2 = _mm_pieces_resid(cat_b, wout, h1, 1.0, cat_map, "out_proj")
    n3 = _rms_fwd(h2, ffn2_norm, "rms3")
    a3, b3, s3 = _ffn_up(n3, wg2, wu2, "ffn2_up")
    h3 = _mm_pieces_resid(s3, wd2, h2, 0.5, ident, "ffn2_down")

    def rs_front(gs, tag):
        gs = [g.reshape((2, N_CHIPS * g.shape[-2], g.shape[-1])) for g in gs]
        from_sib = _sibling_take(gs, f"rs_sibling_take_{tag}")
        ps = [_pair_sum(g, l, cidx, f"rs_pair_sum_{tag}_{i}") for i, (g, l) in enumerate(zip(gs, from_sib))]
        return [p.reshape(N_CHIPS, p.shape[0] // N_CHIPS, p.shape[1]) for p in ps]

    dh3, dh3b, dwf_p, lsq_p = _final_loss(h3, final_norm, loss_target.reshape(t, d), "final_loss")
    da3, db3 = _ffn_bwd_hidden(dh3b, wd2, a3, b3, "ffn2_bwd_hidden")
    g_wd2 = _grad_rows_pieces(s3, dh3b, 0.5, ident, npc, "ffn2_grad_down")
    p_d2s = rs_front([g_wd2], "d2")
    (g_wg2, g_wu2), l_d2 = _grad_cols_pieces(n3, da3, db3, npc, "ffn2_grad_up", comm=_chip_all_to_all_comm(p_d2s))
    p_gu2 = rs_front([g_wg2, g_wu2], "gu2")
    dn3, l_gu2 = _ffn_bwd_input(da3, db3, wg2, wu2, "ffn2_bwd_input", comm=_chip_all_to_all_comm(p_gu2))
    dh2, dh2b, dw3_p = _rms_bwd(dn3, h2, ffn2_norm, dh3, "rms3_bwd")

    dcat = _mm_nt_pieces_out(dh2b, wout, cat_map, "out_proj_bwd")
    g_wout = _grad_rows_pieces(cat_b, dh2b, 1.0, cat_map, npc, "out_proj_grad")
    dq_a, dk_a, dv_a = _attn_bwd(proj, attn, dcat, nh_a, s_, "attn_bwd")
    duq, duk, duv, dz, dba, dprm = _dn_bwd(u_dn, proj, prm, dcat, nh_d, nh_a, s_, zc0, bac, "dn_bwd")
    dx_conv, dconv8 = _conv_bwd(proj, conv8, jnp.concatenate([duq, duk, duv], axis=1), 3 * d_attn, 3 * d_dn, s_,
                                "dn_conv_bwd")
    used = 3 * d_attn + 4 * d_dn + LANE
    dproj_b = jnp.concatenate([dq_a, dk_a, dv_a, dx_conv, dz, dba.astype(BF16),
                               jnp.zeros((t, proj.shape[1] - used), BF16)], axis=1)
    dconv, ddnn, dalog, ddtb = dconv8[:CONV_WIDTH], dprm[0, :dn_norm.shape[0]], dprm[1, :nh_d], dprm[2, :nh_d]
    g_win_full = _mm2d(n2, dproj_b, TN, BF16, "in_proj_grad", (1024, 2432, 512))
    wh = ww // 2
    g_win = jnp.stack([jnp.stack([g_win_full[:, w0[k] + wh * h: w0[k] + wh * (h + 1)] for k in range(N_CHIPS)])
                       for h in range(2)])
    p_ow = rs_front([g_wout, g_win], "ow")
    dn2, l_ow = _mm2d(dproj_b, win_full, NT, F32, "in_proj_bwd", (1024, 2048, 768), comm=_chip_all_to_all_comm(p_ow))
    dh1, dh1b, dwm_p = _rms_bwd(dn2, h1, mix_norm, dh2, "rms2_bwd")

    da1, db1 = _ffn_bwd_hidden(dh1b, wd1, a1, b1, "ffn1_bwd_hidden")
    g_wd1 = _grad_rows_pieces(s1, dh1b, 0.5, ident, npc, "ffn1_grad_down")
    p_d1s = rs_front([g_wd1], "d1")
    (g_wg1, g_wu1), l_d1 = _grad_cols_pieces(n1, da1, db1, npc, "ffn1_grad_up", comm=_chip_all_to_all_comm(p_d1s))
    p_gu1 = rs_front([g_wg1, g_wu1], "gu1")
    dn1, l_gu1 = _ffn_bwd_input(da1, db1, wg1, wu1, "ffn1_bwd_input", comm=_chip_all_to_all_comm(p_gu1))
    dh0, _, dw1_p = _rms_bwd(dn1, h0, ffn1_norm, dh1, "rms1_bwd")
    grad_x = dh0.reshape(bl, s_, d)

    pair = [p_gu1[0], p_gu1[1], p_d1s[0], p_ow[1], p_ow[0], p_gu2[0], p_gu2[1], p_d2s[0]]
    from_chips = [l_gu1[0], l_gu1[1], l_d1[0], l_ow[1], l_ow[0], l_gu2[0], l_gu2[1], l_d2[0]]
    halves = [_chip_sum(p, l, mcidx, f"rs_chip_sum_{i}") for i, (p, l) in enumerate(zip(pair, from_chips))]
    full = _sibling_join(halves, "rs_sibling_join")
    f_wg1, f_wu1, f_wd1, f_win, f_wout, f_wg2, f_wu2, f_wd2 = full

    unpad_cols = lambda f: jnp.concatenate([f[0], f[1][:, :fs - hp]], axis=1)
    unpad_rows = lambda f: f.reshape(2 * f.shape[1], f.shape[2])[:fs]
    gw = {
        "ffn1_w_gate": unpad_cols(f_wg1), "ffn1_w_up": unpad_cols(f_wu1), "ffn1_w_down": unpad_rows(f_wd1),
        "w_in": lax.dynamic_slice(jnp.concatenate([f_win[0], f_win[1]], axis=1), (jnp.int32(0), shift), (d, ws)),
        "w_out": f_wout.reshape(w_out.shape),
        "ffn2_w_gate": unpad_cols(f_wg2), "ffn2_w_up": unpad_cols(f_wu2), "ffn2_w_down": unpad_rows(f_wd2),
    }

    def lanes(v):
        v = v.reshape(-1)
        return jnp.pad(v, (0, _ceil_to(v.shape[0], LANE) - v.shape[0])).reshape(-1, LANE)

    small = [dw1_p.sum(0), dwm_p.sum(0), dw3_p.sum(0), dwf_p.sum(0), ddnn, dalog, ddtb,
             (0.5 / d) * jnp.sum(lsq_p).reshape(1), dconv]
    rows = [lanes(v) for v in small]
    offs = np.cumsum([0] + [r.shape[0] for r in rows])
    packed = jnp.concatenate(rows, axis=0)
    packed = jnp.pad(packed, ((0, _ceil_to(packed.shape[0], 8) - packed.shape[0]), (0, 0)))
    red = _allreduce_small(packed, "allreduce_small")
    take = lambda i, shape: red[offs[i]:offs[i + 1]].reshape(-1)[:int(np.prod(shape))].reshape(shape)
    gw["ffn1_norm"] = take(0, (d,))
    gw["mix_norm"] = take(1, (d,))
    gw["ffn2_norm"] = take(2, (d,))
    gw["final_norm"] = take(3, (d,))
    gw["dn_norm"] = take(4, dn_norm.shape)
    gw["a_log"] = take(5, a_log.shape)
    gw["dt_bias"] = take(6, dt_bias.shape)
    loss = take(7, (1,)).reshape(())
    gw["conv_w"] = lax.dynamic_slice(take(8, (CONV_WIDTH, 3 * d_dn)), (jnp.int32(0), me * cs), (CONV_WIDTH, cs))

    names = ['ffn1_norm', 'ffn1_w_gate', 'ffn1_w_up', 'ffn1_w_down', 'mix_norm', 'w_in', 'conv_w', 'a_log', 'dt_bias',
             'dn_norm', 'w_out', 'ffn2_norm', 'ffn2_w_gate', 'ffn2_w_up', 'ffn2_w_down', 'final_norm']
    wv = dict(zip(names, (ffn1_norm, ffn1_w_gate, ffn1_w_up, ffn1_w_down, mix_norm, w_in, conv_w, a_log, dt_bias,
                          dn_norm, w_out, ffn2_norm, ffn2_w_gate, ffn2_w_up, ffn2_w_down, final_norm)))
    mv = dict(zip(names, (m_ffn1_norm, m_ffn1_w_gate, m_ffn1_w_up, m_ffn1_w_down, m_mix_norm, m_w_in, m_conv_w, m_a_log,
                          m_dt_bias, m_dn_norm, m_w_out, m_ffn2_norm, m_ffn2_w_gate, m_ffn2_w_up, m_ffn2_w_down,
                          m_final_norm)))
    vv = dict(zip(names, (v_ffn1_norm, v_ffn1_w_gate, v_ffn1_w_up, v_ffn1_w_down, v_mix_norm, v_w_in, v_conv_w, v_a_log,
                          v_dt_bias, v_dn_norm, v_w_out, v_ffn2_norm, v_ffn2_w_gate, v_ffn2_w_up, v_ffn2_w_down,
                          v_final_norm)))
    delta, new_m, new_v = {}, {}, {}
    small_names = [n for n in names if wv[n].ndim == 1 or n == "conv_w"]
    transposed = ("ffn1_w_gate", "ffn1_w_up", "ffn2_w_gate", "ffn2_w_up")
    for n in names:
        if n in small_names:
            continue
        if n in transposed:
            res = _adamw(gw[n].T, wv[n].T, mv[n].T, vv[n].T, f"adamw_{n}")
            delta[n], new_m[n], new_v[n] = (r.T for r in res)
        else:
            delta[n], new_m[n], new_v[n] = _adamw(gw[n], wv[n], mv[n], vv[n], f"adamw_{n}")
    srows = {n: lanes(gw[n]).shape[0] for n in small_names}
    soffs = np.cumsum([0] + [srows[n] for n in small_names])
    stot = _ceil_to(int(soffs[-1]), 8)

    def pack(dct):
        p = jnp.concatenate([lanes(dct[n]) for n in small_names], axis=0)
        return jnp.pad(p, ((0, stot - p.shape[0]), (0, 0)))

    sd, sm, sv = _adamw(pack(gw), pack(wv), pack(mv), pack(vv), "adamw_small")
    for i, n in enumerate(small_names):
        cut = lambda p: p[soffs[i]:soffs[i + 1]].reshape(-1)[:wv[n].size].reshape(wv[n].shape)
        delta[n], new_m[n], new_v[n] = cut(sd), cut(sm), cut(sv)

    return (loss, grad_x, *[gw[n] for n in names], *[delta[n] for n in names], *[new_m[n] for n in names],
            *[new_v[n] for n in names])
```
